```python
import math
import jax, jax.numpy as jnp
from jax import lax
import numpy as np

D_MODEL = 2048
BATCH = 1
SEQ = 8192
DEPTH = 4

CHUNK = 64
N_EVEN = (DEPTH + 1) // 2
N_ODD = DEPTH // 2
DN_ALPHA = (2 * DEPTH) ** 0.25
DN_BETA = (8 * DEPTH) ** -0.25
LN_EPS = 1e-5

A_WIDTH = D_MODEL // 2
A_HEAD = 64
A_HEADS = A_WIDTH // A_HEAD
A_DECAY_LORA = 64
A_ICL_LORA = 64
A_GATE_LORA = 160
A_SIZES = (A_WIDTH, A_WIDTH, A_WIDTH, A_DECAY_LORA, A_ICL_LORA, A_GATE_LORA)
A_COLS = sum(A_SIZES)
A_GN_EPS = 64e-5

B_WIDTH = D_MODEL // 2
B_BLOCKS = 16
B_BLOCK = B_WIDTH // B_BLOCKS
B_CONV = 4
B_C = 8.0
B_COLS = 2 * B_WIDTH
EVEN_COLS = A_COLS + B_COLS

C_HEADS = 16
C_HEAD_DIM = 128
C_Q_RANK = 512
C_KV_RANK = 256
IDX_HEADS = 16
IDX_DIM = 64
TOPK_MAX = 256
Q_BLOCK = 128
ODD_SIZES = (C_Q_RANK, C_KV_RANK, IDX_DIM, IDX_HEADS)
ODD_COLS = sum(ODD_SIZES)
REL_BUCKETS = 32
REL_MAX_DIST = 128

D_FF = 5632
FFN_CONV = 3
PLE_DIM = 256

kernel_name = "hybrid_rwkv7_rglru_dsa_deepnorm_trunk"


def split_cols(z, sizes):
    return jnp.split(z, np.cumsum(sizes)[:-1].tolist(), axis=-1)


def layer_norm(x, g, b, eps=LN_EPS):
    xf = x.astype(jnp.float32)
    mu = jnp.mean(xf, -1, keepdims=True)
    var = jnp.mean(jnp.square(xf - mu), -1, keepdims=True)
    return ((xf - mu) * lax.rsqrt(var + eps) * g + b).astype(x.dtype)


def rms_norm(x, g, eps=1e-6):
    xf = x.astype(jnp.float32)
    return (xf * lax.rsqrt(jnp.mean(xf * xf, -1, keepdims=True) + eps) * g).astype(x.dtype)


def causal_dwconv(x, w, b):
    width, ch = w.shape
    y = lax.conv_general_dilated(x, w[:, None, :].astype(x.dtype), window_strides=(1,), padding=[(width - 1, 0)], dimension_numbers=('NWC', 'WIO', 'NWC'), feature_group_count=ch)
    return y + b


def token_shift(z):
    return jnp.pad(z, ((0, 0), (1, 0), (0, 0)))[:, :-1]


def t5_bucket(rel):
    nb = REL_BUCKETS // 2
    max_exact = nb // 2
    ret = jnp.where(rel > 0, nb, 0)
    n = jnp.abs(rel)
    nf = jnp.maximum(n, 1).astype(jnp.float32)
    large = max_exact + (jnp.log(nf / max_exact) / math.log(REL_MAX_DIST / max_exact) * (nb - max_exact)).astype(jnp.int32)
    large = jnp.minimum(large, nb - 1)
    return ret + jnp.where(n < max_exact, n, large)


def rwkv7_mix(z, mu, w0, w2, a0, a2, g2, k_k, k_a, r_k, gn_g, gn_b):
    bsz, seq, _ = z.shape
    f32 = jnp.float32
    z = z + (token_shift(z) - z) * mu
    r, k, v, w_lo, a_lo, g_lo = split_cols(z, A_SIZES)
    w = -jax.nn.softplus(-(w0 + jnp.tanh(w_lo) @ w2)) - 0.5
    decay = jnp.exp(-jnp.exp(w.astype(f32)))
    a = jax.nn.sigmoid(a0 + a_lo @ a2)
    g = jax.nn.sigmoid(g_lo) @ g2
    heads = lambda t: t.astype(f32).reshape(bsz, seq, A_HEADS, A_HEAD)
    kk = heads(k * k_k)
    kk = kk / jnp.maximum(jnp.sqrt(jnp.sum(kk * kk, -1, keepdims=True)), 1e-12)
    k = k * (1.0 + (a - 1.0) * k_a)
    r, k, v, a, decay = heads(r), heads(k), heads(v), heads(a), heads(decay)

    def step(state, inp):
        r_t, w_t, k_t, v_t, kk_t, a_t = inp
        sa = jnp.einsum('bhvk,bhk->bhv', state, -kk_t)
        state = state * w_t[:, :, None, :] + sa[..., None] * (kk_t * a_t)[:, :, None, :] + v_t[..., None] * k_t[:, :, None, :]
        return state, jnp.einsum('bhvk,bhk->bhv', state, r_t)

    xs = tuple(jnp.moveaxis(t, 1, 0) for t in (r, decay, k, v, kk, a))
    state0 = jnp.zeros((bsz, A_HEADS, A_HEAD, A_HEAD), f32)
    _, y = lax.scan(step, state0, xs)
    y = jnp.moveaxis(y, 0, 1)
    ym = jnp.mean(y, -1, keepdims=True)
    yv = jnp.mean(jnp.square(y - ym), -1, keepdims=True)
    y = ((y - ym) * lax.rsqrt(yv + A_GN_EPS)).reshape(bsz, seq, A_WIDTH) * gn_g + gn_b
    bonus = (jnp.sum(r * k * r_k, -1, keepdims=True) * v).reshape(bsz, seq, A_WIDTH)
    return ((y + bonus) * g).astype(z.dtype)


def rglru_mix(z, conv_w, conv_b, w_r, b_r, w_i, b_i, lam):
    bsz, seq, _ = z.shape
    f32 = jnp.float32
    xb, gate = jnp.split(z, 2, axis=-1)
    xc = causal_dwconv(xb, conv_w, conv_b)
    xh = xc.reshape(bsz, seq, B_BLOCKS, B_BLOCK)
    r = jax.nn.sigmoid(jnp.einsum('bsnd,nde->bsne', xh, w_r).reshape(bsz, seq, B_WIDTH) + b_r)
    i = jax.nn.sigmoid(jnp.einsum('bsnd,nde->bsne', xh, w_i).reshape(bsz, seq, B_WIDTH) + b_i)
    log_a = -B_C * r.astype(f32) * jax.nn.softplus(-lam.astype(f32))
    a = jnp.exp(log_a)
    u = jnp.sqrt(-jnp.expm1(2.0 * log_a)) * (i * xc).astype(f32)

    def combine(left, right):
        a_l, h_l = left
        a_r, h_r = right
        return a_l * a_r, a_r * h_l + h_r

    _, h = lax.associative_scan(combine, (a, u), axis=1)
    return (jax.nn.gelu(gate) * h.astype(z.dtype)).astype(z.dtype)


def dsa_mix(x, w_in, q_norm, kv_norm, w_uq, w_uk, w_uv, w_qidx, kidx_g, kidx_b, rel_bias):
    bsz, seq, _ = x.shape
    f32 = jnp.float32
    c_q, c_kv, k_idx, w_idx = split_cols(x @ w_in, ODD_SIZES)
    c_q = rms_norm(c_q, q_norm)
    c_kv = rms_norm(c_kv, kv_norm)
    q = (c_q @ w_uq).reshape(bsz, seq, C_HEADS, C_HEAD_DIM)
    q_abs = jnp.einsum('bshd,hrd->bshr', q, w_uk)
    q_idx = (c_q @ w_qidx).reshape(bsz, seq, IDX_HEADS, IDX_DIM)
    k_idx = layer_norm(k_idx, kidx_g, kidx_b)
    w_idx = w_idx * (IDX_HEADS ** -0.5)
    k_sel = min(TOPK_MAX, seq // 4)
    n_blk = seq // Q_BLOCK
    key_chunk = jnp.arange(seq) // CHUNK

    def blocks(t):
        return jnp.moveaxis(t.reshape(bsz, n_blk, Q_BLOCK, *t.shape[2:]), 1, 0)

    def attend_block(inp):
        q_abs_b, q_idx_b, w_idx_b, start = inp
        q_pos = start + jnp.arange(Q_BLOCK)
        q_chunk = q_pos // CHUNK
        allowed = key_chunk[None, :] <= q_chunk[:, None]
        dots = jnp.einsum('bqjd,bsd->bqjs', q_idx_b, k_idx) * (IDX_DIM ** -0.5)
        score = jnp.einsum('bqj,bqjs->bqs', w_idx_b, jax.nn.relu(dots)).astype(f32)
        score = jnp.where(allowed[None], score, -jnp.inf)
        _, idx = lax.top_k(score, k_sel)
        c_sel = jax.vmap(lambda c, ix: c[ix])(c_kv, idx)
        bias = rel_bias[t5_bucket(idx - q_pos[None, :, None])]
        logits = jnp.einsum('bqhr,bqkr->bhqk', q_abs_b, c_sel).astype(f32) * (C_HEAD_DIM ** -0.5) + jnp.moveaxis(bias, -1, 1).astype(f32)
        valid = (idx // CHUNK) <= q_chunk[None, :, None]
        logits = jnp.where(valid[:, None], logits, -jnp.inf)
        probs = jax.nn.softmax(logits, axis=-1).astype(c_sel.dtype)
        return jnp.einsum('bhqk,bqkr->bqhr', probs, c_sel)

    starts = jnp.arange(n_blk, dtype=jnp.int32) * Q_BLOCK
    o_lat = lax.map(attend_block, (blocks(q_abs), blocks(q_idx), blocks(w_idx), starts))
    o_lat = jnp.moveaxis(o_lat, 0, 1).reshape(bsz, seq, C_HEADS, C_KV_RANK)
    return jnp.einsum('bshr,hrd->bshd', o_lat, w_uv).reshape(bsz, seq, C_HEADS * C_HEAD_DIM)


def conv_ffn(x, w_up, conv_w, conv_b, w_down):
    h = causal_dwconv(x @ w_up, conv_w, conv_b)
    gate, up = jnp.split(h, 2, axis=-1)
    return (jax.nn.gelu(gate) * up) @ w_down


def setup_inputs(seed: int = 0) -> dict:
    key = jax.random.key(seed)
    keys = iter(jax.random.split(key, 64))
    f32 = jnp.float32
    E, O, L, D = N_EVEN, N_ODD, DEPTH, D_MODEL

    def nrm(shape, scale):
        return scale * jax.random.normal(next(keys), shape, f32)

    def gain(shape):
        return 1.0 + nrm(shape, 0.02)

    lam_s = jax.random.uniform(next(keys), (E, B_WIDTH), f32, 0.9, 0.999) ** (1.0 / B_C)
    w0 = jnp.linspace(-6.0, -1.0, A_WIDTH, dtype=f32)[None, :] + 0.5 + nrm((E, A_WIDTH), 0.1)
    return {
        "x": nrm((BATCH, SEQ, D), 1.0),
        "p": nrm((DEPTH, BATCH, SEQ, PLE_DIM), 1.0),
        "rel_bias": nrm((REL_BUCKETS, C_HEADS), 0.3),
        "ln1_g": gain((L, D)), "ln1_b": nrm((L, D), 0.02),
        "ln2_g": gain((L, D)), "ln2_b": nrm((L, D), 0.02),
        "ffn_w_up": nrm((L, D, 2 * D_FF), D ** -0.5),
        "ffn_conv_w": nrm((L, FFN_CONV, 2 * D_FF), FFN_CONV ** -0.5),
        "ffn_conv_b": nrm((L, 2 * D_FF), 0.02),
        "ffn_w_down": nrm((L, D_FF, D), DN_BETA * D_FF ** -0.5),
        "ple_w_proj": nrm((L, PLE_DIM, D), PLE_DIM ** -0.5),
        "ple_w_gate": nrm((L, D, D), D ** -0.5),
        "ev_w_in": nrm((E, D, EVEN_COLS), D ** -0.5),
        "ev_w_out": nrm((E, A_WIDTH + B_WIDTH, D), DN_BETA * (A_WIDTH + B_WIDTH) ** -0.5),
        "a_mu": jax.random.uniform(next(keys), (E, A_COLS), f32, 0.2, 0.8),
        "a_w0": w0,
        "a_w2": nrm((E, A_DECAY_LORA, A_WIDTH), 0.1 * A_DECAY_LORA ** -0.5),
        "a_a0": nrm((E, A_WIDTH), 0.1),
        "a_a2": nrm((E, A_ICL_LORA, A_WIDTH), 0.5 * A_ICL_LORA ** -0.5),
        "a_g2": nrm((E, A_GATE_LORA, A_WIDTH), A_GATE_LORA ** -0.5),
        "a_k_k": 0.85 + nrm((E, A_WIDTH), 0.02),
        "a_k_a": gain((E, A_WIDTH)),
        "a_r_k": nrm((E, A_HEADS, A_HEAD), 0.1),
        "a_gn_g": gain((E, A_WIDTH)), "a_gn_b": nrm((E, A_WIDTH), 0.02),
        "b_conv_w": nrm((E, B_CONV, B_WIDTH), B_CONV ** -0.5),
        "b_conv_b": nrm((E, B_WIDTH), 0.02),
        "b_w_r": nrm((E, B_BLOCKS, B_BLOCK, B_BLOCK), B_BLOCK ** -0.5),
        "b_b_r": nrm((E, B_WIDTH), 0.02),
        "b_w_i": nrm((E, B_BLOCKS, B_BLOCK, B_BLOCK), B_BLOCK ** -0.5),
        "b_b_i": nrm((E, B_WIDTH), 0.02),
        "b_lambda": jnp.log(lam_s) - jnp.log1p(-lam_s),
        "od_w_in": nrm((O, D, ODD_COLS), D ** -0.5),
        "od_w_out": nrm((O, C_HEADS * C_HEAD_DIM, D), DN_BETA * (C_HEADS * C_HEAD_DIM) ** -0.5),
        "c_q_norm": gain((O, C_Q_RANK)),
        "c_kv_norm": gain((O, C_KV_RANK)),
        "c_w_uq": nrm((O, C_Q_RANK, C_HEADS * C_HEAD_DIM), C_Q_RANK ** -0.5),
        "c_w_uk": nrm((O, C_HEADS, C_KV_RANK, C_HEAD_DIM), C_KV_RANK ** -0.5),
        "c_w_uv": nrm((O, C_HEADS, C_KV_RANK, C_HEAD_DIM), C_KV_RANK ** -0.5),
        "c_w_qidx": nrm((O, C_Q_RANK, IDX_HEADS * IDX_DIM), C_Q_RANK ** -0.5),
        "c_kidx_g": gain((O, IDX_DIM)), "c_kidx_b": nrm((O, IDX_DIM), 0.02),
    }


def reference(x, p, rel_bias, ln1_g, ln1_b, ln2_g, ln2_b, ffn_w_up, ffn_conv_w, ffn_conv_b, ffn_w_down, ple_w_proj, ple_w_gate, ev_w_in, ev_w_out, a_mu, a_w0, a_w2, a_a0, a_a2, a_g2, a_k_k, a_k_a, a_r_k, a_gn_g, a_gn_b, b_conv_w, b_conv_b, b_w_r, b_b_r, b_w_i, b_b_i, b_lambda, od_w_in, od_w_out, c_q_norm, c_kv_norm, c_w_uq, c_w_uk, c_w_uv, c_w_qidx, c_kidx_g, c_kidx_b):
    for layer in range(DEPTH):
        j = layer // 2
        if layer % 2 == 0:
            z = x @ ev_w_in[j]
            y_a = rwkv7_mix(z[..., :A_COLS], a_mu[j], a_w0[j], a_w2[j], a_a0[j], a_a2[j], a_g2[j], a_k_k[j], a_k_a[j], a_r_k[j], a_gn_g[j], a_gn_b[j])
            y_b = rglru_mix(z[..., A_COLS:], b_conv_w[j], b_conv_b[j], b_w_r[j], b_b_r[j], b_w_i[j], b_b_i[j], b_lambda[j])
            y = jnp.concatenate([y_a, y_b], axis=-1) @ ev_w_out[j]
        else:
            y = dsa_mix(x, od_w_in[j], c_q_norm[j], c_kv_norm[j], c_w_uq[j], c_w_uk[j], c_w_uv[j], c_w_qidx[j], c_kidx_g[j], c_kidx_b[j], rel_bias) @ od_w_out[j]
        x = layer_norm(DN_ALPHA * x + y, ln1_g[layer], ln1_b[layer])
        x = layer_norm(DN_ALPHA * x + conv_ffn(x, ffn_w_up[layer], ffn_conv_w[layer], ffn_conv_b[layer], ffn_w_down[layer]), ln2_g[layer], ln2_b[layer])
        x = x + jax.nn.sigmoid(x @ ple_w_gate[layer]) * (p[layer] @ ple_w_proj[layer])
    return x
```

```python
import functools
import math

import jax
import jax.numpy as jnp
import numpy as np
from jax import lax
from jax.experimental import pallas as pl
from jax.experimental.pallas import tpu as pltpu

F32 = jnp.float32
BF16 = jnp.bfloat16

D_MODEL = 2048
DEPTH = 4
CHUNK = 64
DN_ALPHA = (2 * DEPTH) ** 0.25
LN_EPS = 1e-5
A_WIDTH = 1024
A_HEAD = 64
A_HEADS = 16
A_DECAY_LORA = 64
A_ICL_LORA = 64
A_GATE_LORA = 160
A_LORA = A_DECAY_LORA + A_ICL_LORA + A_GATE_LORA
A_LORA_PAD = 384
A_COLS = 3 * A_WIDTH + A_LORA
A_GN_EPS = 64e-5
B_WIDTH = 1024
B_BLOCKS = 16
B_BLOCK = 64
B_CONV = 4
B_C = 8.0
C_HEADS = 16
C_HEAD_DIM = 128
C_Q_RANK = 512
C_KV_RANK = 256
IDX_HEADS = 16
IDX_DIM = 64
TOPK_MAX = 256
REL_BUCKETS = 32
REL_MAX_DIST = 128
D_FF = 5632
FFN_CONV = 3
PLE_DIM = 256

VMEM_LIMIT_BYTES = 56 * 1024 * 1024
NEG_BIG = -1e30
HALO = 8


def _cparams(*sem):
    return pltpu.CompilerParams(dimension_semantics=sem, vmem_limit_bytes=VMEM_LIMIT_BYTES)


def _split_bf16(a):
    hi = a.astype(BF16)
    lo = (a - hi.astype(F32)).astype(BF16)
    return hi, lo


def _dot(a, b, dims=(((1,), (0,)), ((), ()))):
    return lax.dot_general(a, b, dims, preferred_element_type=F32)


_NT = (((1,), (1,)), ((), ()))
_TN = (((0,), (0,)), ((), ()))
_NN = (((1,), (0,)), ((), ()))


def _dot3(a, b, dims=_NN):
    ah, al = _split_bf16(a)
    bh, bl = _split_bf16(b)
    return _dot(ah, bh, dims) + (_dot(ah, bl, dims) + _dot(al, bh, dims))


def _dot_exact_rhs(a, b_bf16, dims=_NN):
    hi = a.astype(BF16)
    r1 = a - hi.astype(F32)
    mid = r1.astype(BF16)
    lo = (r1 - mid.astype(F32)).astype(BF16)
    return _dot(hi, b_bf16, dims) + (_dot(mid, b_bf16, dims) + _dot(lo, b_bf16, dims))


def _layer_norm_rows(v, g, b, eps):
    mu = jnp.mean(v, axis=-1, keepdims=True)
    d = v - mu
    var = jnp.mean(d * d, axis=-1, keepdims=True)
    return d * lax.rsqrt(var + eps) * g + b


def _softplus(x):
    return jnp.maximum(x, 0.0) + jnp.log1p(jnp.exp(-jnp.abs(x)))


def _mm_kernel(a_ref, b_ref, o_ref):
    o_ref[...] = _dot(a_ref[...], b_ref[...]).astype(o_ref.dtype)


def matmul(a, b, *, tm, tn, out_dtype=F32):
    m, k = a.shape
    _, n = b.shape
    assert m % tm == 0 and n % tn == 0
    return pl.pallas_call(
        _mm_kernel,
        grid=(n // tn, m // tm),
        in_specs=[pl.BlockSpec((tm, k), lambda j, i: (i, 0)),
                  pl.BlockSpec((k, tn), lambda j, i: (0, j))],
        out_specs=pl.BlockSpec((tm, tn), lambda j, i: (i, j)),
        out_shape=jax.ShapeDtypeStruct((m, n), out_dtype),
        compiler_params=_cparams("parallel", "parallel"),
    )(a, b)


def _mm_ln_kernel(a_ref, b_ref, x_ref, g_ref, beta_ref, o_ref, ob_ref, acc_ref, *, nk):
    kk = pl.program_id(1)

    @pl.when(kk == 0)
    def _():
        acc_ref[...] = jnp.zeros_like(acc_ref)

    acc_ref[...] += _dot(a_ref[...], b_ref[...])

    @pl.when(kk == nk - 1)
    def _():
        v = DN_ALPHA * x_ref[...] + acc_ref[...]
        y = _layer_norm_rows(v, g_ref[...], beta_ref[...], LN_EPS)
        o_ref[...] = y
        ob_ref[...] = y.astype(BF16)


def matmul_residual_ln(a, b, x, g, beta, *, tm, tk):
    m, k = a.shape
    _, n = b.shape
    assert m % tm == 0 and k % tk == 0
    nk = k // tk
    return pl.pallas_call(
        functools.partial(_mm_ln_kernel, nk=nk),
        grid=(m // tm, nk),
        in_specs=[pl.BlockSpec((tm, tk), lambda i, kk: (i, kk)),
                  pl.BlockSpec((tk, n), lambda i, kk: (kk, 0)),
                  pl.BlockSpec((tm, n), lambda i, kk: (i, 0)),
                  pl.BlockSpec((1, n), lambda i, kk: (0, 0)),
                  pl.BlockSpec((1, n), lambda i, kk: (0, 0))],
        out_specs=[pl.BlockSpec((tm, n), lambda i, kk: (i, 0)),
                   pl.BlockSpec((tm, n), lambda i, kk: (i, 0))],
        out_shape=[jax.ShapeDtypeStruct((m, n), F32), jax.ShapeDtypeStruct((m, n), BF16)],
        scratch_shapes=[pltpu.VMEM((tm, n), F32)],
        compiler_params=_cparams("parallel", "arbitrary"),
    )(a, b, x, g.reshape(1, n), beta.reshape(1, n))


def _ple_kernel(xb_ref, wg_ref, p_ref, wp_ref, x_ref, o_ref, ob_ref):
    gate = jax.nn.sigmoid(_dot(xb_ref[...], wg_ref[...]))
    proj = _dot(p_ref[...], wp_ref[...])
    y = x_ref[...] + gate * proj
    o_ref[...] = y
    ob_ref[...] = y.astype(BF16)


def ple_update(xb, wg, pb, wp, x, *, tm, tn):
    m, d = xb.shape
    pd = pb.shape[1]
    return pl.pallas_call(
        _ple_kernel,
        grid=(d // tn, m // tm),
        in_specs=[pl.BlockSpec((tm, d), lambda j, i: (i, 0)),
                  pl.BlockSpec((d, tn), lambda j, i: (0, j)),
                  pl.BlockSpec((tm, pd), lambda j, i: (i, 0)),
                  pl.BlockSpec((pd, tn), lambda j, i: (0, j)),
                  pl.BlockSpec((tm, tn), lambda j, i: (i, j))],
        out_specs=[pl.BlockSpec((tm, tn), lambda j, i: (i, j)),
                   pl.BlockSpec((tm, tn), lambda j, i: (i, j))],
        out_shape=[jax.ShapeDtypeStruct((m, d), F32), jax.ShapeDtypeStruct((m, d), BF16)],
        compiler_params=_cparams("parallel", "parallel"),
    )(xb, wg, pb, wp, x)


def _ffn_mid_kernel(hg_ref, hu_ref, hg_halo_ref, hu_halo_ref, cwg_ref, cwu_ref, cbg_ref, cbu_ref,
                    o_ref, eg_ref, eu_ref, *, tm):
    i = pl.program_id(1)
    live = (i > 0).astype(F32)
    eg_ref[0:HALO, :] = hg_halo_ref[...] * live
    eu_ref[0:HALO, :] = hu_halo_ref[...] * live
    eg_ref[HALO:, :] = hg_ref[...]
    eu_ref[HALO:, :] = hu_ref[...]

    def conv(e_ref, w_ref, b_ref):
        acc = b_ref[...] + w_ref[FFN_CONV - 1:FFN_CONV, :] * e_ref[HALO:, :]
        for d in range(1, FFN_CONV):
            acc = acc + w_ref[FFN_CONV - 1 - d:FFN_CONV - d, :] * e_ref[HALO - d:HALO - d + tm, :]
        return acc

    gate = conv(eg_ref, cwg_ref, cbg_ref)
    up = conv(eu_ref, cwu_ref, cbu_ref)
    o_ref[...] = (jax.nn.gelu(gate) * up).astype(o_ref.dtype)


def ffn_mid(h, conv_w, conv_b, *, tm, tc):
    s, two_ff = h.shape
    ff = two_ff // 2
    nc = ff // tc
    hb = tm // HALO
    cb = conv_b.reshape(1, two_ff)
    return pl.pallas_call(
        functools.partial(_ffn_mid_kernel, tm=tm),
        grid=(nc, s // tm),
        in_specs=[pl.BlockSpec((tm, tc), lambda j, i: (i, j)),
                  pl.BlockSpec((tm, tc), lambda j, i: (i, j + nc)),
                  pl.BlockSpec((HALO, tc), lambda j, i: (jnp.maximum(i * hb - 1, 0), j)),
                  pl.BlockSpec((HALO, tc), lambda j, i: (jnp.maximum(i * hb - 1, 0), j + nc)),
                  pl.BlockSpec((FFN_CONV, tc), lambda j, i: (0, j)),
                  pl.BlockSpec((FFN_CONV, tc), lambda j, i: (0, j + nc)),
                  pl.BlockSpec((1, tc), lambda j, i: (0, j)),
                  pl.BlockSpec((1, tc), lambda j, i: (0, j + nc))],
        out_specs=pl.BlockSpec((tm, tc), lambda j, i: (i, j)),
        out_shape=jax.ShapeDtypeStruct((s, ff), BF16),
        scratch_shapes=[pltpu.VMEM((tm + HALO, tc), F32), pltpu.VMEM((tm + HALO, tc), F32)],
        compiler_params=_cparams("parallel", "parallel"),
    )(h, h, h, h, conv_w, conv_w, cb, cb)


def _shift_mix(z, halo, mu, live):
    prev = pltpu.roll(z, 1, 0)
    row0 = lax.broadcasted_iota(jnp.int32, z.shape, 0) == 0
    prev = jnp.where(row0, halo[HALO - 1:HALO, :] * live, prev)
    return z + (prev - z) * mu


def _rwkv_prep_kernel(z_ref, zh_ref, lo_ref, loh_ref, mu_ref, mulo_ref, w0_ref, w2_ref, a0_ref, a2_ref,
                      g2_ref, kk_ref, ka_ref, rk_ref, bd_ref,
                      r_o, lw_o, k_o, v_o, kap_o, b_o, g_o, bonus_o):
    live = (pl.program_id(0) > 0).astype(F32)
    z = _shift_mix(z_ref[...], zh_ref[...], mu_ref[...], live)
    lo = _shift_mix(lo_ref[...], loh_ref[...], mulo_ref[...], live)
    r = z[:, 0:A_WIDTH]
    k = z[:, A_WIDTH:2 * A_WIDTH]
    v = z[:, 2 * A_WIDTH:3 * A_WIDTH]
    w = -_softplus(-(w0_ref[...] + _dot(jnp.tanh(lo).astype(BF16), w2_ref[...]))) - 0.5
    lw = -jnp.exp(w)
    a = jax.nn.sigmoid(a0_ref[...] + _dot(lo.astype(BF16), a2_ref[...]))
    g = _dot(jax.nn.sigmoid(lo).astype(BF16), g2_ref[...])
    kk = k * kk_ref[...]
    bd = bd_ref[...]
    ssq = _dot_exact_rhs(kk * kk, bd)
    kap = kk / jnp.maximum(jnp.sqrt(ssq), 1e-12)
    k2 = k * (1.0 + (a - 1.0) * ka_ref[...])
    bonus = _dot_exact_rhs(r * k2 * rk_ref[...], bd) * v
    b = kap * a
    g_o[...] = g
    bonus_o[...] = bonus
    for h in range(A_HEADS):
        sl = slice(h * A_HEAD, (h + 1) * A_HEAD)
        r_o[h] = r[:, sl]
        lw_o[h] = lw[:, sl]
        k_o[h] = k2[:, sl]
        v_o[h] = v[:, sl]
        kap_o[h] = kap[:, sl]
        b_o[h] = b[:, sl]


def rwkv_prep(z_rkv, z_lo, mu_rkv, mu_lo, w0, w2p, a0, a2p, g2p, k_k, k_a, r_k, bd, *, tm):
    s = z_rkv.shape[0]
    hb = tm // HALO
    row = lambda i: (i, 0)
    halo = lambda i: (jnp.maximum(i * hb - 1, 0), 0)
    const = lambda i: (0, 0)
    hm = jax.ShapeDtypeStruct((A_HEADS, s, A_HEAD), F32)
    hm_spec = pl.BlockSpec((A_HEADS, tm, A_HEAD), lambda i: (0, i, 0))
    full = jax.ShapeDtypeStruct((s, A_WIDTH), F32)
    vec = lambda a: a.reshape(1, -1)
    return pl.pallas_call(
        _rwkv_prep_kernel,
        grid=(s // tm,),
        in_specs=[pl.BlockSpec((tm, 3 * A_WIDTH), row), pl.BlockSpec((HALO, 3 * A_WIDTH), halo),
                  pl.BlockSpec((tm, A_LORA_PAD), row), pl.BlockSpec((HALO, A_LORA_PAD), halo),
                  pl.BlockSpec((1, 3 * A_WIDTH), const), pl.BlockSpec((1, A_LORA_PAD), const),
                  pl.BlockSpec((1, A_WIDTH), const), pl.BlockSpec((A_LORA_PAD, A_WIDTH), const),
                  pl.BlockSpec((1, A_WIDTH), const), pl.BlockSpec((A_LORA_PAD, A_WIDTH), const),
                  pl.BlockSpec((A_LORA_PAD, A_WIDTH), const),
                  pl.BlockSpec((1, A_WIDTH), const), pl.BlockSpec((1, A_WIDTH), const),
                  pl.BlockSpec((1, A_WIDTH), const), pl.BlockSpec((A_WIDTH, A_WIDTH), const)],
        out_specs=[hm_spec] * 6 + [pl.BlockSpec((tm, A_WIDTH), row)] * 2,
        out_shape=[hm] * 6 + [full] * 2,
        compiler_params=_cparams("parallel"),
    )(z_rkv, z_rkv, z_lo, z_lo, vec(mu_rkv), vec(mu_lo), vec(w0), w2p, vec(a0), a2p, g2p,
      vec(k_k), vec(k_a), vec(r_k), bd)


def _tri_inverse(a, row, col):
    eye = (row == col).astype(F32)
    ad = jnp.where((row >> 3) == (col >> 3), a, 0.0)
    t = eye - ad
    a2 = _dot3(ad, ad)
    t = t + _dot3(t, a2)
    a4 = _dot3(a2, a2)
    t = t + _dot3(t, a4)
    for sh in (3, 4, 5):
        inner = (row >> sh) == (col >> sh)
        outer = (row >> (sh + 1)) == (col >> (sh + 1))
        aoff = jnp.where(jnp.logical_and(outer, jnp.logical_not(inner)), a, 0.0)
        t = t - _dot3(_dot3(t, aoff), t)
    return t


def _rwkv_chunk_kernel(r_ref, lw_ref, k_ref, v_ref, kap_ref, b_ref, y_ref, state_ref):
    c = CHUNK

    @pl.when(pl.program_id(0) == 0)
    def _():
        state_ref[...] = jnp.zeros_like(state_ref)

    row = lax.broadcasted_iota(jnp.int32, (c, c), 0)
    col = lax.broadcasted_iota(jnp.int32, (c, c), 1)
    tril = row >= col
    stril = row > col
    eye = row == col
    lower_ones = tril.astype(BF16)

    def head(h, carry):
        r, lw, k, v, kap, b = r_ref[h], lw_ref[h], k_ref[h], v_ref[h], kap_ref[h], b_ref[h]
        hi = lw.astype(BF16)
        r1 = lw - hi.astype(F32)
        mid = r1.astype(BF16)
        lo = (r1 - mid.astype(F32)).astype(BF16)
        ci = _dot(lower_ones, hi) + (_dot(lower_ones, mid) + _dot(lower_ones, lo))
        ce = ci - lw
        cend = ci[c - 1:c, :]
        gn = jnp.exp(-ci)
        gend = jnp.exp(cend - ci)
        kap_h = kap * jnp.exp(ce)
        r_h = r * jnp.exp(ci)
        b_h = b * gn
        k_h = k * gn
        b_t = b * gend
        k_t = k * gend
        p = _dot3(jnp.concatenate([kap_h, r_h], axis=0), jnp.concatenate([b_h, k_h], axis=0), _NT)
        a_ab = jnp.where(stril, p[:c, :c], 0.0)
        a_ak = jnp.where(stril, p[:c, c:], 0.0)
        r_b = jnp.where(tril, p[c:, :c], 0.0)
        r_k = jnp.where(tril, p[c:, c:], 0.0)
        t = _tri_inverse(a_ab, row, col)
        x = _dot3(t, jnp.concatenate([kap_h, _dot3(a_ak, v)], axis=1))
        m = state_ref[h]
        wm = _dot3(jnp.concatenate([x[:, :A_HEAD], r_h], axis=0), m)
        u = -(wm[:c] + x[:, A_HEAD:])
        uv = jnp.concatenate([u, v], axis=0)
        y = wm[c:] + _dot3(jnp.concatenate([r_b, r_k], axis=1), uv)
        decay_diag = jnp.where(eye, jnp.exp(cend), 0.0)
        state_ref[h] = _dot3(jnp.concatenate([b_t, k_t, decay_diag], axis=0),
                             jnp.concatenate([uv, m], axis=0), _TN)
        ym = jnp.mean(y, axis=-1, keepdims=True)
        yc = y - ym
        yv = jnp.mean(yc * yc, axis=-1, keepdims=True)
        y_ref[h] = yc * lax.rsqrt(yv + A_GN_EPS)
        return carry

    lax.fori_loop(0, A_HEADS, head, 0, unroll=2)


def rwkv_chunks(r, lw, k, v, kap, b):
    _, s, _ = r.shape
    spec = pl.BlockSpec((A_HEADS, CHUNK, A_HEAD), lambda n: (0, n, 0))
    return pl.pallas_call(
        _rwkv_chunk_kernel,
        grid=(s // CHUNK,),
        in_specs=[spec] * 6,
        out_specs=spec,
        out_shape=jax.ShapeDtypeStruct((A_HEADS, s, A_HEAD), F32),
        scratch_shapes=[pltpu.VMEM((A_HEADS, A_HEAD, A_HEAD), F32)],
        compiler_params=_cparams("arbitrary"),
    )(r, lw, k, v, kap, b)


def _rwkv_post_kernel(y_ref, bonus_ref, g_ref, gg_ref, gb_ref, o_ref):
    y = jnp.concatenate([y_ref[h] for h in range(A_HEADS)], axis=1)
    o_ref[...] = ((y * gg_ref[...] + gb_ref[...] + bonus_ref[...]) * g_ref[...]).astype(o_ref.dtype)


def rwkv_post(y, bonus, g, gn_g, gn_b, *, tm):
    _, s, _ = y.shape
    row = lambda i: (i, 0)
    const = lambda i: (0, 0)
    return pl.pallas_call(
        _rwkv_post_kernel,
        grid=(s // tm,),
        in_specs=[pl.BlockSpec((A_HEADS, tm, A_HEAD), lambda i: (0, i, 0)),
                  pl.BlockSpec((tm, A_WIDTH), row), pl.BlockSpec((tm, A_WIDTH), row),
                  pl.BlockSpec((1, A_WIDTH), const), pl.BlockSpec((1, A_WIDTH), const)],
        out_specs=pl.BlockSpec((tm, A_WIDTH), row),
        out_shape=jax.ShapeDtypeStruct((s, A_WIDTH), BF16),
        compiler_params=_cparams("parallel"),
    )(y, bonus, g, gn_g.reshape(1, -1), gn_b.reshape(1, -1))


def _rglru_kernel(xb_ref, gate_ref, halo_ref, cw_ref, cb_ref, wr_ref, br_ref, wi_ref, bi_ref, lam_ref,
                  o_ref, xe_ref, a_ref, u_ref, h_ref, carry_ref, *, tm):
    i = pl.program_id(0)

    @pl.when(i == 0)
    def _():
        carry_ref[...] = jnp.zeros_like(carry_ref)

    xe_ref[0:HALO, :] = halo_ref[...] * (i > 0).astype(F32)
    xe_ref[HALO:, :] = xb_ref[...]
    xc = cb_ref[...] + cw_ref[B_CONV - 1:B_CONV, :] * xe_ref[HALO:, :]
    for d in range(1, B_CONV):
        xc = xc + cw_ref[B_CONV - 1 - d:B_CONV - d, :] * xe_ref[HALO - d:HALO - d + tm, :]
    xcb = xc.astype(BF16)
    r = jax.nn.sigmoid(_dot(xcb, wr_ref[...]) + br_ref[...])
    gi = jax.nn.sigmoid(_dot(xcb, wi_ref[...]) + bi_ref[...])
    log_a = -B_C * r * _softplus(-lam_ref[...])
    a = jnp.exp(log_a)
    a_ref[...] = a
    u_ref[...] = jnp.sqrt(-jnp.tanh(log_a) * (a * a + 1.0)) * (gi * xc)

    def group(gidx, h):
        base = pl.multiple_of(gidx * HALO, HALO)
        a8 = a_ref[pl.ds(base, HALO), :]
        u8 = u_ref[pl.ds(base, HALO), :]
        rows = []
        for rr in range(HALO):
            h = a8[rr:rr + 1, :] * h + u8[rr:rr + 1, :]
            rows.append(h)
        h_ref[pl.ds(base, HALO), :] = jnp.concatenate(rows, axis=0)
        return h

    carry_ref[...] = lax.fori_loop(0, tm // HALO, group, carry_ref[...])
    o_ref[...] = (jax.nn.gelu(gate_ref[...]) * h_ref[...]).astype(o_ref.dtype)


def rglru(z_b, conv_w, conv_b, wr, b_r, wi, b_i, lam, *, tm):
    s = z_b.shape[0]
    hb = tm // HALO
    row = lambda i: (i, 0)
    const = lambda i: (0, 0)
    vec = lambda a: a.reshape(1, -1)
    return pl.pallas_call(
        functools.partial(_rglru_kernel, tm=tm),
        grid=(s // tm,),
        in_specs=[pl.BlockSpec((tm, B_WIDTH), row), pl.BlockSpec((tm, B_WIDTH), lambda i: (i, 1)),
                  pl.BlockSpec((HALO, B_WIDTH), lambda i: (jnp.maximum(i * hb - 1, 0), 0)),
                  pl.BlockSpec((B_CONV, B_WIDTH), const), pl.BlockSpec((1, B_WIDTH), const),
                  pl.BlockSpec((B_WIDTH, B_WIDTH), const), pl.BlockSpec((1, B_WIDTH), const),
                  pl.BlockSpec((B_WIDTH, B_WIDTH), const), pl.BlockSpec((1, B_WIDTH), const),
                  pl.BlockSpec((1, B_WIDTH), const)],
        out_specs=pl.BlockSpec((tm, B_WIDTH), row),
        out_shape=jax.ShapeDtypeStruct((s, B_WIDTH), BF16),
        scratch_shapes=[pltpu.VMEM((tm + HALO, B_WIDTH), F32), pltpu.VMEM((tm, B_WIDTH), F32),
                        pltpu.VMEM((tm, B_WIDTH), F32), pltpu.VMEM((tm, B_WIDTH), F32),
                        pltpu.VMEM((1, B_WIDTH), F32)],
        compiler_params=_cparams("arbitrary"),
    )(z_b, z_b, z_b, conv_w, vec(conv_b), wr, vec(b_r), wi, vec(b_i), vec(lam))


ODD_PAD = 1024
ODD_KIDX_AT = 768
ODD_WIDX_AT = 896
QB = 256
SCORE_SCALE = (IDX_HEADS ** -0.5) * (IDX_DIM ** -0.5)
INT_MIN = -(2 ** 31)
CHUNK_SHIFT = 6
CUT_BITS = 14


def _dsa_in_kernel(x_ref, w_ref, qn_ref, kvn_ref, kg_ref, kb_ref, cq_o, ckv_o, kidx_o, widx_o):
    acc = _dot(x_ref[...], w_ref[...])
    cq = acc[:, 0:C_Q_RANK]
    ckv = acc[:, C_Q_RANK:C_Q_RANK + C_KV_RANK]
    kidx = acc[:, ODD_KIDX_AT:ODD_KIDX_AT + IDX_DIM]
    widx = acc[:, ODD_WIDX_AT:ODD_WIDX_AT + IDX_HEADS]
    rms = lambda t, g: t * lax.rsqrt(jnp.mean(t * t, axis=-1, keepdims=True) + 1e-6) * g
    cq_o[...] = rms(cq, qn_ref[...]).astype(BF16)
    ckv_o[...] = rms(ckv, kvn_ref[...]).astype(BF16)
    kidx_o[...] = _layer_norm_rows(kidx, kg_ref[...], kb_ref[...], LN_EPS).astype(BF16)
    widx_o[...] = widx * SCORE_SCALE


def dsa_in(xb, w_pad, q_norm, kv_norm, kidx_g, kidx_b, *, tm):
    s, d = xb.shape
    row = lambda i: (i, 0)
    const = lambda i: (0, 0)
    vec = lambda a: a.reshape(1, -1)
    return pl.pallas_call(
        _dsa_in_kernel,
        grid=(s // tm,),
        in_specs=[pl.BlockSpec((tm, d), row), pl.BlockSpec((d, ODD_PAD), const),
                  pl.BlockSpec((1, C_Q_RANK), const), pl.BlockSpec((1, C_KV_RANK), const),
                  pl.BlockSpec((1, IDX_DIM), const), pl.BlockSpec((1, IDX_DIM), const)],
        out_specs=[pl.BlockSpec((tm, C_Q_RANK), row), pl.BlockSpec((tm, C_KV_RANK), row),
                   pl.BlockSpec((tm, IDX_DIM), row), pl.BlockSpec((tm, IDX_HEADS), row)],
        out_shape=[jax.ShapeDtypeStruct((s, C_Q_RANK), BF16), jax.ShapeDtypeStruct((s, C_KV_RANK), BF16),
                   jax.ShapeDtypeStruct((s, IDX_DIM), BF16), jax.ShapeDtypeStruct((s, IDX_HEADS), F32)],
        compiler_params=_cparams("parallel"),
    )(xb, w_pad, vec(q_norm), vec(kv_norm), vec(kidx_g), vec(kidx_b))


def _qabs_kernel(cq_ref, wuq_ref, wuk_ref, o_ref):
    q = _dot(cq_ref[...], wuq_ref[...]).astype(BF16)
    qa = _dot(q, wuk_ref[...], _NT)
    o_ref[...] = (qa * (C_HEAD_DIM ** -0.5)).astype(BF16)


def dsa_qabs(cq, w_uq, w_uk, *, tm):
    s = cq.shape[0]
    return pl.pallas_call(
        _qabs_kernel,
        grid=(s // tm, C_HEADS),
        in_specs=[pl.BlockSpec((tm, C_Q_RANK), lambda i, h: (i, 0)),
                  pl.BlockSpec((C_Q_RANK, C_HEAD_DIM), lambda i, h: (0, h)),
                  pl.BlockSpec((None, C_KV_RANK, C_HEAD_DIM), lambda i, h: (h, 0, 0))],
        out_specs=pl.BlockSpec((None, tm, C_KV_RANK), lambda i, h: (h, i, 0)),
        out_shape=jax.ShapeDtypeStruct((C_HEADS, s, C_KV_RANK), BF16),
        compiler_params=_cparams("parallel", "parallel"),
    )(cq, w_uq, w_uk)


def _sortable_key(score):
    bits = lax.bitcast_convert_type(score + 0.0, jnp.int32)
    return jnp.where(bits < 0, bits ^ jnp.int32(0x7FFFFFFF), bits)


def _dsa_attn_kernel(qabs_ref, qidx_ref, widx_ref, kidx_t_ref, ckv_ref, wuv_ref, bias_ref,
                     o_ref, keys_ref, acc_ref, m_ref, l_ref, *, k_sel):
    i = pl.program_id(0)
    n_tiles = i + 1
    q_local = lax.broadcasted_iota(jnp.int32, (QB, QB), 0)
    s_local = lax.broadcasted_iota(jnp.int32, (QB, QB), 1)
    allowed_diag = (s_local >> CHUNK_SHIFT) <= (q_local >> CHUNK_SHIFT)

    widx = widx_ref[...]
    w_cols = [jnp.broadcast_to(widx[:, j:j + 1], (QB, QB)) for j in range(IDX_HEADS)]

    def score_tile(t, carry):
        off = pl.multiple_of(t * QB, QB)
        kt = kidx_t_ref[:, pl.ds(off, QB)]
        sc = jnp.zeros((QB, QB), F32)
        for j in range(IDX_HEADS):
            d = _dot(qidx_ref[:, j * IDX_DIM:(j + 1) * IDX_DIM], kt)
            sc = sc + w_cols[j] * jnp.maximum(d, 0.0)
        key = _sortable_key(sc)
        key = jnp.where(jnp.logical_or(t < i, allowed_diag), key, jnp.int32(INT_MIN))
        keys_ref[:, pl.ds(off, QB)] = key
        return carry

    lax.fori_loop(0, n_tiles, score_tile, 0)

    def count(pred):
        def body(t, acc):
            off = pl.multiple_of(t * QB, QB)
            return acc + pred(keys_ref[:, pl.ds(off, QB)], t).astype(jnp.int32)
        acc = lax.fori_loop(0, n_tiles, body, jnp.zeros((QB, QB), jnp.int32))
        return jnp.sum(acc.astype(F32), axis=-1, keepdims=True).astype(jnp.int32)

    def thr_bit(it, thr):
        cand = thr + jnp.left_shift(jnp.int32(1), 31 - it)
        cand_b = jnp.broadcast_to(cand, (QB, QB))
        cnt = count(lambda kt, t: kt >= cand_b)
        return jnp.where(cnt >= k_sel, cand, thr)

    thr = lax.fori_loop(0, 32, thr_bit, jnp.full((QB, 1), INT_MIN, jnp.int32))
    thr_b = jnp.broadcast_to(thr, (QB, QB))

    def selected(kt, t, cut_b):
        pos = s_local + t * QB
        return jnp.logical_or(kt > thr_b, jnp.logical_and(kt == thr_b, pos < cut_b))

    def cut_bit(it, cut):
        cand = cut + jnp.left_shift(jnp.int32(1), CUT_BITS - 1 - it)
        cand_b = jnp.broadcast_to(cand, (QB, QB))
        cnt = count(lambda kt, t: selected(kt, t, cand_b))
        return jnp.where(cnt <= k_sel, cand, cut)

    cut = lax.fori_loop(0, CUT_BITS, cut_bit, jnp.zeros((QB, 1), jnp.int32))
    cut_b = jnp.broadcast_to(cut, (QB, QB))

    m_ref[...] = jnp.full(m_ref.shape, NEG_BIG, F32)
    l_ref[...] = jnp.zeros(l_ref.shape, F32)
    acc_ref[...] = jnp.zeros(acc_ref.shape, F32)

    def attend_tile(t, near):
        off = pl.multiple_of(t * QB, QB)
        sel = selected(keys_ref[:, pl.ds(off, QB)], t, cut_b)
        if near == 1:
            sel = jnp.logical_and(sel, allowed_diag)
        mask_add = jnp.where(sel, 0.0, NEG_BIG)
        kv = ckv_ref[pl.ds(off, QB), :]

        def head(h, carry):
            s = _dot(qabs_ref[h], kv, _NT) + mask_add
            if near is not None:
                s = s + bias_ref[near, h]
            m_old = m_ref[h]
            m_new = jnp.maximum(m_old, jnp.max(s, axis=-1, keepdims=True))
            alpha = jnp.exp(m_old - m_new)
            p = jnp.exp(s - m_new)
            l_ref[h] = alpha * l_ref[h] + jnp.sum(p, axis=-1, keepdims=True)
            acc_ref[h] = alpha * acc_ref[h] + _dot(p.astype(BF16), kv)
            m_ref[h] = m_new
            return carry

        lax.fori_loop(0, C_HEADS, head, 0)

    def far_tile(t, carry):
        attend_tile(t, None)
        return carry

    lax.fori_loop(0, jnp.maximum(i - 1, 0), far_tile, 0)

    @pl.when(i >= 1)
    def _():
        attend_tile(i - 1, 0)

    attend_tile(i, 1)

    for h in range(C_HEADS):
        o_lat = (acc_ref[h] / l_ref[h]).astype(BF16)
        o_ref[:, h * C_HEAD_DIM:(h + 1) * C_HEAD_DIM] = _dot(o_lat, wuv_ref[h]).astype(o_ref.dtype)


def dsa_attention(qabs, qidx, widx, kidx_t, ckv, w_uv, bias_near, *, k_sel):
    _, s, _ = qabs.shape
    full2 = lambda i: (0, 0)
    return pl.pallas_call(
        functools.partial(_dsa_attn_kernel, k_sel=k_sel),
        grid=(s // QB,),
        in_specs=[pl.BlockSpec((C_HEADS, QB, C_KV_RANK), lambda i: (0, i, 0)),
                  pl.BlockSpec((QB, IDX_HEADS * IDX_DIM), lambda i: (i, 0)),
                  pl.BlockSpec((QB, IDX_HEADS), lambda i: (i, 0)),
                  pl.BlockSpec((IDX_DIM, s), full2),
                  pl.BlockSpec((s, C_KV_RANK), full2),
                  pl.BlockSpec((C_HEADS, C_KV_RANK, C_HEAD_DIM), lambda i: (0, 0, 0)),
                  pl.BlockSpec((2, C_HEADS, QB, QB), lambda i: (0, 0, 0, 0))],
        out_specs=pl.BlockSpec((QB, C_HEADS * C_HEAD_DIM), lambda i: (i, 0)),
        out_shape=jax.ShapeDtypeStruct((s, C_HEADS * C_HEAD_DIM), BF16),
        scratch_shapes=[pltpu.VMEM((QB, s), jnp.int32),
                        pltpu.VMEM((C_HEADS, QB, C_KV_RANK), F32),
                        pltpu.VMEM((C_HEADS, QB, 1), F32),
                        pltpu.VMEM((C_HEADS, QB, 1), F32)],
        compiler_params=_cparams("parallel"),
    )(qabs, qidx, widx, kidx_t, ckv, w_uv, bias_near)


def _t5_bucket(rel):
    nb = REL_BUCKETS // 2
    max_exact = nb // 2
    ret = jnp.where(rel > 0, nb, 0)
    n = jnp.abs(rel)
    nf = jnp.maximum(n, 1).astype(jnp.float32)
    large = max_exact + (jnp.log(nf / max_exact) / math.log(REL_MAX_DIST / max_exact) * (nb - max_exact)).astype(jnp.int32)
    large = jnp.minimum(large, nb - 1)
    return ret + jnp.where(n < max_exact, n, large)


def _near_bias(rel_bias):
    ql = jnp.arange(QB)[:, None]
    sl = jnp.arange(QB)[None, :]
    rel = jnp.stack([sl - ql - QB, sl - ql])
    far = rel_bias[_t5_bucket(jnp.array(-2 * QB))]
    b = rel_bias[_t5_bucket(rel)] - far
    return jnp.moveaxis(b, -1, 1).astype(F32)


def _pad_rows(w, at, total):
    return jnp.zeros((total, w.shape[1]), w.dtype).at[at:at + w.shape[0]].set(w)


def _block_diag(w):
    n, d, e = w.shape
    eye = jnp.eye(n, dtype=w.dtype)
    return (eye[:, None, :, None] * w[:, :, None, :]).reshape(n * d, n * e)


def _even_mixer(x, xb, w_in, w_out, mu, w0, w2, a0, a2, g2, k_k, k_a, r_k, gn_g, gn_b,
                conv_w, conv_b, w_r, b_r, w_i, b_i, lam, ln_g, ln_b, tm):
    w_in = w_in.astype(BF16)
    n_rkv = 3 * A_WIDTH
    z_rkv = matmul(xb, w_in[:, :n_rkv], tm=tm, tn=1024)
    w_lo = jnp.pad(w_in[:, n_rkv:A_COLS], ((0, 0), (0, A_LORA_PAD - A_LORA)))
    z_lo = matmul(xb, w_lo, tm=tm, tn=A_LORA_PAD)
    z_b = matmul(xb, w_in[:, A_COLS:], tm=tm, tn=1024)

    mu_lo = jnp.pad(mu[n_rkv:], (0, A_LORA_PAD - A_LORA))
    w2p = _pad_rows(w2, 0, A_LORA_PAD).astype(BF16)
    a2p = _pad_rows(a2, A_DECAY_LORA, A_LORA_PAD).astype(BF16)
    g2p = _pad_rows(g2, A_DECAY_LORA + A_ICL_LORA, A_LORA_PAD).astype(BF16)
    bd = _block_diag(jnp.ones((A_HEADS, A_HEAD, A_HEAD), BF16))
    r, lw, k2, v, kap, b, g, bonus = rwkv_prep(
        z_rkv, z_lo, mu[:n_rkv], mu_lo, w0, w2p, a0, a2p, g2p, k_k, k_a, r_k.reshape(-1), bd,
        tm=min(tm, 256))
    y = rwkv_chunks(r, lw, k2, v, kap, b)
    y_a = rwkv_post(y, bonus, g, gn_g, gn_b, tm=tm)

    y_b = rglru(z_b, conv_w, conv_b, _block_diag(w_r).astype(BF16), b_r,
                _block_diag(w_i).astype(BF16), b_i, lam, tm=min(tm, 256))
    y = jnp.concatenate([y_a, y_b], axis=1)
    return matmul_residual_ln(y, w_out.astype(BF16), x, ln_g, ln_b, tm=tm, tk=y.shape[1])


def _odd_mixer(x, xb, w_in, w_out, q_norm, kv_norm, w_uq, w_uk, w_uv, w_qidx, kidx_g, kidx_b,
               bias_near, ln_g, ln_b, tm):
    s = x.shape[0]
    d = w_in.shape[0]
    n_qkv = C_Q_RANK + C_KV_RANK
    w_pad = jnp.zeros((d, ODD_PAD), BF16)
    w_pad = w_pad.at[:, :n_qkv].set(w_in[:, :n_qkv].astype(BF16))
    w_pad = w_pad.at[:, ODD_KIDX_AT:ODD_KIDX_AT + IDX_DIM].set(w_in[:, n_qkv:n_qkv + IDX_DIM].astype(BF16))
    w_pad = w_pad.at[:, ODD_WIDX_AT:ODD_WIDX_AT + IDX_HEADS].set(w_in[:, n_qkv + IDX_DIM:].astype(BF16))
    cq, ckv, kidx, widx = dsa_in(xb, w_pad, q_norm, kv_norm, kidx_g, kidx_b, tm=tm)
    qabs = dsa_qabs(cq, w_uq.astype(BF16), w_uk.astype(BF16), tm=tm)
    qidx = matmul(cq, w_qidx.astype(BF16), tm=tm, tn=IDX_HEADS * IDX_DIM, out_dtype=BF16)
    o = dsa_attention(qabs, qidx, widx, kidx.T, ckv, w_uv.astype(BF16), bias_near,
                      k_sel=min(TOPK_MAX, s // 4))
    return matmul_residual_ln(o, w_out.astype(BF16), x, ln_g, ln_b, tm=tm, tk=o.shape[1])


def kernel(x, p, rel_bias, ln1_g, ln1_b, ln2_g, ln2_b, ffn_w_up, ffn_conv_w, ffn_conv_b, ffn_w_down, ple_w_proj, ple_w_gate, ev_w_in, ev_w_out, a_mu, a_w0, a_w2, a_a0, a_a2, a_g2, a_k_k, a_k_a, a_r_k, a_gn_g, a_gn_b, b_conv_w, b_conv_b, b_w_r, b_b_r, b_w_i, b_b_i, b_lambda, od_w_in, od_w_out, c_q_norm, c_kv_norm, c_w_uq, c_w_uk, c_w_uv, c_w_qidx, c_kidx_g, c_kidx_b):
    bsz, s, d = x.shape
    assert bsz == 1 and s % QB == 0 and s <= 2 ** CUT_BITS
    tm = min(512, s)
    x = x[0]
    xb = x.astype(BF16)
    bias_near = _near_bias(rel_bias)
    for layer in range(DEPTH):
        j = layer // 2
        if layer % 2 == 0:
            x, xb = _even_mixer(x, xb, ev_w_in[j], ev_w_out[j], a_mu[j], a_w0[j], a_w2[j], a_a0[j], a_a2[j],
                                a_g2[j], a_k_k[j], a_k_a[j], a_r_k[j], a_gn_g[j], a_gn_b[j],
                                b_conv_w[j], b_conv_b[j], b_w_r[j], b_b_r[j], b_w_i[j], b_b_i[j], b_lambda[j],
                                ln1_g[layer], ln1_b[layer], tm)
        else:
            x, xb = _odd_mixer(x, xb, od_w_in[j], od_w_out[j], c_q_norm[j], c_kv_norm[j], c_w_uq[j], c_w_uk[j],
                               c_w_uv[j], c_w_qidx[j], c_kidx_g[j], c_kidx_b[j], bias_near,
                               ln1_g[layer], ln1_b[layer], tm)
        h = matmul(xb, ffn_w_up[layer].astype(BF16), tm=tm, tn=1024)
        hm = ffn_mid(h, ffn_conv_w[layer], ffn_conv_b[layer], tm=tm, tc=512)
        x, xb = matmul_residual_ln(hm, ffn_w_down[layer].astype(BF16), x, ln2_g[layer], ln2_b[layer],
                                   tm=tm, tk=512)
        x, xb = ple_update(xb, ple_w_gate[layer].astype(BF16), p[layer, 0].astype(BF16),
                           ple_w_proj[layer].astype(BF16), x, tm=tm, tn=1024)
    return x[None]
```

```python
import functools
import math

import jax
import jax.numpy as jnp
import numpy as np
from jax import lax
from jax.experimental import pallas as pl
from jax.experimental.pallas import tpu as pltpu

F32 = jnp.float32
BF16 = jnp.bfloat16

D_MODEL = 2048
DEPTH = 4
CHUNK = 64
DN_ALPHA = (2 * DEPTH) ** 0.25
LN_EPS = 1e-5
A_WIDTH = 1024
A_HEAD = 64
A_HEADS = 16
A_DECAY_LORA = 64
A_ICL_LORA = 64
A_GATE_LORA = 160
A_LORA = A_DECAY_LORA + A_ICL_LORA + A_GATE_LORA
A_LORA_PAD = 384
A_COLS = 3 * A_WIDTH + A_LORA
A_GN_EPS = 64e-5
B_WIDTH = 1024
B_BLOCKS = 16
B_BLOCK = 64
B_CONV = 4
B_C = 8.0
C_HEADS = 16
C_HEAD_DIM = 128
C_Q_RANK = 512
C_KV_RANK = 256
IDX_HEADS = 16
IDX_DIM = 64
TOPK_MAX = 256
REL_BUCKETS = 32
REL_MAX_DIST = 128
D_FF = 5632
FFN_CONV = 3
PLE_DIM = 256

VMEM_LIMIT_BYTES = 56 * 1024 * 1024
NEG_BIG = -1e30
HALO = 8
LANES = 128


def _cparams(*sem):
    return pltpu.CompilerParams(dimension_semantics=sem, vmem_limit_bytes=VMEM_LIMIT_BYTES)


def _split_bf16(a):
    hi = a.astype(BF16)
    lo = (a - hi.astype(F32)).astype(BF16)
    return hi, lo


def _dot(a, b, dims=(((1,), (0,)), ((), ()))):
    return lax.dot_general(a, b, dims, preferred_element_type=F32)


_NT = (((1,), (1,)), ((), ()))
_TN = (((0,), (0,)), ((), ()))
_NN = (((1,), (0,)), ((), ()))


def _dot3(a, b, dims=_NN):
    ah, al = _split_bf16(a)
    bh, bl = _split_bf16(b)
    return _dot(ah, bh, dims) + (_dot(ah, bl, dims) + _dot(al, bh, dims))


def _dot_exact_rhs(a, b_bf16, dims=_NN):
    hi = a.astype(BF16)
    r1 = a - hi.astype(F32)
    mid = r1.astype(BF16)
    lo = (r1 - mid.astype(F32)).astype(BF16)
    return _dot(hi, b_bf16, dims) + (_dot(mid, b_bf16, dims) + _dot(lo, b_bf16, dims))


def _layer_norm_rows(v, g, b, eps):
    mu = jnp.mean(v, axis=-1, keepdims=True)
    d = v - mu
    var = jnp.mean(d * d, axis=-1, keepdims=True)
    return d * lax.rsqrt(var + eps) * g + b


def _softplus(x):
    return jnp.maximum(x, 0.0) + jnp.log1p(jnp.exp(-jnp.abs(x)))


def _mm_kernel(a_ref, b_ref, o_ref):
    o_ref[...] = _dot(a_ref[...], b_ref[...]).astype(o_ref.dtype)


def matmul(a, b, *, tm, tn, out_dtype=F32):
    m, k = a.shape
    _, n = b.shape
    assert m % tm == 0 and n % tn == 0
    return pl.pallas_call(
        _mm_kernel,
        grid=(n // tn, m // tm),
        in_specs=[pl.BlockSpec((tm, k), lambda j, i: (i, 0)),
                  pl.BlockSpec((k, tn), lambda j, i: (0, j))],
        out_specs=pl.BlockSpec((tm, tn), lambda j, i: (i, j)),
        out_shape=jax.ShapeDtypeStruct((m, n), out_dtype),
        compiler_params=_cparams("parallel", "parallel"),
    )(a, b)


def _mm_ln_kernel(a_ref, b_ref, x_ref, g_ref, beta_ref, o_ref, ob_ref, acc_ref, *, nk):
    kk = pl.program_id(1)

    @pl.when(kk == 0)
    def _():
        acc_ref[...] = jnp.zeros_like(acc_ref)

    acc_ref[...] += _dot(a_ref[...], b_ref[...])

    @pl.when(kk == nk - 1)
    def _():
        v = DN_ALPHA * x_ref[...] + acc_ref[...]
        y = _layer_norm_rows(v, g_ref[...], beta_ref[...], LN_EPS)
        o_ref[...] = y
        ob_ref[...] = y.astype(BF16)


def matmul_residual_ln(a, b, x, g, beta, *, tm, tk):
    m, k = a.shape
    _, n = b.shape
    assert m % tm == 0 and k % tk == 0
    nk = k // tk
    return pl.pallas_call(
        functools.partial(_mm_ln_kernel, nk=nk),
        grid=(m // tm, nk),
        in_specs=[pl.BlockSpec((tm, tk), lambda i, kk: (i, kk)),
                  pl.BlockSpec((tk, n), lambda i, kk: (kk, 0)),
                  pl.BlockSpec((tm, n), lambda i, kk: (i, 0)),
                  pl.BlockSpec((1, n), lambda i, kk: (0, 0)),
                  pl.BlockSpec((1, n), lambda i, kk: (0, 0))],
        out_specs=[pl.BlockSpec((tm, n), lambda i, kk: (i, 0)),
                   pl.BlockSpec((tm, n), lambda i, kk: (i, 0))],
        out_shape=[jax.ShapeDtypeStruct((m, n), F32), jax.ShapeDtypeStruct((m, n), BF16)],
        scratch_shapes=[pltpu.VMEM((tm, n), F32)],
        compiler_params=_cparams("parallel", "arbitrary"),
    )(a, b, x, g.reshape(1, n), beta.reshape(1, n))


def _ple_kernel(xb_ref, wg_ref, p_ref, wp_ref, x_ref, o_ref, ob_ref):
    gate = jax.nn.sigmoid(_dot(xb_ref[...], wg_ref[...]))
    proj = _dot(p_ref[...], wp_ref[...])
    y = x_ref[...] + gate * proj
    o_ref[...] = y
    ob_ref[...] = y.astype(BF16)


def ple_update(xb, wg, pb, wp, x, *, tm, tn):
    m, d = xb.shape
    pd = pb.shape[1]
    return pl.pallas_call(
        _ple_kernel,
        grid=(d // tn, m // tm),
        in_specs=[pl.BlockSpec((tm, d), lambda j, i: (i, 0)),
                  pl.BlockSpec((d, tn), lambda j, i: (0, j)),
                  pl.BlockSpec((tm, pd), lambda j, i: (i, 0)),
                  pl.BlockSpec((pd, tn), lambda j, i: (0, j)),
                  pl.BlockSpec((tm, tn), lambda j, i: (i, j))],
        out_specs=[pl.BlockSpec((tm, tn), lambda j, i: (i, j)),
                   pl.BlockSpec((tm, tn), lambda j, i: (i, j))],
        out_shape=[jax.ShapeDtypeStruct((m, d), F32), jax.ShapeDtypeStruct((m, d), BF16)],
        compiler_params=_cparams("parallel", "parallel"),
    )(xb, wg, pb, wp, x)


def _ffn_mid_kernel(hg_ref, hu_ref, hg_halo_ref, hu_halo_ref, cwg_ref, cwu_ref, cbg_ref, cbu_ref,
                    o_ref, eg_ref, eu_ref, *, tm):
    i = pl.program_id(1)
    live = (i > 0).astype(F32)
    eg_ref[0:HALO, :] = hg_halo_ref[...] * live
    eu_ref[0:HALO, :] = hu_halo_ref[...] * live
    eg_ref[HALO:, :] = hg_ref[...]
    eu_ref[HALO:, :] = hu_ref[...]

    def conv(e_ref, w_ref, b_ref):
        acc = b_ref[...] + w_ref[FFN_CONV - 1:FFN_CONV, :] * e_ref[HALO:, :]
        for d in range(1, FFN_CONV):
            acc = acc + w_ref[FFN_CONV - 1 - d:FFN_CONV - d, :] * e_ref[HALO - d:HALO - d + tm, :]
        return acc

    gate = conv(eg_ref, cwg_ref, cbg_ref)
    up = conv(eu_ref, cwu_ref, cbu_ref)
    o_ref[...] = (jax.nn.gelu(gate) * up).astype(o_ref.dtype)


def ffn_mid(h, conv_w, conv_b, *, tm, tc):
    s, two_ff = h.shape
    ff = two_ff // 2
    nc = ff // tc
    hb = tm // HALO
    cb = conv_b.reshape(1, two_ff)
    return pl.pallas_call(
        functools.partial(_ffn_mid_kernel, tm=tm),
        grid=(nc, s // tm),
        in_specs=[pl.BlockSpec((tm, tc), lambda j, i: (i, j)),
                  pl.BlockSpec((tm, tc), lambda j, i: (i, j + nc)),
                  pl.BlockSpec((HALO, tc), lambda j, i: (jnp.maximum(i * hb - 1, 0), j)),
                  pl.BlockSpec((HALO, tc), lambda j, i: (jnp.maximum(i * hb - 1, 0), j + nc)),
                  pl.BlockSpec((FFN_CONV, tc), lambda j, i: (0, j)),
                  pl.BlockSpec((FFN_CONV, tc), lambda j, i: (0, j + nc)),
                  pl.BlockSpec((1, tc), lambda j, i: (0, j)),
                  pl.BlockSpec((1, tc), lambda j, i: (0, j + nc))],
        out_specs=pl.BlockSpec((tm, tc), lambda j, i: (i, j)),
        out_shape=jax.ShapeDtypeStruct((s, ff), BF16),
        scratch_shapes=[pltpu.VMEM((tm + HALO, tc), F32), pltpu.VMEM((tm + HALO, tc), F32)],
        compiler_params=_cparams("parallel", "parallel"),
    )(h, h, h, h, conv_w, conv_w, cb, cb)


def _shift_mix(z, halo, mu, live):
    prev = pltpu.roll(z, 1, 0)
    row0 = lax.broadcasted_iota(jnp.int32, z.shape, 0) == 0
    prev = jnp.where(row0, halo[HALO - 1:HALO, :] * live, prev)
    return z + (prev - z) * mu


def _rwkv_prep_kernel(z_ref, zh_ref, lo_ref, loh_ref, mu_ref, mulo_ref, w0_ref, w2_ref, a0_ref, a2_ref,
                      g2_ref, kk_ref, ka_ref, rk_ref, bd_ref,
                      r_o, lw_o, k_o, v_o, kap_o, b_o, g_o, bonus_o):
    live = (pl.program_id(0) > 0).astype(F32)
    z = _shift_mix(z_ref[...], zh_ref[...], mu_ref[...], live)
    lo = _shift_mix(lo_ref[...], loh_ref[...], mulo_ref[...], live)
    r = z[:, 0:A_WIDTH]
    k = z[:, A_WIDTH:2 * A_WIDTH]
    v = z[:, 2 * A_WIDTH:3 * A_WIDTH]
    w = -_softplus(-(w0_ref[...] + _dot(jnp.tanh(lo).astype(BF16), w2_ref[...]))) - 0.5
    lw = -jnp.exp(w)
    a = jax.nn.sigmoid(a0_ref[...] + _dot(lo.astype(BF16), a2_ref[...]))
    g = _dot(jax.nn.sigmoid(lo).astype(BF16), g2_ref[...])
    kk = k * kk_ref[...]
    bd = bd_ref[...]
    ssq = _dot_exact_rhs(kk * kk, bd)
    kap = kk / jnp.maximum(jnp.sqrt(ssq), 1e-12)
    k2 = k * (1.0 + (a - 1.0) * ka_ref[...])
    bonus = _dot_exact_rhs(r * k2 * rk_ref[...], bd) * v
    b = kap * a
    g_o[...] = g
    bonus_o[...] = bonus
    for h in range(A_HEADS):
        sl = slice(h * A_HEAD, (h + 1) * A_HEAD)
        r_o[h] = r[:, sl]
        lw_o[h] = lw[:, sl]
        k_o[h] = k2[:, sl]
        v_o[h] = v[:, sl]
        kap_o[h] = kap[:, sl]
        b_o[h] = b[:, sl]


def rwkv_prep(z_rkv, z_lo, mu_rkv, mu_lo, w0, w2p, a0, a2p, g2p, k_k, k_a, r_k, bd, *, tm):
    s = z_rkv.shape[0]
    hb = tm // HALO
    row = lambda i: (i, 0)
    halo = lambda i: (jnp.maximum(i * hb - 1, 0), 0)
    const = lambda i: (0, 0)
    hm = jax.ShapeDtypeStruct((A_HEADS, s, A_HEAD), F32)
    hm_spec = pl.BlockSpec((A_HEADS, tm, A_HEAD), lambda i: (0, i, 0))
    full = jax.ShapeDtypeStruct((s, A_WIDTH), F32)
    vec = lambda a: a.reshape(1, -1)
    return pl.pallas_call(
        _rwkv_prep_kernel,
        grid=(s // tm,),
        in_specs=[pl.BlockSpec((tm, 3 * A_WIDTH), row), pl.BlockSpec((HALO, 3 * A_WIDTH), halo),
                  pl.BlockSpec((tm, A_LORA_PAD), row), pl.BlockSpec((HALO, A_LORA_PAD), halo),
                  pl.BlockSpec((1, 3 * A_WIDTH), const), pl.BlockSpec((1, A_LORA_PAD), const),
                  pl.BlockSpec((1, A_WIDTH), const), pl.BlockSpec((A_LORA_PAD, A_WIDTH), const),
                  pl.BlockSpec((1, A_WIDTH), const), pl.BlockSpec((A_LORA_PAD, A_WIDTH), const),
                  pl.BlockSpec((A_LORA_PAD, A_WIDTH), const),
                  pl.BlockSpec((1, A_WIDTH), const), pl.BlockSpec((1, A_WIDTH), const),
                  pl.BlockSpec((1, A_WIDTH), const), pl.BlockSpec((A_WIDTH, A_WIDTH), const)],
        out_specs=[hm_spec] * 6 + [pl.BlockSpec((tm, A_WIDTH), row)] * 2,
        out_shape=[hm] * 6 + [full] * 2,
        compiler_params=_cparams("parallel"),
    )(z_rkv, z_rkv, z_lo, z_lo, vec(mu_rkv), vec(mu_lo), vec(w0), w2p, vec(a0), a2p, g2p,
      vec(k_k), vec(k_a), vec(r_k), bd)


_BNN = (((2,), (1,)), ((0,), (0,)))
_BNT = (((2,), (2,)), ((0,), (0,)))
_BTN = (((1,), (1,)), ((0,), (0,)))


def _tri_inverse(a, row, col):
    mm = lambda p, q: _dot3(p, q, _BNN)
    eye = (row == col).astype(F32)
    ad = jnp.where((row >> 3) == (col >> 3), a, 0.0)
    t = eye - ad
    a2 = mm(ad, ad)
    t = t + mm(t, a2)
    a4 = mm(a2, a2)
    t = t + mm(t, a4)
    for sh in (3, 4, 5):
        inner = (row >> sh) == (col >> sh)
        outer = (row >> (sh + 1)) == (col >> (sh + 1))
        aoff = jnp.where(jnp.logical_and(outer, jnp.logical_not(inner)), a, 0.0)
        t = t - mm(mm(t, aoff), t)
    return t


def _rwkv_chunk_kernel(r_ref, lw_ref, k_ref, v_ref, kap_ref, b_ref, y_ref, state_ref):
    c = CHUNK

    @pl.when(pl.program_id(0) == 0)
    def _():
        state_ref[...] = jnp.zeros_like(state_ref)

    row = lax.broadcasted_iota(jnp.int32, (1, c, c), 1)
    col = lax.broadcasted_iota(jnp.int32, (1, c, c), 2)
    tril = row >= col
    stril = row > col
    eye = row == col
    lower_ones = jnp.broadcast_to(tril.astype(BF16), (A_HEADS, c, c))

    r, lw, k, v, kap, b = r_ref[...], lw_ref[...], k_ref[...], v_ref[...], kap_ref[...], b_ref[...]
    hi = lw.astype(BF16)
    r1 = lw - hi.astype(F32)
    mid = r1.astype(BF16)
    lo = (r1 - mid.astype(F32)).astype(BF16)
    ci = _dot(lower_ones, hi, _BNN) + (_dot(lower_ones, mid, _BNN) + _dot(lower_ones, lo, _BNN))
    ce = ci - lw
    cend = ci[:, c - 1:c, :]
    gn = jnp.exp(-ci)
    gend = jnp.exp(cend - ci)
    kap_h = kap * jnp.exp(ce)
    r_h = r * jnp.exp(ci)
    b_h = b * gn
    k_h = k * gn
    b_t = b * gend
    k_t = k * gend
    p = _dot3(jnp.concatenate([kap_h, r_h], axis=1), jnp.concatenate([b_h, k_h], axis=1), _BNT)
    a_ab = jnp.where(stril, p[:, :c, :c], 0.0)
    a_ak = jnp.where(stril, p[:, :c, c:], 0.0)
    r_b = jnp.where(tril, p[:, c:, :c], 0.0)
    r_k = jnp.where(tril, p[:, c:, c:], 0.0)
    t = _tri_inverse(a_ab, row, col)
    x = _dot3(t, jnp.concatenate([kap_h, _dot3(a_ak, v, _BNN)], axis=2), _BNN)
    m = state_ref[...]
    wm = _dot3(jnp.concatenate([x[:, :, :A_HEAD], r_h], axis=1), m, _BNN)
    u = -(wm[:, :c] + x[:, :, A_HEAD:])
    uv = jnp.concatenate([u, v], axis=1)
    y = wm[:, c:] + _dot3(jnp.concatenate([r_b, r_k], axis=2), uv, _BNN)
    decay_diag = jnp.where(eye, jnp.exp(cend), 0.0)
    state_ref[...] = _dot3(jnp.concatenate([b_t, k_t, decay_diag], axis=1),
                           jnp.concatenate([uv, m], axis=1), _BTN)
    ym = jnp.mean(y, axis=-1, keepdims=True)
    yc = y - ym
    yv = jnp.mean(yc * yc, axis=-1, keepdims=True)
    y_ref[...] = yc * lax.rsqrt(yv + A_GN_EPS)


def rwkv_chunks(r, lw, k, v, kap, b):
    _, s, _ = r.shape
    spec = pl.BlockSpec((A_HEADS, CHUNK, A_HEAD), lambda n: (0, n, 0))
    return pl.pallas_call(
        _rwkv_chunk_kernel,
        grid=(s // CHUNK,),
        in_specs=[spec] * 6,
        out_specs=spec,
        out_shape=jax.ShapeDtypeStruct((A_HEADS, s, A_HEAD), F32),
        scratch_shapes=[pltpu.VMEM((A_HEADS, A_HEAD, A_HEAD), F32)],
        compiler_params=_cparams("arbitrary"),
    )(r, lw, k, v, kap, b)


def _rwkv_post_kernel(y_ref, bonus_ref, g_ref, gg_ref, gb_ref, o_ref):
    y = jnp.concatenate([y_ref[h] for h in range(A_HEADS)], axis=1)
    o_ref[...] = ((y * gg_ref[...] + gb_ref[...] + bonus_ref[...]) * g_ref[...]).astype(o_ref.dtype)


def rwkv_post(y, bonus, g, gn_g, gn_b, *, tm):
    _, s, _ = y.shape
    row = lambda i: (i, 0)
    const = lambda i: (0, 0)
    return pl.pallas_call(
        _rwkv_post_kernel,
        grid=(s // tm,),
        in_specs=[pl.BlockSpec((A_HEADS, tm, A_HEAD), lambda i: (0, i, 0)),
                  pl.BlockSpec((tm, A_WIDTH), row), pl.BlockSpec((tm, A_WIDTH), row),
                  pl.BlockSpec((1, A_WIDTH), const), pl.BlockSpec((1, A_WIDTH), const)],
        out_specs=pl.BlockSpec((tm, A_WIDTH), row),
        out_shape=jax.ShapeDtypeStruct((s, A_WIDTH), BF16),
        compiler_params=_cparams("parallel"),
    )(y, bonus, g, gn_g.reshape(1, -1), gn_b.reshape(1, -1))


def _rglru_kernel(xb_ref, gate_ref, halo_ref, cw_ref, cb_ref, wr_ref, br_ref, wi_ref, bi_ref, lam_ref,
                  o_ref, xe_ref, a_ref, u_ref, h_ref, carry_ref, *, tm):
    i = pl.program_id(0)

    @pl.when(i == 0)
    def _():
        carry_ref[...] = jnp.zeros_like(carry_ref)

    xe_ref[0:HALO, :] = halo_ref[...] * (i > 0).astype(F32)
    xe_ref[HALO:, :] = xb_ref[...]
    xc = cb_ref[...] + cw_ref[B_CONV - 1:B_CONV, :] * xe_ref[HALO:, :]
    for d in range(1, B_CONV):
        xc = xc + cw_ref[B_CONV - 1 - d:B_CONV - d, :] * xe_ref[HALO - d:HALO - d + tm, :]
    xcb = xc.astype(BF16)
    r = jax.nn.sigmoid(_dot(xcb, wr_ref[...]) + br_ref[...])
    gi = jax.nn.sigmoid(_dot(xcb, wi_ref[...]) + bi_ref[...])
    log_a = -B_C * r * _softplus(-lam_ref[...])
    a = jnp.exp(log_a)
    a_ref[...] = a
    u_ref[...] = jnp.sqrt(-jnp.tanh(log_a) * (a * a + 1.0)) * (gi * xc)

    def group(gidx, h):
        base = pl.multiple_of(gidx * HALO, HALO)
        a8 = a_ref[pl.ds(base, HALO), :]
        u8 = u_ref[pl.ds(base, HALO), :]
        rows = []
        for rr in range(HALO):
            h = a8[rr:rr + 1, :] * h + u8[rr:rr + 1, :]
            rows.append(h)
        h_ref[pl.ds(base, HALO), :] = jnp.concatenate(rows, axis=0)
        return h

    carry_ref[...] = lax.fori_loop(0, tm // HALO, group, carry_ref[...])
    o_ref[...] = (jax.nn.gelu(gate_ref[...]) * h_ref[...]).astype(o_ref.dtype)


def rglru(z_b, conv_w, conv_b, wr, b_r, wi, b_i, lam, *, tm):
    s = z_b.shape[0]
    hb = tm // HALO
    row = lambda i: (i, 0)
    const = lambda i: (0, 0)
    vec = lambda a: a.reshape(1, -1)
    return pl.pallas_call(
        functools.partial(_rglru_kernel, tm=tm),
        grid=(s // tm,),
        in_specs=[pl.BlockSpec((tm, B_WIDTH), row), pl.BlockSpec((tm, B_WIDTH), lambda i: (i, 1)),
                  pl.BlockSpec((HALO, B_WIDTH), lambda i: (jnp.maximum(i * hb - 1, 0), 0)),
                  pl.BlockSpec((B_CONV, B_WIDTH), const), pl.BlockSpec((1, B_WIDTH), const),
                  pl.BlockSpec((B_WIDTH, B_WIDTH), const), pl.BlockSpec((1, B_WIDTH), const),
                  pl.BlockSpec((B_WIDTH, B_WIDTH), const), pl.BlockSpec((1, B_WIDTH), const),
                  pl.BlockSpec((1, B_WIDTH), const)],
        out_specs=pl.BlockSpec((tm, B_WIDTH), row),
        out_shape=jax.ShapeDtypeStruct((s, B_WIDTH), BF16),
        scratch_shapes=[pltpu.VMEM((tm + HALO, B_WIDTH), F32), pltpu.VMEM((tm, B_WIDTH), F32),
                        pltpu.VMEM((tm, B_WIDTH), F32), pltpu.VMEM((tm, B_WIDTH), F32),
                        pltpu.VMEM((1, B_WIDTH), F32)],
        compiler_params=_cparams("arbitrary"),
    )(z_b, z_b, z_b, conv_w, vec(conv_b), wr, vec(b_r), wi, vec(b_i), vec(lam))


ODD_PAD = 1024
ODD_KIDX_AT = 768
ODD_WIDX_AT = 896
QB = 256
SCORE_SCALE = (IDX_HEADS ** -0.5) * (IDX_DIM ** -0.5)
INT_MIN = -(2 ** 31)
CHUNK_SHIFT = 6
CUT_BITS = 14
SEL_ROWS = 128
HEAD_GROUP = 4


def _dsa_in_kernel(x_ref, w_ref, qn_ref, kvn_ref, kg_ref, kb_ref, cq_o, ckv_o, kidx_o, widx_o):
    acc = _dot(x_ref[...], w_ref[...])
    cq = acc[:, 0:C_Q_RANK]
    ckv = acc[:, C_Q_RANK:C_Q_RANK + C_KV_RANK]
    kidx = acc[:, ODD_KIDX_AT:ODD_KIDX_AT + IDX_DIM]
    widx = acc[:, ODD_WIDX_AT:ODD_WIDX_AT + IDX_HEADS]
    rms = lambda t, g: t * lax.rsqrt(jnp.mean(t * t, axis=-1, keepdims=True) + 1e-6) * g
    cq_o[...] = rms(cq, qn_ref[...]).astype(BF16)
    ckv_o[...] = rms(ckv, kvn_ref[...]).astype(BF16)
    kidx_o[...] = _layer_norm_rows(kidx, kg_ref[...], kb_ref[...], LN_EPS).astype(BF16)
    widx_o[...] = widx * SCORE_SCALE


def dsa_in(xb, w_pad, q_norm, kv_norm, kidx_g, kidx_b, *, tm):
    s, d = xb.shape
    row = lambda i: (i, 0)
    const = lambda i: (0, 0)
    vec = lambda a: a.reshape(1, -1)
    return pl.pallas_call(
        _dsa_in_kernel,
        grid=(s // tm,),
        in_specs=[pl.BlockSpec((tm, d), row), pl.BlockSpec((d, ODD_PAD), const),
                  pl.BlockSpec((1, C_Q_RANK), const), pl.BlockSpec((1, C_KV_RANK), const),
                  pl.BlockSpec((1, IDX_DIM), const), pl.BlockSpec((1, IDX_DIM), const)],
        out_specs=[pl.BlockSpec((tm, C_Q_RANK), row), pl.BlockSpec((tm, C_KV_RANK), row),
                   pl.BlockSpec((tm, IDX_DIM), row), pl.BlockSpec((tm, IDX_HEADS), row)],
        out_shape=[jax.ShapeDtypeStruct((s, C_Q_RANK), BF16), jax.ShapeDtypeStruct((s, C_KV_RANK), BF16),
                   jax.ShapeDtypeStruct((s, IDX_DIM), BF16), jax.ShapeDtypeStruct((s, IDX_HEADS), F32)],
        compiler_params=_cparams("parallel"),
    )(xb, w_pad, vec(q_norm), vec(kv_norm), vec(kidx_g), vec(kidx_b))


def _qabs_kernel(cq_ref, wuq_ref, wuk_ref, o_ref):
    q = _dot(cq_ref[...], wuq_ref[...]).astype(BF16)
    qa = _dot(q, wuk_ref[...], _NT)
    o_ref[...] = (qa * (C_HEAD_DIM ** -0.5)).astype(BF16)


def dsa_qabs(cq, w_uq, w_uk, *, tm):
    s = cq.shape[0]
    return pl.pallas_call(
        _qabs_kernel,
        grid=(s // tm, C_HEADS),
        in_specs=[pl.BlockSpec((tm, C_Q_RANK), lambda i, h: (i, 0)),
                  pl.BlockSpec((C_Q_RANK, C_HEAD_DIM), lambda i, h: (0, h)),
                  pl.BlockSpec((None, C_KV_RANK, C_HEAD_DIM), lambda i, h: (h, 0, 0))],
        out_specs=pl.BlockSpec((None, tm, C_KV_RANK), lambda i, h: (h, i, 0)),
        out_shape=jax.ShapeDtypeStruct((C_HEADS, s, C_KV_RANK), BF16),
        compiler_params=_cparams("parallel", "parallel"),
    )(cq, w_uq, w_uk)


def _sortable_key(score):
    bits = lax.bitcast_convert_type(score + 0.0, jnp.int32)
    return jnp.where(bits < 0, bits ^ jnp.int32(0x7FFFFFFF), bits)


def _dsa_attn_kernel(qabs_ref, qidx_ref, widx_ref, kidx_t_ref, ckv_ref, wuv_ref, bias_ref,
                     o_ref, keys_ref, acc_ref, m_ref, l_ref, *, k_sel):
    i = pl.program_id(0)
    n_tiles = i + 1
    q_local = lax.broadcasted_iota(jnp.int32, (QB, QB), 0)
    s_local = lax.broadcasted_iota(jnp.int32, (QB, QB), 1)
    allowed_diag = (s_local >> CHUNK_SHIFT) <= (q_local >> CHUNK_SHIFT)

    widx = widx_ref[...]
    w_cols = [jnp.broadcast_to(widx[:, j:j + 1], (QB, QB)) for j in range(IDX_HEADS)]

    def score_tile(t, carry):
        off = pl.multiple_of(t * QB, QB)
        kt = kidx_t_ref[:, pl.ds(off, QB)]
        sc = jnp.zeros((QB, QB), F32)
        for j in range(IDX_HEADS):
            d = _dot(qidx_ref[:, j * IDX_DIM:(j + 1) * IDX_DIM], kt)
            sc = sc + w_cols[j] * jnp.maximum(d, 0.0)
        key = _sortable_key(sc)
        key = jnp.where(jnp.logical_or(t < i, allowed_diag), key, jnp.int32(INT_MIN))
        keys_ref[:, pl.ds(off, QB)] = key
        return carry

    lax.fori_loop(0, n_tiles, score_tile, 0)

    half = QB // 2
    pos_local = lax.broadcasted_iota(jnp.int32, (SEL_ROWS, half), 1)

    def select_rows(r0):
        def count(pred):
            def body(t, acc):
                off = pl.multiple_of(t * QB, QB)
                lo = pred(keys_ref[r0:r0 + SEL_ROWS, pl.ds(off, half)], t * QB)
                hi = pred(keys_ref[r0:r0 + SEL_ROWS, pl.ds(off + half, half)], t * QB + half)
                return acc + (lo.astype(jnp.int32) + hi.astype(jnp.int32))
            acc = lax.fori_loop(0, n_tiles, body, jnp.zeros((SEL_ROWS, half), jnp.int32))
            return jnp.sum(acc.astype(F32), axis=-1, keepdims=True).astype(jnp.int32)

        def thr_bit(it, thr):
            cand = thr + jnp.left_shift(jnp.int32(1), 31 - it)
            cand_b = jnp.broadcast_to(cand, (SEL_ROWS, half))
            cnt = count(lambda kt, base: kt >= cand_b)
            return jnp.where(cnt >= k_sel, cand, thr)

        thr = lax.fori_loop(0, 32, thr_bit, jnp.full((SEL_ROWS, 1), INT_MIN, jnp.int32))
        thr_b = jnp.broadcast_to(thr, (SEL_ROWS, half))

        def cut_bit(it, cut):
            cand = cut + jnp.left_shift(jnp.int32(1), CUT_BITS - 1 - it)
            cand_b = jnp.broadcast_to(cand, (SEL_ROWS, half))
            cnt = count(lambda kt, base: jnp.logical_or(
                kt > thr_b, jnp.logical_and(kt == thr_b, pos_local + base < cand_b)))
            return jnp.where(cnt <= k_sel, cand, cut)

        n_ge = count(lambda kt, base: kt >= thr_b)
        tied = jnp.logical_and(n_ge > k_sel, thr > INT_MIN)
        any_tied = jnp.max(jnp.where(tied, 1.0, 0.0)) > 0.0
        no_cut = jnp.full((SEL_ROWS, 1), 2 ** CUT_BITS - 1, jnp.int32)
        cut = lax.cond(any_tied,
                       lambda: lax.fori_loop(0, CUT_BITS, cut_bit, jnp.zeros((SEL_ROWS, 1), jnp.int32)),
                       lambda: no_cut)
        return thr, cut

    parts = [select_rows(r0) for r0 in range(0, QB, SEL_ROWS)]
    thr_c = jnp.concatenate([p[0] for p in parts], axis=0)
    cut_c = jnp.concatenate([p[1] for p in parts], axis=0)
    rep_lanes = lambda a, n: jnp.concatenate([a] * (n // LANES), axis=-1)

    m_ref[...] = jnp.full(m_ref.shape, NEG_BIG, F32)
    l_ref[...] = jnp.zeros(l_ref.shape, F32)
    acc_ref[...] = jnp.zeros(acc_ref.shape, F32)

    def attend(off, width, near):
        kt = keys_ref[:, pl.ds(off, width)]
        thr_w = jnp.broadcast_to(thr_c, (QB, width))
        cut_w = jnp.broadcast_to(cut_c, (QB, width))
        pos = lax.broadcasted_iota(jnp.int32, (QB, width), 1) + off
        sel = jnp.logical_or(kt > thr_w, jnp.logical_and(kt == thr_w, pos < cut_w))
        if near == 1:
            sel = jnp.logical_and(sel, allowed_diag)
        mask_add = jnp.where(sel, 0.0, NEG_BIG)[None]
        kv = ckv_ref[pl.ds(off, width), :]
        kv_ones = jnp.concatenate([kv, jnp.ones((width, LANES), BF16)], axis=1)
        rep = lambda a, n: jnp.concatenate([a] * (n // LANES), axis=-1)

        for g in range(C_HEADS // HEAD_GROUP):
            hs = slice(g * HEAD_GROUP, (g + 1) * HEAD_GROUP)
            q = qabs_ref[hs].reshape(HEAD_GROUP * QB, C_KV_RANK)
            s = _dot(q, kv, _NT).reshape(HEAD_GROUP, QB, width) + mask_add
            if near is not None:
                s = s + bias_ref[near, hs]
            m_old = m_ref[hs]
            row_max = jnp.broadcast_to(jnp.max(s, axis=-1, keepdims=True), m_old.shape)
            m_new = jnp.maximum(m_old, row_max)
            alpha = jnp.exp(m_old - m_new)
            p = jnp.exp(s - rep(m_new, width)).astype(BF16)
            pv = _dot(p.reshape(HEAD_GROUP * QB, width), kv_ones).reshape(HEAD_GROUP, QB, C_KV_RANK + LANES)
            l_ref[hs] = alpha * l_ref[hs] + pv[:, :, C_KV_RANK:]
            acc_ref[hs] = rep(alpha, C_KV_RANK) * acc_ref[hs] + pv[:, :, :C_KV_RANK]
            m_ref[hs] = m_new

    n_far = jnp.maximum(i - 1, 0)

    def far_pair(t2, carry):
        attend(pl.multiple_of(t2 * (2 * QB), 2 * QB), 2 * QB, None)
        return carry

    lax.fori_loop(0, n_far // 2, far_pair, 0)

    @pl.when(n_far % 2 == 1)
    def _():
        attend(pl.multiple_of((n_far - 1) * QB, QB), QB, None)

    @pl.when(i >= 1)
    def _():
        attend(pl.multiple_of((i - 1) * QB, QB), QB, 0)

    attend(pl.multiple_of(i * QB, QB), QB, 1)

    for h in range(C_HEADS):
        o_lat = (acc_ref[h] / rep_lanes(l_ref[h], C_KV_RANK)).astype(BF16)
        o_ref[:, h * C_HEAD_DIM:(h + 1) * C_HEAD_DIM] = _dot(o_lat, wuv_ref[h]).astype(o_ref.dtype)


def dsa_attention(qabs, qidx, widx, kidx_t, ckv, w_uv, bias_near, *, k_sel):
    _, s, _ = qabs.shape
    full2 = lambda i: (0, 0)
    once = pl.Buffered(1)
    return pl.pallas_call(
        functools.partial(_dsa_attn_kernel, k_sel=k_sel),
        grid=(s // QB,),
        in_specs=[pl.BlockSpec((C_HEADS, QB, C_KV_RANK), lambda i: (0, i, 0)),
                  pl.BlockSpec((QB, IDX_HEADS * IDX_DIM), lambda i: (i, 0)),
                  pl.BlockSpec((QB, IDX_HEADS), lambda i: (i, 0)),
                  pl.BlockSpec((IDX_DIM, s), full2, pipeline_mode=once),
                  pl.BlockSpec((s, C_KV_RANK), full2, pipeline_mode=once),
                  pl.BlockSpec((C_HEADS, C_KV_RANK, C_HEAD_DIM), lambda i: (0, 0, 0), pipeline_mode=once),
                  pl.BlockSpec((2, C_HEADS, QB, QB), lambda i: (0, 0, 0, 0), pipeline_mode=once)],
        out_specs=pl.BlockSpec((QB, C_HEADS * C_HEAD_DIM), lambda i: (i, 0)),
        out_shape=jax.ShapeDtypeStruct((s, C_HEADS * C_HEAD_DIM), BF16),
        scratch_shapes=[pltpu.VMEM((QB, s), jnp.int32),
                        pltpu.VMEM((C_HEADS, QB, C_KV_RANK), F32),
                        pltpu.VMEM((C_HEADS, QB, LANES), F32),
                        pltpu.VMEM((C_HEADS, QB, LANES), F32)],
        compiler_params=_cparams("parallel"),
    )(qabs, qidx, widx, kidx_t, ckv, w_uv, bias_near)


def _t5_bucket(rel):
    nb = REL_BUCKETS // 2
    max_exact = nb // 2
    ret = jnp.where(rel > 0, nb, 0)
    n = jnp.abs(rel)
    nf = jnp.maximum(n, 1).astype(jnp.float32)
    large = max_exact + (jnp.log(nf / max_exact) / math.log(REL_MAX_DIST / max_exact) * (nb - max_exact)).astype(jnp.int32)
    large = jnp.minimum(large, nb - 1)
    return ret + jnp.where(n < max_exact, n, large)


def _near_bias(rel_bias):
    ql = jnp.arange(QB)[:, None]
    sl = jnp.arange(QB)[None, :]
    rel = jnp.stack([sl - ql - QB, sl - ql])
    far = rel_bias[_t5_bucket(jnp.array(-2 * QB))]
    b = rel_bias[_t5_bucket(rel)] - far
    return jnp.moveaxis(b, -1, 1).astype(F32)


def _pad_rows(w, at, total):
    return jnp.zeros((total, w.shape[1]), w.dtype).at[at:at + w.shape[0]].set(w)


def _block_diag(w):
    n, d, e = w.shape
    eye = jnp.eye(n, dtype=w.dtype)
    return (eye[:, None, :, None] * w[:, :, None, :]).reshape(n * d, n * e)


def _even_mixer(x, xb, w_in, w_out, mu, w0, w2, a0, a2, g2, k_k, k_a, r_k, gn_g, gn_b,
                conv_w, conv_b, w_r, b_r, w_i, b_i, lam, ln_g, ln_b, tm):
    w_in = w_in.astype(BF16)
    n_rkv = 3 * A_WIDTH
    z_rkv = matmul(xb, w_in[:, :n_rkv], tm=tm, tn=1024)
    w_lo = jnp.pad(w_in[:, n_rkv:A_COLS], ((0, 0), (0, A_LORA_PAD - A_LORA)))
    z_lo = matmul(xb, w_lo, tm=tm, tn=A_LORA_PAD)
    z_b = matmul(xb, w_in[:, A_COLS:], tm=tm, tn=1024)

    mu_lo = jnp.pad(mu[n_rkv:], (0, A_LORA_PAD - A_LORA))
    w2p = _pad_rows(w2, 0, A_LORA_PAD).astype(BF16)
    a2p = _pad_rows(a2, A_DECAY_LORA, A_LORA_PAD).astype(BF16)
    g2p = _pad_rows(g2, A_DECAY_LORA + A_ICL_LORA, A_LORA_PAD).astype(BF16)
    bd = _block_diag(jnp.ones((A_HEADS, A_HEAD, A_HEAD), BF16))
    r, lw, k2, v, kap, b, g, bonus = rwkv_prep(
        z_rkv, z_lo, mu[:n_rkv], mu_lo, w0, w2p, a0, a2p, g2p, k_k, k_a, r_k.reshape(-1), bd,
        tm=min(tm, 256))
    y = rwkv_chunks(r, lw, k2, v, kap, b)
    y_a = rwkv_post(y, bonus, g, gn_g, gn_b, tm=tm)

    y_b = rglru(z_b, conv_w, conv_b, _block_diag(w_r).astype(BF16), b_r,
                _block_diag(w_i).astype(BF16), b_i, lam, tm=min(tm, 256))
    y = jnp.concatenate([y_a, y_b], axis=1)
    return matmul_residual_ln(y, w_out.astype(BF16), x, ln_g, ln_b, tm=tm, tk=y.shape[1])


def _odd_mixer(x, xb, w_in, w_out, q_norm, kv_norm, w_uq, w_uk, w_uv, w_qidx, kidx_g, kidx_b,
               bias_near, ln_g, ln_b, tm):
    s = x.shape[0]
    d = w_in.shape[0]
    n_qkv = C_Q_RANK + C_KV_RANK
    w_pad = jnp.zeros((d, ODD_PAD), BF16)
    w_pad = w_pad.at[:, :n_qkv].set(w_in[:, :n_qkv].astype(BF16))
    w_pad = w_pad.at[:, ODD_KIDX_AT:ODD_KIDX_AT + IDX_DIM].set(w_in[:, n_qkv:n_qkv + IDX_DIM].astype(BF16))
    w_pad = w_pad.at[:, ODD_WIDX_AT:ODD_WIDX_AT + IDX_HEADS].set(w_in[:, n_qkv + IDX_DIM:].astype(BF16))
    cq, ckv, kidx, widx = dsa_in(xb, w_pad, q_norm, kv_norm, kidx_g, kidx_b, tm=tm)
    qabs = dsa_qabs(cq, w_uq.astype(BF16), w_uk.astype(BF16), tm=tm)
    qidx = matmul(cq, w_qidx.astype(BF16), tm=tm, tn=IDX_HEADS * IDX_DIM, out_dtype=BF16)
    o = dsa_attention(qabs, qidx, widx, kidx.T, ckv, w_uv.astype(BF16), bias_near,
                      k_sel=min(TOPK_MAX, s // 4))
    return matmul_residual_ln(o, w_out.astype(BF16), x, ln_g, ln_b, tm=tm, tk=o.shape[1])


def kernel(x, p, rel_bias, ln1_g, ln1_b, ln2_g, ln2_b, ffn_w_up, ffn_conv_w, ffn_conv_b, ffn_w_down, ple_w_proj, ple_w_gate, ev_w_in, ev_w_out, a_mu, a_w0, a_w2, a_a0, a_a2, a_g2, a_k_k, a_k_a, a_r_k, a_gn_g, a_gn_b, b_conv_w, b_conv_b, b_w_r, b_b_r, b_w_i, b_b_i, b_lambda, od_w_in, od_w_out, c_q_norm, c_kv_norm, c_w_uq, c_w_uk, c_w_uv, c_w_qidx, c_kidx_g, c_kidx_b):
    bsz, s, d = x.shape
    assert bsz == 1 and s % QB == 0 and s <= 2 ** CUT_BITS
    tm = min(512, s)
    x = x[0]
    xb = x.astype(BF16)
    bias_near = _near_bias(rel_bias)
    for layer in range(DEPTH):
        j = layer // 2
        if layer % 2 == 0:
            x, xb = _even_mixer(x, xb, ev_w_in[j], ev_w_out[j], a_mu[j], a_w0[j], a_w2[j], a_a0[j], a_a2[j],
                                a_g2[j], a_k_k[j], a_k_a[j], a_r_k[j], a_gn_g[j], a_gn_b[j],
                                b_conv_w[j], b_conv_b[j], b_w_r[j], b_b_r[j], b_w_i[j], b_b_i[j], b_lambda[j],
                                ln1_g[layer], ln1_b[layer], tm)
        else:
            x, xb = _odd_mixer(x, xb, od_w_in[j], od_w_out[j], c_q_norm[j], c_kv_norm[j], c_w_uq[j], c_w_uk[j],
                               c_w_uv[j], c_w_qidx[j], c_kidx_g[j], c_kidx_b[j], bias_near,
                               ln1_g[layer], ln1_b[layer], tm)
        h = matmul(xb, ffn_w_up[layer].astype(BF16), tm=tm, tn=1024)
        hm = ffn_mid(h, ffn_conv_w[layer], ffn_conv_b[layer], tm=tm, tc=512)
        x, xb = matmul_residual_ln(hm, ffn_w_down[layer].astype(BF16), x, ln2_g[layer], ln2_b[layer],
                                   tm=tm, tk=512)
        x, xb = ple_update(xb, ple_w_gate[layer].astype(BF16), p[layer, 0].astype(BF16),
                           ple_w_proj[layer].astype(BF16), x, tm=tm, tn=1024)
    return x[None]
```

```python
import functools
import math

import jax
import jax.numpy as jnp
import numpy as np
from jax import lax
from jax.experimental import pallas as pl
from jax.experimental.pallas import tpu as pltpu

F32 = jnp.float32
BF16 = jnp.bfloat16

D_MODEL = 2048
DEPTH = 4
CHUNK = 64
DN_ALPHA = (2 * DEPTH) ** 0.25
LN_EPS = 1e-5
A_WIDTH = 1024
A_HEAD = 64
A_HEADS = 16
A_DECAY_LORA = 64
A_ICL_LORA = 64
A_GATE_LORA = 160
A_LORA = A_DECAY_LORA + A_ICL_LORA + A_GATE_LORA
A_LORA_PAD = 384
A_COLS = 3 * A_WIDTH + A_LORA
A_GN_EPS = 64e-5
B_WIDTH = 1024
B_BLOCKS = 16
B_BLOCK = 64
B_CONV = 4
B_C = 8.0
C_HEADS = 16
C_HEAD_DIM = 128
C_Q_RANK = 512
C_KV_RANK = 256
IDX_HEADS = 16
IDX_DIM = 64
TOPK_MAX = 256
REL_BUCKETS = 32
REL_MAX_DIST = 128
D_FF = 5632
FFN_CONV = 3
PLE_DIM = 256

VMEM_LIMIT_BYTES = 56 * 1024 * 1024
NEG_BIG = -1e30
HALO = 8
LANES = 128
LOG2E = math.log2(math.e)


def _cparams(*sem):
    return pltpu.CompilerParams(dimension_semantics=sem, vmem_limit_bytes=VMEM_LIMIT_BYTES)


def _split_bf16(a):
    hi = a.astype(BF16)
    lo = (a - hi.astype(F32)).astype(BF16)
    return hi, lo


def _dot(a, b, dims=(((1,), (0,)), ((), ()))):
    return lax.dot_general(a, b, dims, preferred_element_type=F32)


_NT = (((1,), (1,)), ((), ()))
_TN = (((0,), (0,)), ((), ()))
_NN = (((1,), (0,)), ((), ()))


def _dot3(a, b, dims=_NN):
    ah, al = _split_bf16(a)
    bh, bl = _split_bf16(b)
    return _dot(ah, bh, dims) + (_dot(ah, bl, dims) + _dot(al, bh, dims))


def _dot_exact_rhs(a, b_bf16, dims=_NN):
    hi = a.astype(BF16)
    r1 = a - hi.astype(F32)
    mid = r1.astype(BF16)
    lo = (r1 - mid.astype(F32)).astype(BF16)
    return _dot(hi, b_bf16, dims) + (_dot(mid, b_bf16, dims) + _dot(lo, b_bf16, dims))


def _layer_norm_rows(v, g, b, eps):
    mu = jnp.mean(v, axis=-1, keepdims=True)
    d = v - mu
    var = jnp.mean(d * d, axis=-1, keepdims=True)
    return d * lax.rsqrt(var + eps) * g + b


def _softplus(x):
    return jnp.maximum(x, 0.0) + jnp.log1p(jnp.exp(-jnp.abs(x)))


def _mm_kernel(a_ref, b_ref, o_ref):
    o_ref[...] = _dot(a_ref[...], b_ref[...]).astype(o_ref.dtype)


def matmul(a, b, *, tm, tn, out_dtype=F32):
    m, k = a.shape
    _, n = b.shape
    assert m % tm == 0 and n % tn == 0
    return pl.pallas_call(
        _mm_kernel,
        grid=(n // tn, m // tm),
        in_specs=[pl.BlockSpec((tm, k), lambda j, i: (i, 0)),
                  pl.BlockSpec((k, tn), lambda j, i: (0, j))],
        out_specs=pl.BlockSpec((tm, tn), lambda j, i: (i, j)),
        out_shape=jax.ShapeDtypeStruct((m, n), out_dtype),
        compiler_params=_cparams("parallel", "parallel"),
    )(a, b)


def _mm_ln_kernel(a_ref, b_ref, x_ref, g_ref, beta_ref, o_ref, ob_ref, acc_ref, *, nk):
    kk = pl.program_id(1)

    @pl.when(kk == 0)
    def _():
        acc_ref[...] = jnp.zeros_like(acc_ref)

    acc_ref[...] += _dot(a_ref[...], b_ref[...])

    @pl.when(kk == nk - 1)
    def _():
        v = DN_ALPHA * x_ref[...] + acc_ref[...]
        y = _layer_norm_rows(v, g_ref[...], beta_ref[...], LN_EPS)
        o_ref[...] = y
        ob_ref[...] = y.astype(BF16)


def matmul_residual_ln(a, b, x, g, beta, *, tm, tk):
    m, k = a.shape
    _, n = b.shape
    assert m % tm == 0 and k % tk == 0
    nk = k // tk
    return pl.pallas_call(
        functools.partial(_mm_ln_kernel, nk=nk),
        grid=(m // tm, nk),
        in_specs=[pl.BlockSpec((tm, tk), lambda i, kk: (i, kk)),
                  pl.BlockSpec((tk, n), lambda i, kk: (kk, 0)),
                  pl.BlockSpec((tm, n), lambda i, kk: (i, 0)),
                  pl.BlockSpec((1, n), lambda i, kk: (0, 0)),
                  pl.BlockSpec((1, n), lambda i, kk: (0, 0))],
        out_specs=[pl.BlockSpec((tm, n), lambda i, kk: (i, 0)),
                   pl.BlockSpec((tm, n), lambda i, kk: (i, 0))],
        out_shape=[jax.ShapeDtypeStruct((m, n), F32), jax.ShapeDtypeStruct((m, n), BF16)],
        scratch_shapes=[pltpu.VMEM((tm, n), F32)],
        compiler_params=_cparams("parallel", "arbitrary"),
    )(a, b, x, g.reshape(1, n), beta.reshape(1, n))


def _ple_kernel(xb_ref, wg_ref, p_ref, wp_ref, x_ref, o_ref, ob_ref):
    gate = jax.nn.sigmoid(_dot(xb_ref[...], wg_ref[...]))
    proj = _dot(p_ref[...], wp_ref[...])
    y = x_ref[...] + gate * proj
    o_ref[...] = y
    ob_ref[...] = y.astype(BF16)


def ple_update(xb, wg, pb, wp, x, *, tm, tn):
    m, d = xb.shape
    pd = pb.shape[1]
    return pl.pallas_call(
        _ple_kernel,
        grid=(d // tn, m // tm),
        in_specs=[pl.BlockSpec((tm, d), lambda j, i: (i, 0)),
                  pl.BlockSpec((d, tn), lambda j, i: (0, j)),
                  pl.BlockSpec((tm, pd), lambda j, i: (i, 0)),
                  pl.BlockSpec((pd, tn), lambda j, i: (0, j)),
                  pl.BlockSpec((tm, tn), lambda j, i: (i, j))],
        out_specs=[pl.BlockSpec((tm, tn), lambda j, i: (i, j)),
                   pl.BlockSpec((tm, tn), lambda j, i: (i, j))],
        out_shape=[jax.ShapeDtypeStruct((m, d), F32), jax.ShapeDtypeStruct((m, d), BF16)],
        compiler_params=_cparams("parallel", "parallel"),
    )(xb, wg, pb, wp, x)


def _ffn_up_kernel(x_ref, wg_ref, wu_ref, cwg_ref, cwu_ref, cbg_ref, cbu_ref, o_ref, eg_ref, eu_ref, *, tm):
    i = pl.program_id(1)

    @pl.when(i == 0)
    def _():
        eg_ref[0:HALO, :] = jnp.zeros((HALO, eg_ref.shape[1]), F32)
        eu_ref[0:HALO, :] = jnp.zeros((HALO, eu_ref.shape[1]), F32)

    @pl.when(i > 0)
    def _():
        eg_ref[0:HALO, :] = eg_ref[tm:tm + HALO, :]
        eu_ref[0:HALO, :] = eu_ref[tm:tm + HALO, :]

    x = x_ref[...]
    eg_ref[HALO:, :] = _dot(x, wg_ref[...])
    eu_ref[HALO:, :] = _dot(x, wu_ref[...])

    def conv(e_ref, w_ref, b_ref):
        acc = b_ref[...] + w_ref[FFN_CONV - 1:FFN_CONV, :] * e_ref[HALO:, :]
        for d in range(1, FFN_CONV):
            acc = acc + w_ref[FFN_CONV - 1 - d:FFN_CONV - d, :] * e_ref[HALO - d:HALO - d + tm, :]
        return acc

    gate = conv(eg_ref, cwg_ref, cbg_ref)
    up = conv(eu_ref, cwu_ref, cbu_ref)
    o_ref[...] = (jax.nn.gelu(gate) * up).astype(o_ref.dtype)


def ffn_up(xb, w_up, conv_w, conv_b, *, tm, tc):
    s, d = xb.shape
    two_ff = w_up.shape[1]
    ff = two_ff // 2
    nc = ff // tc
    cb = conv_b.reshape(1, two_ff)
    return pl.pallas_call(
        functools.partial(_ffn_up_kernel, tm=tm),
        grid=(nc, s // tm),
        in_specs=[pl.BlockSpec((tm, d), lambda j, i: (i, 0)),
                  pl.BlockSpec((d, tc), lambda j, i: (0, j)),
                  pl.BlockSpec((d, tc), lambda j, i: (0, j + nc)),
                  pl.BlockSpec((FFN_CONV, tc), lambda j, i: (0, j)),
                  pl.BlockSpec((FFN_CONV, tc), lambda j, i: (0, j + nc)),
                  pl.BlockSpec((1, tc), lambda j, i: (0, j)),
                  pl.BlockSpec((1, tc), lambda j, i: (0, j + nc))],
        out_specs=pl.BlockSpec((tm, tc), lambda j, i: (i, j)),
        out_shape=jax.ShapeDtypeStruct((s, ff), BF16),
        scratch_shapes=[pltpu.VMEM((tm + HALO, tc), F32), pltpu.VMEM((tm + HALO, tc), F32)],
        compiler_params=_cparams("parallel", "arbitrary"),
    )(xb, w_up, w_up, conv_w, conv_w, cb, cb)


def _shift_mix(z, halo, mu, live):
    prev = pltpu.roll(z, 1, 0)
    row0 = lax.broadcasted_iota(jnp.int32, z.shape, 0) == 0
    prev = jnp.where(row0, halo[HALO - 1:HALO, :] * live, prev)
    return z + (prev - z) * mu


def _rwkv_prep_kernel(z_ref, zh_ref, lo_ref, loh_ref, mu_ref, mulo_ref, w0_ref, w2_ref, a0_ref, a2_ref,
                      g2_ref, kk_ref, ka_ref, rk_ref, bd_ref,
                      r_o, lw_o, k_o, v_o, kap_o, b_o, g_o, bonus_o):
    live = (pl.program_id(0) > 0).astype(F32)
    z = _shift_mix(z_ref[...], zh_ref[...], mu_ref[...], live)
    lo = _shift_mix(lo_ref[...], loh_ref[...], mulo_ref[...], live)
    r = z[:, 0:A_WIDTH]
    k = z[:, A_WIDTH:2 * A_WIDTH]
    v = z[:, 2 * A_WIDTH:3 * A_WIDTH]
    w = -_softplus(-(w0_ref[...] + _dot(jnp.tanh(lo).astype(BF16), w2_ref[...]))) - 0.5
    lw = -jnp.exp(w)
    a = jax.nn.sigmoid(a0_ref[...] + _dot(lo.astype(BF16), a2_ref[...]))
    g = _dot(jax.nn.sigmoid(lo).astype(BF16), g2_ref[...])
    kk = k * kk_ref[...]
    bd = bd_ref[...]
    ssq = _dot_exact_rhs(kk * kk, bd)
    kap = kk / jnp.maximum(jnp.sqrt(ssq), 1e-12)
    k2 = k * (1.0 + (a - 1.0) * ka_ref[...])
    bonus = _dot_exact_rhs(r * k2 * rk_ref[...], bd) * v
    b = kap * a
    g_o[...] = g
    bonus_o[...] = bonus
    for h in range(A_HEADS):
        sl = slice(h * A_HEAD, (h + 1) * A_HEAD)
        r_o[h] = r[:, sl]
        lw_o[h] = lw[:, sl]
        k_o[h] = k2[:, sl]
        v_o[h] = v[:, sl]
        kap_o[h] = kap[:, sl]
        b_o[h] = b[:, sl]


def rwkv_prep(z_rkv, z_lo, mu_rkv, mu_lo, w0, w2p, a0, a2p, g2p, k_k, k_a, r_k, bd, *, tm):
    s = z_rkv.shape[0]
    hb = tm // HALO
    row = lambda i: (i, 0)
    halo = lambda i: (jnp.maximum(i * hb - 1, 0), 0)
    const = lambda i: (0, 0)
    hm = jax.ShapeDtypeStruct((A_HEADS, s, A_HEAD), F32)
    hm_spec = pl.BlockSpec((A_HEADS, tm, A_HEAD), lambda i: (0, i, 0))
    full = jax.ShapeDtypeStruct((s, A_WIDTH), F32)
    vec = lambda a: a.reshape(1, -1)
    return pl.pallas_call(
        _rwkv_prep_kernel,
        grid=(s // tm,),
        in_specs=[pl.BlockSpec((tm, 3 * A_WIDTH), row), pl.BlockSpec((HALO, 3 * A_WIDTH), halo),
                  pl.BlockSpec((tm, A_LORA_PAD), row), pl.BlockSpec((HALO, A_LORA_PAD), halo),
                  pl.BlockSpec((1, 3 * A_WIDTH), const), pl.BlockSpec((1, A_LORA_PAD), const),
                  pl.BlockSpec((1, A_WIDTH), const), pl.BlockSpec((A_LORA_PAD, A_WIDTH), const),
                  pl.BlockSpec((1, A_WIDTH), const), pl.BlockSpec((A_LORA_PAD, A_WIDTH), const),
                  pl.BlockSpec((A_LORA_PAD, A_WIDTH), const),
                  pl.BlockSpec((1, A_WIDTH), const), pl.BlockSpec((1, A_WIDTH), const),
                  pl.BlockSpec((1, A_WIDTH), const), pl.BlockSpec((A_WIDTH, A_WIDTH), const)],
        out_specs=[hm_spec] * 6 + [pl.BlockSpec((tm, A_WIDTH), row)] * 2,
        out_shape=[hm] * 6 + [full] * 2,
        compiler_params=_cparams("parallel"),
    )(z_rkv, z_rkv, z_lo, z_lo, vec(mu_rkv), vec(mu_lo), vec(w0), w2p, vec(a0), a2p, g2p,
      vec(k_k), vec(k_a), vec(r_k), bd)


_BNN = (((2,), (1,)), ((0,), (0,)))
_BNT = (((2,), (2,)), ((0,), (0,)))
_BTN = (((1,), (1,)), ((0,), (0,)))


def _tri_inverse(a, row, col):
    mm = lambda p, q: _dot3(p, q, _BNN)
    eye = (row == col).astype(F32)
    ad = jnp.where((row >> 3) == (col >> 3), a, 0.0)
    t = eye - ad
    a2 = mm(ad, ad)
    t = t + mm(t, a2)
    a4 = mm(a2, a2)
    t = t + mm(t, a4)
    for sh in (3, 4, 5):
        inner = (row >> sh) == (col >> sh)
        outer = (row >> (sh + 1)) == (col >> (sh + 1))
        aoff = jnp.where(jnp.logical_and(outer, jnp.logical_not(inner)), a, 0.0)
        t = t - mm(mm(t, aoff), t)
    return t


def _rwkv_chunk_kernel(r_ref, lw_ref, k_ref, v_ref, kap_ref, b_ref, y_ref, state_ref):
    c = CHUNK

    @pl.when(pl.program_id(0) == 0)
    def _():
        state_ref[...] = jnp.zeros_like(state_ref)

    row = lax.broadcasted_iota(jnp.int32, (1, c, c), 1)
    col = lax.broadcasted_iota(jnp.int32, (1, c, c), 2)
    tril = row >= col
    stril = row > col
    eye = row == col
    lower_ones = jnp.broadcast_to(tril.astype(BF16), (A_HEADS, c, c))

    r, lw, k, v, kap, b = r_ref[...], lw_ref[...], k_ref[...], v_ref[...], kap_ref[...], b_ref[...]
    hi = lw.astype(BF16)
    r1 = lw - hi.astype(F32)
    mid = r1.astype(BF16)
    lo = (r1 - mid.astype(F32)).astype(BF16)
    ci = _dot(lower_ones, hi, _BNN) + (_dot(lower_ones, mid, _BNN) + _dot(lower_ones, lo, _BNN))
    ce = ci - lw
    cend = ci[:, c - 1:c, :]
    gn = jnp.exp(-ci)
    gend = jnp.exp(cend - ci)
    kap_h = kap * jnp.exp(ce)
    r_h = r * jnp.exp(ci)
    b_h = b * gn
    k_h = k * gn
    b_t = b * gend
    k_t = k * gend
    p = _dot3(jnp.concatenate([kap_h, r_h], axis=1), jnp.concatenate([b_h, k_h], axis=1), _BNT)
    a_ab = jnp.where(stril, p[:, :c, :c], 0.0)
    a_ak = jnp.where(stril, p[:, :c, c:], 0.0)
    r_b = jnp.where(tril, p[:, c:, :c], 0.0)
    r_k = jnp.where(tril, p[:, c:, c:], 0.0)
    t = _tri_inverse(a_ab, row, col)
    x = _dot3(t, jnp.concatenate([kap_h, _dot3(a_ak, v, _BNN)], axis=2), _BNN)
    m = state_ref[...]
    wm = _dot3(jnp.concatenate([x[:, :, :A_HEAD], r_h], axis=1), m, _BNN)
    u = -(wm[:, :c] + x[:, :, A_HEAD:])
    uv = jnp.concatenate([u, v], axis=1)
    y = wm[:, c:] + _dot3(jnp.concatenate([r_b, r_k], axis=2), uv, _BNN)
    decay_diag = jnp.where(eye, jnp.exp(cend), 0.0)
    state_ref[...] = _dot3(jnp.concatenate([b_t, k_t, decay_diag], axis=1),
                           jnp.concatenate([uv, m], axis=1), _BTN)
    ym = jnp.mean(y, axis=-1, keepdims=True)
    yc = y - ym
    yv = jnp.mean(yc * yc, axis=-1, keepdims=True)
    y_ref[...] = yc * lax.rsqrt(yv + A_GN_EPS)


def rwkv_chunks(r, lw, k, v, kap, b):
    _, s, _ = r.shape
    spec = pl.BlockSpec((A_HEADS, CHUNK, A_HEAD), lambda n: (0, n, 0))
    return pl.pallas_call(
        _rwkv_chunk_kernel,
        grid=(s // CHUNK,),
        in_specs=[spec] * 6,
        out_specs=spec,
        out_shape=jax.ShapeDtypeStruct((A_HEADS, s, A_HEAD), F32),
        scratch_shapes=[pltpu.VMEM((A_HEADS, A_HEAD, A_HEAD), F32)],
        compiler_params=_cparams("arbitrary"),
    )(r, lw, k, v, kap, b)


def _rwkv_post_kernel(y_ref, bonus_ref, g_ref, gg_ref, gb_ref, o_ref):
    y = jnp.concatenate([y_ref[h] for h in range(A_HEADS)], axis=1)
    o_ref[...] = ((y * gg_ref[...] + gb_ref[...] + bonus_ref[...]) * g_ref[...]).astype(o_ref.dtype)


def rwkv_post(y, bonus, g, gn_g, gn_b, *, tm):
    _, s, _ = y.shape
    row = lambda i: (i, 0)
    const = lambda i: (0, 0)
    return pl.pallas_call(
        _rwkv_post_kernel,
        grid=(s // tm,),
        in_specs=[pl.BlockSpec((A_HEADS, tm, A_HEAD), lambda i: (0, i, 0)),
                  pl.BlockSpec((tm, A_WIDTH), row), pl.BlockSpec((tm, A_WIDTH), row),
                  pl.BlockSpec((1, A_WIDTH), const), pl.BlockSpec((1, A_WIDTH), const)],
        out_specs=pl.BlockSpec((tm, A_WIDTH), row),
        out_shape=jax.ShapeDtypeStruct((s, A_WIDTH), BF16),
        compiler_params=_cparams("parallel"),
    )(y, bonus, g, gn_g.reshape(1, -1), gn_b.reshape(1, -1))


def _rglru_kernel(xb_ref, gate_ref, halo_ref, cw_ref, cb_ref, wr_ref, br_ref, wi_ref, bi_ref, lam_ref,
                  o_ref, xe_ref, a_ref, u_ref, h_ref, carry_ref, *, tm):
    i = pl.program_id(0)

    @pl.when(i == 0)
    def _():
        carry_ref[...] = jnp.zeros_like(carry_ref)

    xe_ref[0:HALO, :] = halo_ref[...] * (i > 0).astype(F32)
    xe_ref[HALO:, :] = xb_ref[...]
    xc = cb_ref[...] + cw_ref[B_CONV - 1:B_CONV, :] * xe_ref[HALO:, :]
    for d in range(1, B_CONV):
        xc = xc + cw_ref[B_CONV - 1 - d:B_CONV - d, :] * xe_ref[HALO - d:HALO - d + tm, :]
    xcb = xc.astype(BF16)
    r = jax.nn.sigmoid(_dot(xcb, wr_ref[...]) + br_ref[...])
    gi = jax.nn.sigmoid(_dot(xcb, wi_ref[...]) + bi_ref[...])
    log_a = -B_C * r * _softplus(-lam_ref[...])
    a = jnp.exp(log_a)
    a_ref[...] = a
    u_ref[...] = jnp.sqrt(-jnp.tanh(log_a) * (a * a + 1.0)) * (gi * xc)

    def group(gidx, h):
        base = pl.multiple_of(gidx * HALO, HALO)
        a8 = a_ref[pl.ds(base, HALO), :]
        u8 = u_ref[pl.ds(base, HALO), :]
        rows = []
        for rr in range(HALO):
            h = a8[rr:rr + 1, :] * h + u8[rr:rr + 1, :]
            rows.append(h)
        h_ref[pl.ds(base, HALO), :] = jnp.concatenate(rows, axis=0)
        return h

    carry_ref[...] = lax.fori_loop(0, tm // HALO, group, carry_ref[...])
    o_ref[...] = (jax.nn.gelu(gate_ref[...]) * h_ref[...]).astype(o_ref.dtype)


def rglru(z_b, conv_w, conv_b, wr, b_r, wi, b_i, lam, *, tm):
    s = z_b.shape[0]
    hb = tm // HALO
    row = lambda i: (i, 0)
    const = lambda i: (0, 0)
    vec = lambda a: a.reshape(1, -1)
    return pl.pallas_call(
        functools.partial(_rglru_kernel, tm=tm),
        grid=(s // tm,),
        in_specs=[pl.BlockSpec((tm, B_WIDTH), row), pl.BlockSpec((tm, B_WIDTH), lambda i: (i, 1)),
                  pl.BlockSpec((HALO, B_WIDTH), lambda i: (jnp.maximum(i * hb - 1, 0), 0)),
                  pl.BlockSpec((B_CONV, B_WIDTH), const), pl.BlockSpec((1, B_WIDTH), const),
                  pl.BlockSpec((B_WIDTH, B_WIDTH), const), pl.BlockSpec((1, B_WIDTH), const),
                  pl.BlockSpec((B_WIDTH, B_WIDTH), const), pl.BlockSpec((1, B_WIDTH), const),
                  pl.BlockSpec((1, B_WIDTH), const)],
        out_specs=pl.BlockSpec((tm, B_WIDTH), row),
        out_shape=jax.ShapeDtypeStruct((s, B_WIDTH), BF16),
        scratch_shapes=[pltpu.VMEM((tm + HALO, B_WIDTH), F32), pltpu.VMEM((tm, B_WIDTH), F32),
                        pltpu.VMEM((tm, B_WIDTH), F32), pltpu.VMEM((tm, B_WIDTH), F32),
                        pltpu.VMEM((1, B_WIDTH), F32)],
        compiler_params=_cparams("arbitrary"),
    )(z_b, z_b, z_b, conv_w, vec(conv_b), wr, vec(b_r), wi, vec(b_i), vec(lam))


ODD_PAD = 1024
ODD_KIDX_AT = 768
ODD_WIDX_AT = 896
QB = 256
SCORE_SCALE = (IDX_HEADS ** -0.5) * (IDX_DIM ** -0.5)
INT_MIN = -(2 ** 31)
CHUNK_SHIFT = 6
CUT_BITS = 14
SEL_ROWS = 128
HEAD_GROUP = 2


def _dsa_in_kernel(x_ref, w_ref, qn_ref, kvn_ref, kg_ref, kb_ref, cq_o, ckv_o, kidx_o, widx_o):
    acc = _dot(x_ref[...], w_ref[...])
    cq = acc[:, 0:C_Q_RANK]
    ckv = acc[:, C_Q_RANK:C_Q_RANK + C_KV_RANK]
    kidx = acc[:, ODD_KIDX_AT:ODD_KIDX_AT + IDX_DIM]
    widx = acc[:, ODD_WIDX_AT:ODD_WIDX_AT + IDX_HEADS]
    rms = lambda t, g: t * lax.rsqrt(jnp.mean(t * t, axis=-1, keepdims=True) + 1e-6) * g
    cq_o[...] = rms(cq, qn_ref[...]).astype(BF16)
    ckv_o[...] = rms(ckv, kvn_ref[...]).astype(BF16)
    kidx_o[...] = _layer_norm_rows(kidx, kg_ref[...], kb_ref[...], LN_EPS).astype(BF16)
    widx_o[...] = widx * SCORE_SCALE


def dsa_in(xb, w_pad, q_norm, kv_norm, kidx_g, kidx_b, *, tm):
    s, d = xb.shape
    row = lambda i: (i, 0)
    const = lambda i: (0, 0)
    vec = lambda a: a.reshape(1, -1)
    return pl.pallas_call(
        _dsa_in_kernel,
        grid=(s // tm,),
        in_specs=[pl.BlockSpec((tm, d), row), pl.BlockSpec((d, ODD_PAD), const),
                  pl.BlockSpec((1, C_Q_RANK), const), pl.BlockSpec((1, C_KV_RANK), const),
                  pl.BlockSpec((1, IDX_DIM), const), pl.BlockSpec((1, IDX_DIM), const)],
        out_specs=[pl.BlockSpec((tm, C_Q_RANK), row), pl.BlockSpec((tm, C_KV_RANK), row),
                   pl.BlockSpec((tm, IDX_DIM), row), pl.BlockSpec((tm, IDX_HEADS), row)],
        out_shape=[jax.ShapeDtypeStruct((s, C_Q_RANK), BF16), jax.ShapeDtypeStruct((s, C_KV_RANK), BF16),
                   jax.ShapeDtypeStruct((s, IDX_DIM), BF16), jax.ShapeDtypeStruct((s, IDX_HEADS), F32)],
        compiler_params=_cparams("parallel"),
    )(xb, w_pad, vec(q_norm), vec(kv_norm), vec(kidx_g), vec(kidx_b))


def _qabs_kernel(cq_ref, wuq_ref, wuk_ref, o_ref):
    q = _dot(cq_ref[...], wuq_ref[...]).astype(BF16)
    qa = _dot(q, wuk_ref[...], _NT)
    o_ref[...] = (qa * (C_HEAD_DIM ** -0.5 * LOG2E)).astype(BF16)


def dsa_qabs(cq, w_uq, w_uk, *, tm):
    s = cq.shape[0]
    return pl.pallas_call(
        _qabs_kernel,
        grid=(s // tm, C_HEADS),
        in_specs=[pl.BlockSpec((tm, C_Q_RANK), lambda i, h: (i, 0)),
                  pl.BlockSpec((C_Q_RANK, C_HEAD_DIM), lambda i, h: (0, h)),
                  pl.BlockSpec((None, C_KV_RANK, C_HEAD_DIM), lambda i, h: (h, 0, 0))],
        out_specs=pl.BlockSpec((None, tm, C_KV_RANK), lambda i, h: (h, i, 0)),
        out_shape=jax.ShapeDtypeStruct((C_HEADS, s, C_KV_RANK), BF16),
        compiler_params=_cparams("parallel", "parallel"),
    )(cq, w_uq, w_uk)


def _sortable_key(score):
    bits = lax.bitcast_convert_type(score + 0.0, jnp.int32)
    return jnp.where(bits < 0, bits ^ jnp.int32(0x7FFFFFFF), bits)


def _dsa_attn_kernel(qabs_ref, qidx_ref, widx_ref, kidx_t_ref, ckv_ref, wuv_ref, bias_ref,
                     o_ref, keys_ref, acc_ref, m_ref, l_ref, *, k_sel):
    i = pl.program_id(0)
    n_tiles = i + 1
    q_local = lax.broadcasted_iota(jnp.int32, (QB, QB), 0)
    s_local = lax.broadcasted_iota(jnp.int32, (QB, QB), 1)
    allowed_diag = (s_local >> CHUNK_SHIFT) <= (q_local >> CHUNK_SHIFT)

    widx = widx_ref[...]
    w_cols = [jnp.broadcast_to(widx[:, j:j + 1], (QB, QB)) for j in range(IDX_HEADS)]

    def score_tile(t, carry):
        off = pl.multiple_of(t * QB, QB)
        kt = kidx_t_ref[:, pl.ds(off, QB)]
        sc = jnp.zeros((QB, QB), F32)
        for j in range(IDX_HEADS):
            d = _dot(qidx_ref[:, j * IDX_DIM:(j + 1) * IDX_DIM], kt)
            sc = sc + w_cols[j] * jnp.maximum(d, 0.0)
        key = _sortable_key(sc)
        key = jnp.where(jnp.logical_or(t < i, allowed_diag), key, jnp.int32(INT_MIN))
        keys_ref[:, pl.ds(off, QB)] = key
        return carry

    lax.fori_loop(0, n_tiles, score_tile, 0)

    half = QB // 2
    pos_local = lax.broadcasted_iota(jnp.int32, (SEL_ROWS, half), 1)

    def select_rows(r0):
        def count(pred):
            def body(t, acc):
                off = pl.multiple_of(t * QB, QB)
                lo = pred(keys_ref[r0:r0 + SEL_ROWS, pl.ds(off, half)], t * QB)
                hi = pred(keys_ref[r0:r0 + SEL_ROWS, pl.ds(off + half, half)], t * QB + half)
                return acc + (lo.astype(jnp.int32) + hi.astype(jnp.int32))
            acc = lax.fori_loop(0, n_tiles, body, jnp.zeros((SEL_ROWS, half), jnp.int32))
            return jnp.sum(acc.astype(F32), axis=-1, keepdims=True).astype(jnp.int32)

        def thr_bit(it, thr):
            cand = thr + jnp.left_shift(jnp.int32(1), 31 - it)
            cand_b = jnp.broadcast_to(cand, (SEL_ROWS, half))
            cnt = count(lambda kt, base: kt >= cand_b)
            return jnp.where(cnt >= k_sel, cand, thr)

        thr = lax.fori_loop(0, 32, thr_bit, jnp.full((SEL_ROWS, 1), INT_MIN, jnp.int32))
        thr_b = jnp.broadcast_to(thr, (SEL_ROWS, half))

        def cut_bit(it, cut):
            cand = cut + jnp.left_shift(jnp.int32(1), CUT_BITS - 1 - it)
            cand_b = jnp.broadcast_to(cand, (SEL_ROWS, half))
            cnt = count(lambda kt, base: jnp.logical_or(
                kt > thr_b, jnp.logical_and(kt == thr_b, pos_local + base < cand_b)))
            return jnp.where(cnt <= k_sel, cand, cut)

        n_ge = count(lambda kt, base: kt >= thr_b)
        tied = jnp.logical_and(n_ge > k_sel, thr > INT_MIN)
        any_tied = jnp.max(jnp.where(tied, 1.0, 0.0)) > 0.0
        no_cut = jnp.full((SEL_ROWS, 1), 2 ** CUT_BITS - 1, jnp.int32)
        cut = lax.cond(any_tied,
                       lambda: lax.fori_loop(0, CUT_BITS, cut_bit, jnp.zeros((SEL_ROWS, 1), jnp.int32)),
                       lambda: no_cut)
        return thr, cut

    parts = [select_rows(r0) for r0 in range(0, QB, SEL_ROWS)]
    thr_c = jnp.concatenate([p[0] for p in parts], axis=0)
    cut_c = jnp.concatenate([p[1] for p in parts], axis=0)
    rep_lanes = lambda a, n: jnp.concatenate([a] * (n // LANES), axis=-1)

    m_ref[...] = jnp.full(m_ref.shape, NEG_BIG, F32)
    l_ref[...] = jnp.zeros(l_ref.shape, F32)
    acc_ref[...] = jnp.zeros(acc_ref.shape, F32)

    def attend(off, width, near):
        kt = keys_ref[:, pl.ds(off, width)]
        thr_w = jnp.broadcast_to(thr_c, (QB, width))
        cut_w = jnp.broadcast_to(cut_c, (QB, width))
        pos = lax.broadcasted_iota(jnp.int32, (QB, width), 1) + off
        sel = jnp.logical_or(kt > thr_w, jnp.logical_and(kt == thr_w, pos < cut_w))
        if near == 1:
            sel = jnp.logical_and(sel, allowed_diag)
        mask_add = jnp.where(sel, 0.0, NEG_BIG)[None]
        kv = ckv_ref[pl.ds(off, width), :]
        rep = lambda a, n: jnp.concatenate([a] * (n // LANES), axis=-1)

        for g in range(C_HEADS // HEAD_GROUP):
            hs = slice(g * HEAD_GROUP, (g + 1) * HEAD_GROUP)
            q = qabs_ref[hs].reshape(HEAD_GROUP * QB, C_KV_RANK)
            s = _dot(q, kv, _NT).reshape(HEAD_GROUP, QB, width) + mask_add
            if near is not None:
                s = s + bias_ref[near, hs]
            m_old = m_ref[hs]
            row_max = jnp.broadcast_to(jnp.max(s, axis=-1, keepdims=True), m_old.shape)
            m_new = jnp.maximum(m_old, row_max)
            alpha = jnp.exp2(m_old - m_new)
            p = jnp.exp2(s - rep(m_new, width))
            row_sum = jnp.broadcast_to(jnp.sum(p, axis=-1, keepdims=True), m_old.shape)
            l_ref[hs] = alpha * l_ref[hs] + row_sum
            pv = _dot(p.astype(BF16).reshape(HEAD_GROUP * QB, width), kv).reshape(HEAD_GROUP, QB, C_KV_RANK)
            acc_ref[hs] = rep(alpha, C_KV_RANK) * acc_ref[hs] + pv
            m_ref[hs] = m_new

    n_far = jnp.maximum(i - 1, 0)

    def far_pair(t2, carry):
        attend(pl.multiple_of(t2 * (2 * QB), 2 * QB), 2 * QB, None)
        return carry

    lax.fori_loop(0, n_far // 2, far_pair, 0)

    @pl.when(n_far % 2 == 1)
    def _():
        attend(pl.multiple_of((n_far - 1) * QB, QB), QB, None)

    @pl.when(i >= 1)
    def _():
        attend(pl.multiple_of((i - 1) * QB, QB), QB, 0)

    attend(pl.multiple_of(i * QB, QB), QB, 1)

    for h in range(C_HEADS):
        o_lat = (acc_ref[h] / rep_lanes(l_ref[h], C_KV_RANK)).astype(BF16)
        o_ref[:, h * C_HEAD_DIM:(h + 1) * C_HEAD_DIM] = _dot(o_lat, wuv_ref[h]).astype(o_ref.dtype)


def dsa_attention(qabs, qidx, widx, kidx_t, ckv, w_uv, bias_near, *, k_sel):
    _, s, _ = qabs.shape
    full2 = lambda i: (0, 0)
    once = pl.Buffered(1)
    return pl.pallas_call(
        functools.partial(_dsa_attn_kernel, k_sel=k_sel),
        grid=(s // QB,),
        in_specs=[pl.BlockSpec((C_HEADS, QB, C_KV_RANK), lambda i: (0, i, 0)),
                  pl.BlockSpec((QB, IDX_HEADS * IDX_DIM), lambda i: (i, 0)),
                  pl.BlockSpec((QB, IDX_HEADS), lambda i: (i, 0)),
                  pl.BlockSpec((IDX_DIM, s), full2, pipeline_mode=once),
                  pl.BlockSpec((s, C_KV_RANK), full2, pipeline_mode=once),
                  pl.BlockSpec((C_HEADS, C_KV_RANK, C_HEAD_DIM), lambda i: (0, 0, 0), pipeline_mode=once),
                  pl.BlockSpec((2, C_HEADS, QB, QB), lambda i: (0, 0, 0, 0), pipeline_mode=once)],
        out_specs=pl.BlockSpec((QB, C_HEADS * C_HEAD_DIM), lambda i: (i, 0)),
        out_shape=jax.ShapeDtypeStruct((s, C_HEADS * C_HEAD_DIM), BF16),
        scratch_shapes=[pltpu.VMEM((QB, s), jnp.int32),
                        pltpu.VMEM((C_HEADS, QB, C_KV_RANK), F32),
                        pltpu.VMEM((C_HEADS, QB, LANES), F32),
                        pltpu.VMEM((C_HEADS, QB, LANES), F32)],
        compiler_params=_cparams("parallel"),
    )(qabs, qidx, widx, kidx_t, ckv, w_uv, bias_near)


def _t5_bucket(rel):
    nb = REL_BUCKETS // 2
    max_exact = nb // 2
    ret = jnp.where(rel > 0, nb, 0)
    n = jnp.abs(rel)
    nf = jnp.maximum(n, 1).astype(jnp.float32)
    large = max_exact + (jnp.log(nf / max_exact) / math.log(REL_MAX_DIST / max_exact) * (nb - max_exact)).astype(jnp.int32)
    large = jnp.minimum(large, nb - 1)
    return ret + jnp.where(n < max_exact, n, large)


def _near_bias(rel_bias):
    ql = jnp.arange(QB)[:, None]
    sl = jnp.arange(QB)[None, :]
    rel = jnp.stack([sl - ql - QB, sl - ql])
    far = rel_bias[_t5_bucket(jnp.array(-2 * QB))]
    table = ((rel_bias - far) * LOG2E).astype(F32)
    onehot = jax.nn.one_hot(_t5_bucket(rel), REL_BUCKETS, dtype=F32)
    return jnp.einsum('tqsb,bh->thqs', onehot, table, precision=lax.Precision.HIGHEST)


def _pad_rows(w, at, total):
    return jnp.zeros((total, w.shape[1]), w.dtype).at[at:at + w.shape[0]].set(w)


def _block_diag(w):
    n, d, e = w.shape
    eye = jnp.eye(n, dtype=w.dtype)
    return (eye[:, None, :, None] * w[:, :, None, :]).reshape(n * d, n * e)


def _even_mixer(x, xb, w_in, w_out, mu, w0, w2, a0, a2, g2, k_k, k_a, r_k, gn_g, gn_b,
                conv_w, conv_b, w_r, b_r, w_i, b_i, lam, ln_g, ln_b, tm):
    w_in = w_in.astype(BF16)
    n_rkv = 3 * A_WIDTH
    z_rkv = matmul(xb, w_in[:, :n_rkv], tm=tm, tn=1024)
    w_lo = jnp.pad(w_in[:, n_rkv:A_COLS], ((0, 0), (0, A_LORA_PAD - A_LORA)))
    z_lo = matmul(xb, w_lo, tm=tm, tn=A_LORA_PAD)
    z_b = matmul(xb, w_in[:, A_COLS:], tm=tm, tn=1024)

    mu_lo = jnp.pad(mu[n_rkv:], (0, A_LORA_PAD - A_LORA))
    w2p = _pad_rows(w2, 0, A_LORA_PAD).astype(BF16)
    a2p = _pad_rows(a2, A_DECAY_LORA, A_LORA_PAD).astype(BF16)
    g2p = _pad_rows(g2, A_DECAY_LORA + A_ICL_LORA, A_LORA_PAD).astype(BF16)
    bd = _block_diag(jnp.ones((A_HEADS, A_HEAD, A_HEAD), BF16))
    r, lw, k2, v, kap, b, g, bonus = rwkv_prep(
        z_rkv, z_lo, mu[:n_rkv], mu_lo, w0, w2p, a0, a2p, g2p, k_k, k_a, r_k.reshape(-1), bd,
        tm=min(tm, 256))
    y = rwkv_chunks(r, lw, k2, v, kap, b)
    y_a = rwkv_post(y, bonus, g, gn_g, gn_b, tm=tm)

    y_b = rglru(z_b, conv_w, conv_b, _block_diag(w_r).astype(BF16), b_r,
                _block_diag(w_i).astype(BF16), b_i, lam, tm=min(tm, 256))
    y = jnp.concatenate([y_a, y_b], axis=1)
    return matmul_residual_ln(y, w_out.astype(BF16), x, ln_g, ln_b, tm=tm, tk=y.shape[1])


def _odd_mixer(x, xb, w_in, w_out, q_norm, kv_norm, w_uq, w_uk, w_uv, w_qidx, kidx_g, kidx_b,
               bias_near, ln_g, ln_b, tm):
    s = x.shape[0]
    d = w_in.shape[0]
    n_qkv = C_Q_RANK + C_KV_RANK
    w_pad = jnp.zeros((d, ODD_PAD), BF16)
    w_pad = w_pad.at[:, :n_qkv].set(w_in[:, :n_qkv].astype(BF16))
    w_pad = w_pad.at[:, ODD_KIDX_AT:ODD_KIDX_AT + IDX_DIM].set(w_in[:, n_qkv:n_qkv + IDX_DIM].astype(BF16))
    w_pad = w_pad.at[:, ODD_WIDX_AT:ODD_WIDX_AT + IDX_HEADS].set(w_in[:, n_qkv + IDX_DIM:].astype(BF16))
    cq, ckv, kidx, widx = dsa_in(xb, w_pad, q_norm, kv_norm, kidx_g, kidx_b, tm=tm)
    qabs = dsa_qabs(cq, w_uq.astype(BF16), w_uk.astype(BF16), tm=tm)
    qidx = matmul(cq, w_qidx.astype(BF16), tm=tm, tn=IDX_HEADS * IDX_DIM, out_dtype=BF16)
    o = dsa_attention(qabs, qidx, widx, kidx.T, ckv, w_uv.astype(BF16), bias_near,
                      k_sel=min(TOPK_MAX, s // 4))
    return matmul_residual_ln(o, w_out.astype(BF16), x, ln_g, ln_b, tm=tm, tk=o.shape[1])


def kernel(x, p, rel_bias, ln1_g, ln1_b, ln2_g, ln2_b, ffn_w_up, ffn_conv_w, ffn_conv_b, ffn_w_down, ple_w_proj, ple_w_gate, ev_w_in, ev_w_out, a_mu, a_w0, a_w2, a_a0, a_a2, a_g2, a_k_k, a_k_a, a_r_k, a_gn_g, a_gn_b, b_conv_w, b_conv_b, b_w_r, b_b_r, b_w_i, b_b_i, b_lambda, od_w_in, od_w_out, c_q_norm, c_kv_norm, c_w_uq, c_w_uk, c_w_uv, c_w_qidx, c_kidx_g, c_kidx_b):
    bsz, s, d = x.shape
    assert bsz == 1 and s % QB == 0 and s <= 2 ** CUT_BITS
    tm = min(512, s)
    x = x[0]
    xb = x.astype(BF16)
    bias_near = _near_bias(rel_bias)
    for layer in range(DEPTH):
        j = layer // 2
        if layer % 2 == 0:
            x, xb = _even_mixer(x, xb, ev_w_in[j], ev_w_out[j], a_mu[j], a_w0[j], a_w2[j], a_a0[j], a_a2[j],
                                a_g2[j], a_k_k[j], a_k_a[j], a_r_k[j], a_gn_g[j], a_gn_b[j],
                                b_conv_w[j], b_conv_b[j], b_w_r[j], b_b_r[j], b_w_i[j], b_b_i[j], b_lambda[j],
                                ln1_g[layer], ln1_b[layer], tm)
        else:
            x, xb = _odd_mixer(x, xb, od_w_in[j], od_w_out[j], c_q_norm[j], c_kv_norm[j], c_w_uq[j], c_w_uk[j],
                               c_w_uv[j], c_w_qidx[j], c_kidx_g[j], c_kidx_b[j], bias_near,
                               ln1_g[layer], ln1_b[layer], tm)
        hm = ffn_up(xb, ffn_w_up[layer].astype(BF16), ffn_conv_w[layer], ffn_conv_b[layer], tm=tm, tc=512)
        x, xb = matmul_residual_ln(hm, ffn_w_down[layer].astype(BF16), x, ln2_g[layer], ln2_b[layer],
                                   tm=tm, tk=512)
        x, xb = ple_update(xb, ple_w_gate[layer].astype(BF16), p[layer, 0].astype(BF16),
                           ple_w_proj[layer].astype(BF16), x, tm=tm, tn=1024)
    return x[None]
```

```python
import functools
import math

import jax
import jax.numpy as jnp
import numpy as np
from jax import lax
from jax.experimental import pallas as pl
from jax.experimental.pallas import tpu as pltpu

F32 = jnp.float32
BF16 = jnp.bfloat16

D_MODEL = 2048
DEPTH = 4
CHUNK = 64
DN_ALPHA = (2 * DEPTH) ** 0.25
LN_EPS = 1e-5
A_WIDTH = 1024
A_HEAD = 64
A_HEADS = 16
A_DECAY_LORA = 64
A_ICL_LORA = 64
A_GATE_LORA = 160
A_LORA = A_DECAY_LORA + A_ICL_LORA + A_GATE_LORA
A_LORA_PAD = 384
A_COLS = 3 * A_WIDTH + A_LORA
A_GN_EPS = 64e-5
B_WIDTH = 1024
B_BLOCKS = 16
B_BLOCK = 64
B_CONV = 4
B_C = 8.0
C_HEADS = 16
C_HEAD_DIM = 128
C_Q_RANK = 512
C_KV_RANK = 256
IDX_HEADS = 16
IDX_DIM = 64
TOPK_MAX = 256
REL_BUCKETS = 32
REL_MAX_DIST = 128
D_FF = 5632
FFN_CONV = 3
PLE_DIM = 256

VMEM_LIMIT_BYTES = 56 * 1024 * 1024
NEG_BIG = -1e30
HALO = 8
LANES = 128
LOG2E = math.log2(math.e)


def _cparams(*sem):
    return pltpu.CompilerParams(dimension_semantics=sem, vmem_limit_bytes=VMEM_LIMIT_BYTES)


def _split_bf16(a):
    hi = a.astype(BF16)
    lo = (a - hi.astype(F32)).astype(BF16)
    return hi, lo


def _dot(a, b, dims=(((1,), (0,)), ((), ()))):
    return lax.dot_general(a, b, dims, preferred_element_type=F32)


_NT = (((1,), (1,)), ((), ()))
_TN = (((0,), (0,)), ((), ()))
_NN = (((1,), (0,)), ((), ()))


def _dot3(a, b, dims=_NN):
    ah, al = _split_bf16(a)
    bh, bl = _split_bf16(b)
    return _dot(ah, bh, dims) + (_dot(ah, bl, dims) + _dot(al, bh, dims))


def _dot_exact_rhs(a, b_bf16, dims=_NN):
    hi = a.astype(BF16)
    r1 = a - hi.astype(F32)
    mid = r1.astype(BF16)
    lo = (r1 - mid.astype(F32)).astype(BF16)
    return _dot(hi, b_bf16, dims) + (_dot(mid, b_bf16, dims) + _dot(lo, b_bf16, dims))


def _layer_norm_rows(v, g, b, eps):
    mu = jnp.mean(v, axis=-1, keepdims=True)
    d = v - mu
    var = jnp.mean(d * d, axis=-1, keepdims=True)
    return d * lax.rsqrt(var + eps) * g + b


def _softplus(x):
    return jnp.maximum(x, 0.0) + jnp.log1p(jnp.exp(-jnp.abs(x)))


def _mm_kernel(a_ref, b_ref, o_ref):
    o_ref[...] = _dot(a_ref[...], b_ref[...]).astype(o_ref.dtype)


def matmul(a, b, *, tm, tn, out_dtype=F32):
    m, k = a.shape
    _, n = b.shape
    assert m % tm == 0 and n % tn == 0
    return pl.pallas_call(
        _mm_kernel,
        grid=(n // tn, m // tm),
        in_specs=[pl.BlockSpec((tm, k), lambda j, i: (i, 0)),
                  pl.BlockSpec((k, tn), lambda j, i: (0, j))],
        out_specs=pl.BlockSpec((tm, tn), lambda j, i: (i, j)),
        out_shape=jax.ShapeDtypeStruct((m, n), out_dtype),
        compiler_params=_cparams("parallel", "parallel"),
    )(a, b)


def _mm_ln_kernel(a_ref, b_ref, x_ref, g_ref, beta_ref, o_ref, ob_ref, acc_ref, *, nk):
    kk = pl.program_id(1)

    @pl.when(kk == 0)
    def _():
        acc_ref[...] = jnp.zeros_like(acc_ref)

    acc_ref[...] += _dot(a_ref[...], b_ref[...])

    @pl.when(kk == nk - 1)
    def _():
        v = DN_ALPHA * x_ref[...] + acc_ref[...]
        y = _layer_norm_rows(v, g_ref[...], beta_ref[...], LN_EPS)
        o_ref[...] = y
        ob_ref[...] = y.astype(BF16)


def matmul_residual_ln(a, b, x, g, beta, *, tm, tk):
    m, k = a.shape
    _, n = b.shape
    assert m % tm == 0 and k % tk == 0
    nk = k // tk
    return pl.pallas_call(
        functools.partial(_mm_ln_kernel, nk=nk),
        grid=(m // tm, nk),
        in_specs=[pl.BlockSpec((tm, tk), lambda i, kk: (i, kk)),
                  pl.BlockSpec((tk, n), lambda i, kk: (kk, 0)),
                  pl.BlockSpec((tm, n), lambda i, kk: (i, 0)),
                  pl.BlockSpec((1, n), lambda i, kk: (0, 0)),
                  pl.BlockSpec((1, n), lambda i, kk: (0, 0))],
        out_specs=[pl.BlockSpec((tm, n), lambda i, kk: (i, 0)),
                   pl.BlockSpec((tm, n), lambda i, kk: (i, 0))],
        out_shape=[jax.ShapeDtypeStruct((m, n), F32), jax.ShapeDtypeStruct((m, n), BF16)],
        scratch_shapes=[pltpu.VMEM((tm, n), F32)],
        compiler_params=_cparams("parallel", "arbitrary"),
    )(a, b, x, g.reshape(1, n), beta.reshape(1, n))


def _ple_kernel(xb_ref, wg_ref, p_ref, wp_ref, x_ref, o_ref, ob_ref):
    gate = jax.nn.sigmoid(_dot(xb_ref[...], wg_ref[...]))
    proj = _dot(p_ref[...], wp_ref[...])
    y = x_ref[...] + gate * proj
    o_ref[...] = y
    ob_ref[...] = y.astype(BF16)


def ple_update(xb, wg, pb, wp, x, *, tm, tn):
    m, d = xb.shape
    pd = pb.shape[1]
    return pl.pallas_call(
        _ple_kernel,
        grid=(d // tn, m // tm),
        in_specs=[pl.BlockSpec((tm, d), lambda j, i: (i, 0)),
                  pl.BlockSpec((d, tn), lambda j, i: (0, j)),
                  pl.BlockSpec((tm, pd), lambda j, i: (i, 0)),
                  pl.BlockSpec((pd, tn), lambda j, i: (0, j)),
                  pl.BlockSpec((tm, tn), lambda j, i: (i, j))],
        out_specs=[pl.BlockSpec((tm, tn), lambda j, i: (i, j)),
                   pl.BlockSpec((tm, tn), lambda j, i: (i, j))],
        out_shape=[jax.ShapeDtypeStruct((m, d), F32), jax.ShapeDtypeStruct((m, d), BF16)],
        compiler_params=_cparams("parallel", "parallel"),
    )(xb, wg, pb, wp, x)


def _ffn_up_kernel(x_ref, wg_ref, wu_ref, cwg_ref, cwu_ref, cbg_ref, cbu_ref, o_ref, eg_ref, eu_ref, *, tm):
    i = pl.program_id(1)

    @pl.when(i == 0)
    def _():
        eg_ref[0:HALO, :] = jnp.zeros((HALO, eg_ref.shape[1]), F32)
        eu_ref[0:HALO, :] = jnp.zeros((HALO, eu_ref.shape[1]), F32)

    @pl.when(i > 0)
    def _():
        eg_ref[0:HALO, :] = eg_ref[tm:tm + HALO, :]
        eu_ref[0:HALO, :] = eu_ref[tm:tm + HALO, :]

    x = x_ref[...]
    eg_ref[HALO:, :] = _dot(x, wg_ref[...])
    eu_ref[HALO:, :] = _dot(x, wu_ref[...])

    def conv(e_ref, w_ref, b_ref):
        acc = b_ref[...] + w_ref[FFN_CONV - 1:FFN_CONV, :] * e_ref[HALO:, :]
        for d in range(1, FFN_CONV):
            acc = acc + w_ref[FFN_CONV - 1 - d:FFN_CONV - d, :] * e_ref[HALO - d:HALO - d + tm, :]
        return acc

    gate = conv(eg_ref, cwg_ref, cbg_ref)
    up = conv(eu_ref, cwu_ref, cbu_ref)
    o_ref[...] = (jax.nn.gelu(gate) * up).astype(o_ref.dtype)


def ffn_up(xb, w_up, conv_w, conv_b, *, tm, tc):
    s, d = xb.shape
    two_ff = w_up.shape[1]
    ff = two_ff // 2
    nc = ff // tc
    cb = conv_b.reshape(1, two_ff)
    return pl.pallas_call(
        functools.partial(_ffn_up_kernel, tm=tm),
        grid=(nc, s // tm),
        in_specs=[pl.BlockSpec((tm, d), lambda j, i: (i, 0)),
                  pl.BlockSpec((d, tc), lambda j, i: (0, j)),
                  pl.BlockSpec((d, tc), lambda j, i: (0, j + nc)),
                  pl.BlockSpec((FFN_CONV, tc), lambda j, i: (0, j)),
                  pl.BlockSpec((FFN_CONV, tc), lambda j, i: (0, j + nc)),
                  pl.BlockSpec((1, tc), lambda j, i: (0, j)),
                  pl.BlockSpec((1, tc), lambda j, i: (0, j + nc))],
        out_specs=pl.BlockSpec((tm, tc), lambda j, i: (i, j)),
        out_shape=jax.ShapeDtypeStruct((s, ff), BF16),
        scratch_shapes=[pltpu.VMEM((tm + HALO, tc), F32), pltpu.VMEM((tm + HALO, tc), F32)],
        compiler_params=_cparams("parallel", "arbitrary"),
    )(xb, w_up, w_up, conv_w, conv_w, cb, cb)


def _shift_mix(z, halo, mu, live):
    prev = pltpu.roll(z, 1, 0)
    row0 = lax.broadcasted_iota(jnp.int32, z.shape, 0) == 0
    prev = jnp.where(row0, halo[HALO - 1:HALO, :] * live, prev)
    return z + (prev - z) * mu


def _rwkv_prep_kernel(z_ref, zh_ref, lo_ref, loh_ref, mu_ref, mulo_ref, w0_ref, w2_ref, a0_ref, a2_ref,
                      g2_ref, kk_ref, ka_ref, rk_ref, bd_ref,
                      r_o, lw_o, k_o, v_o, kap_o, b_o, g_o, bonus_o):
    live = (pl.program_id(0) > 0).astype(F32)
    z = _shift_mix(z_ref[...], zh_ref[...], mu_ref[...], live)
    lo = _shift_mix(lo_ref[...], loh_ref[...], mulo_ref[...], live)
    r = z[:, 0:A_WIDTH]
    k = z[:, A_WIDTH:2 * A_WIDTH]
    v = z[:, 2 * A_WIDTH:3 * A_WIDTH]
    w = -_softplus(-(w0_ref[...] + _dot(jnp.tanh(lo).astype(BF16), w2_ref[...]))) - 0.5
    lw = -jnp.exp(w)
    a = jax.nn.sigmoid(a0_ref[...] + _dot(lo.astype(BF16), a2_ref[...]))
    g = _dot(jax.nn.sigmoid(lo).astype(BF16), g2_ref[...])
    kk = k * kk_ref[...]
    bd = bd_ref[...]
    ssq = _dot_exact_rhs(kk * kk, bd)
    kap = kk / jnp.maximum(jnp.sqrt(ssq), 1e-12)
    k2 = k * (1.0 + (a - 1.0) * ka_ref[...])
    bonus = _dot_exact_rhs(r * k2 * rk_ref[...], bd) * v
    b = kap * a
    g_o[...] = g
    bonus_o[...] = bonus
    for h in range(A_HEADS):
        sl = slice(h * A_HEAD, (h + 1) * A_HEAD)
        r_o[h] = r[:, sl]
        lw_o[h] = lw[:, sl]
        k_o[h] = k2[:, sl]
        v_o[h] = v[:, sl]
        kap_o[h] = kap[:, sl]
        b_o[h] = b[:, sl]


def rwkv_prep(z_rkv, z_lo, mu_rkv, mu_lo, w0, w2p, a0, a2p, g2p, k_k, k_a, r_k, bd, *, tm):
    s = z_rkv.shape[0]
    hb = tm // HALO
    row = lambda i: (i, 0)
    halo = lambda i: (jnp.maximum(i * hb - 1, 0), 0)
    const = lambda i: (0, 0)
    hm = jax.ShapeDtypeStruct((A_HEADS, s, A_HEAD), F32)
    hm_spec = pl.BlockSpec((A_HEADS, tm, A_HEAD), lambda i: (0, i, 0))
    full = jax.ShapeDtypeStruct((s, A_WIDTH), F32)
    vec = lambda a: a.reshape(1, -1)
    return pl.pallas_call(
        _rwkv_prep_kernel,
        grid=(s // tm,),
        in_specs=[pl.BlockSpec((tm, 3 * A_WIDTH), row), pl.BlockSpec((HALO, 3 * A_WIDTH), halo),
                  pl.BlockSpec((tm, A_LORA_PAD), row), pl.BlockSpec((HALO, A_LORA_PAD), halo),
                  pl.BlockSpec((1, 3 * A_WIDTH), const), pl.BlockSpec((1, A_LORA_PAD), const),
                  pl.BlockSpec((1, A_WIDTH), const), pl.BlockSpec((A_LORA_PAD, A_WIDTH), const),
                  pl.BlockSpec((1, A_WIDTH), const), pl.BlockSpec((A_LORA_PAD, A_WIDTH), const),
                  pl.BlockSpec((A_LORA_PAD, A_WIDTH), const),
                  pl.BlockSpec((1, A_WIDTH), const), pl.BlockSpec((1, A_WIDTH), const),
                  pl.BlockSpec((1, A_WIDTH), const), pl.BlockSpec((A_WIDTH, A_WIDTH), const)],
        out_specs=[hm_spec] * 6 + [pl.BlockSpec((tm, A_WIDTH), row)] * 2,
        out_shape=[hm] * 6 + [full] * 2,
        compiler_params=_cparams("parallel"),
    )(z_rkv, z_rkv, z_lo, z_lo, vec(mu_rkv), vec(mu_lo), vec(w0), w2p, vec(a0), a2p, g2p,
      vec(k_k), vec(k_a), vec(r_k), bd)


_BNN = (((2,), (1,)), ((0,), (0,)))
_BNT = (((2,), (2,)), ((0,), (0,)))
_BTN = (((1,), (1,)), ((0,), (0,)))


def _tri_inverse(a, row, col):
    mm1 = lambda p, q: _dot(p.astype(BF16), q.astype(BF16), _BNN)
    mm3 = lambda p, q: _dot3(p, q, _BNN)
    eye = (row == col).astype(F32)
    ad = jnp.where((row >> 3) == (col >> 3), a, 0.0)
    t = eye - ad
    a2 = mm1(ad, ad)
    t = t + mm1(t, a2)
    a4 = mm1(a2, a2)
    t = t + mm1(t, a4)
    for sh, mm in ((3, mm1), (4, mm3), (5, mm3)):
        inner = (row >> sh) == (col >> sh)
        outer = (row >> (sh + 1)) == (col >> (sh + 1))
        aoff = jnp.where(jnp.logical_and(outer, jnp.logical_not(inner)), a, 0.0)
        t = t - mm(mm(t, aoff), t)
    return t


def _rwkv_chunk_kernel(r_ref, lw_ref, k_ref, v_ref, kap_ref, b_ref, y_ref, state_ref):
    c = CHUNK

    @pl.when(pl.program_id(0) == 0)
    def _():
        state_ref[...] = jnp.zeros_like(state_ref)

    row = lax.broadcasted_iota(jnp.int32, (1, c, c), 1)
    col = lax.broadcasted_iota(jnp.int32, (1, c, c), 2)
    tril = row >= col
    stril = row > col
    eye = row == col
    lower_ones = jnp.broadcast_to(tril.astype(BF16), (A_HEADS, c, c))

    r, lw, k, v, kap, b = r_ref[...], lw_ref[...], k_ref[...], v_ref[...], kap_ref[...], b_ref[...]
    hi = lw.astype(BF16)
    r1 = lw - hi.astype(F32)
    mid = r1.astype(BF16)
    lo = (r1 - mid.astype(F32)).astype(BF16)
    ci = _dot(lower_ones, hi, _BNN) + (_dot(lower_ones, mid, _BNN) + _dot(lower_ones, lo, _BNN))
    ce = ci - lw
    cend = ci[:, c - 1:c, :]
    gn = jnp.exp(-ci)
    gend = jnp.exp(cend - ci)
    kap_h = kap * jnp.exp(ce)
    r_h = r * jnp.exp(ci)
    b_h = b * gn
    k_h = k * gn
    b_t = b * gend
    k_t = k * gend
    p = _dot3(jnp.concatenate([kap_h, r_h], axis=1), jnp.concatenate([b_h, k_h], axis=1), _BNT)
    a_ab = jnp.where(stril, p[:, :c, :c], 0.0)
    a_ak = jnp.where(stril, p[:, :c, c:], 0.0)
    r_b = jnp.where(tril, p[:, c:, :c], 0.0)
    r_k = jnp.where(tril, p[:, c:, c:], 0.0)
    t = _tri_inverse(a_ab, row, col)
    x = _dot3(t, jnp.concatenate([kap_h, _dot3(a_ak, v, _BNN)], axis=2), _BNN)
    m = state_ref[...]
    wm = _dot3(jnp.concatenate([x[:, :, :A_HEAD], r_h], axis=1), m, _BNN)
    u = -(wm[:, :c] + x[:, :, A_HEAD:])
    uv = jnp.concatenate([u, v], axis=1)
    y = wm[:, c:] + _dot3(jnp.concatenate([r_b, r_k], axis=2), uv, _BNN)
    decay_diag = jnp.where(eye, jnp.exp(cend), 0.0)
    state_ref[...] = _dot3(jnp.concatenate([b_t, k_t, decay_diag], axis=1),
                           jnp.concatenate([uv, m], axis=1), _BTN)
    ym = jnp.mean(y, axis=-1, keepdims=True)
    yc = y - ym
    yv = jnp.mean(yc * yc, axis=-1, keepdims=True)
    y_ref[...] = yc * lax.rsqrt(yv + A_GN_EPS)


def rwkv_chunks(r, lw, k, v, kap, b):
    _, s, _ = r.shape
    spec = pl.BlockSpec((A_HEADS, CHUNK, A_HEAD), lambda n: (0, n, 0))
    return pl.pallas_call(
        _rwkv_chunk_kernel,
        grid=(s // CHUNK,),
        in_specs=[spec] * 6,
        out_specs=spec,
        out_shape=jax.ShapeDtypeStruct((A_HEADS, s, A_HEAD), F32),
        scratch_shapes=[pltpu.VMEM((A_HEADS, A_HEAD, A_HEAD), F32)],
        compiler_params=_cparams("arbitrary"),
    )(r, lw, k, v, kap, b)


def _rwkv_post_kernel(y_ref, bonus_ref, g_ref, gg_ref, gb_ref, o_ref):
    y = jnp.concatenate([y_ref[h] for h in range(A_HEADS)], axis=1)
    o_ref[...] = ((y * gg_ref[...] + gb_ref[...] + bonus_ref[...]) * g_ref[...]).astype(o_ref.dtype)


def rwkv_post(y, bonus, g, gn_g, gn_b, *, tm):
    _, s, _ = y.shape
    row = lambda i: (i, 0)
    const = lambda i: (0, 0)
    return pl.pallas_call(
        _rwkv_post_kernel,
        grid=(s // tm,),
        in_specs=[pl.BlockSpec((A_HEADS, tm, A_HEAD), lambda i: (0, i, 0)),
                  pl.BlockSpec((tm, A_WIDTH), row), pl.BlockSpec((tm, A_WIDTH), row),
                  pl.BlockSpec((1, A_WIDTH), const), pl.BlockSpec((1, A_WIDTH), const)],
        out_specs=pl.BlockSpec((tm, A_WIDTH), row),
        out_shape=jax.ShapeDtypeStruct((s, A_WIDTH), BF16),
        compiler_params=_cparams("parallel"),
    )(y, bonus, g, gn_g.reshape(1, -1), gn_b.reshape(1, -1))


def _rglru_kernel(xb_ref, gate_ref, halo_ref, cw_ref, cb_ref, wr_ref, br_ref, wi_ref, bi_ref, lam_ref,
                  o_ref, xe_ref, a_ref, u_ref, h_ref, carry_ref, *, tm):
    i = pl.program_id(0)

    @pl.when(i == 0)
    def _():
        carry_ref[...] = jnp.zeros_like(carry_ref)

    xe_ref[0:HALO, :] = halo_ref[...] * (i > 0).astype(F32)
    xe_ref[HALO:, :] = xb_ref[...]
    xc = cb_ref[...] + cw_ref[B_CONV - 1:B_CONV, :] * xe_ref[HALO:, :]
    for d in range(1, B_CONV):
        xc = xc + cw_ref[B_CONV - 1 - d:B_CONV - d, :] * xe_ref[HALO - d:HALO - d + tm, :]
    xcb = xc.astype(BF16)
    r = jax.nn.sigmoid(_dot(xcb, wr_ref[...]) + br_ref[...])
    gi = jax.nn.sigmoid(_dot(xcb, wi_ref[...]) + bi_ref[...])
    log_a = -B_C * r * _softplus(-lam_ref[...])
    a = jnp.exp(log_a)
    a_ref[...] = a
    u_ref[...] = jnp.sqrt(-jnp.tanh(log_a) * (a * a + 1.0)) * (gi * xc)

    def group(gidx, h):
        base = pl.multiple_of(gidx * HALO, HALO)
        a8 = a_ref[pl.ds(base, HALO), :]
        u8 = u_ref[pl.ds(base, HALO), :]
        rows = []
        for rr in range(HALO):
            h = a8[rr:rr + 1, :] * h + u8[rr:rr + 1, :]
            rows.append(h)
        h_ref[pl.ds(base, HALO), :] = jnp.concatenate(rows, axis=0)
        return h

    carry_ref[...] = lax.fori_loop(0, tm // HALO, group, carry_ref[...])
    o_ref[...] = (jax.nn.gelu(gate_ref[...]) * h_ref[...]).astype(o_ref.dtype)


def rglru(z_b, conv_w, conv_b, wr, b_r, wi, b_i, lam, *, tm):
    s = z_b.shape[0]
    hb = tm // HALO
    row = lambda i: (i, 0)
    const = lambda i: (0, 0)
    vec = lambda a: a.reshape(1, -1)
    return pl.pallas_call(
        functools.partial(_rglru_kernel, tm=tm),
        grid=(s // tm,),
        in_specs=[pl.BlockSpec((tm, B_WIDTH), row), pl.BlockSpec((tm, B_WIDTH), lambda i: (i, 1)),
                  pl.BlockSpec((HALO, B_WIDTH), lambda i: (jnp.maximum(i * hb - 1, 0), 0)),
                  pl.BlockSpec((B_CONV, B_WIDTH), const), pl.BlockSpec((1, B_WIDTH), const),
                  pl.BlockSpec((B_WIDTH, B_WIDTH), const), pl.BlockSpec((1, B_WIDTH), const),
                  pl.BlockSpec((B_WIDTH, B_WIDTH), const), pl.BlockSpec((1, B_WIDTH), const),
                  pl.BlockSpec((1, B_WIDTH), const)],
        out_specs=pl.BlockSpec((tm, B_WIDTH), row),
        out_shape=jax.ShapeDtypeStruct((s, B_WIDTH), BF16),
        scratch_shapes=[pltpu.VMEM((tm + HALO, B_WIDTH), F32), pltpu.VMEM((tm, B_WIDTH), F32),
                        pltpu.VMEM((tm, B_WIDTH), F32), pltpu.VMEM((tm, B_WIDTH), F32),
                        pltpu.VMEM((1, B_WIDTH), F32)],
        compiler_params=_cparams("arbitrary"),
    )(z_b, z_b, z_b, conv_w, vec(conv_b), wr, vec(b_r), wi, vec(b_i), vec(lam))


ODD_PAD = 1024
ODD_KIDX_AT = 768
ODD_WIDX_AT = 896
QB = 256
SCORE_SCALE = (IDX_HEADS ** -0.5) * (IDX_DIM ** -0.5)
INT_MIN = -(2 ** 31)
CHUNK_SHIFT = 6
CUT_BITS = 14
HEAD_GROUP = 2


def _dsa_in_kernel(x_ref, w_ref, qn_ref, kvn_ref, kg_ref, kb_ref, cq_o, ckv_o, kidx_o, widx_o):
    acc = _dot(x_ref[...], w_ref[...])
    cq = acc[:, 0:C_Q_RANK]
    ckv = acc[:, C_Q_RANK:C_Q_RANK + C_KV_RANK]
    kidx = acc[:, ODD_KIDX_AT:ODD_KIDX_AT + IDX_DIM]
    widx = acc[:, ODD_WIDX_AT:ODD_WIDX_AT + IDX_HEADS]
    rms = lambda t, g: t * lax.rsqrt(jnp.mean(t * t, axis=-1, keepdims=True) + 1e-6) * g
    cq_o[...] = rms(cq, qn_ref[...]).astype(BF16)
    ckv_o[...] = rms(ckv, kvn_ref[...]).astype(BF16)
    kidx_o[...] = _layer_norm_rows(kidx, kg_ref[...], kb_ref[...], LN_EPS).astype(BF16)
    widx_o[...] = widx * SCORE_SCALE


def dsa_in(xb, w_pad, q_norm, kv_norm, kidx_g, kidx_b, *, tm):
    s, d = xb.shape
    row = lambda i: (i, 0)
    const = lambda i: (0, 0)
    vec = lambda a: a.reshape(1, -1)
    return pl.pallas_call(
        _dsa_in_kernel,
        grid=(s // tm,),
        in_specs=[pl.BlockSpec((tm, d), row), pl.BlockSpec((d, ODD_PAD), const),
                  pl.BlockSpec((1, C_Q_RANK), const), pl.BlockSpec((1, C_KV_RANK), const),
                  pl.BlockSpec((1, IDX_DIM), const), pl.BlockSpec((1, IDX_DIM), const)],
        out_specs=[pl.BlockSpec((tm, C_Q_RANK), row), pl.BlockSpec((tm, C_KV_RANK), row),
                   pl.BlockSpec((tm, IDX_DIM), row), pl.BlockSpec((tm, IDX_HEADS), row)],
        out_shape=[jax.ShapeDtypeStruct((s, C_Q_RANK), BF16), jax.ShapeDtypeStruct((s, C_KV_RANK), BF16),
                   jax.ShapeDtypeStruct((s, IDX_DIM), BF16), jax.ShapeDtypeStruct((s, IDX_HEADS), F32)],
        compiler_params=_cparams("parallel"),
    )(xb, w_pad, vec(q_norm), vec(kv_norm), vec(kidx_g), vec(kidx_b))


def _qabs_kernel(cq_ref, wuq_ref, wuk_ref, o_ref):
    q = _dot(cq_ref[...], wuq_ref[...]).astype(BF16)
    qa = _dot(q, wuk_ref[...], _NT)
    o_ref[...] = (qa * (C_HEAD_DIM ** -0.5 * LOG2E)).astype(BF16)


def dsa_qabs(cq, w_uq, w_uk, *, tm):
    s = cq.shape[0]
    return pl.pallas_call(
        _qabs_kernel,
        grid=(s // tm, C_HEADS),
        in_specs=[pl.BlockSpec((tm, C_Q_RANK), lambda i, h: (i, 0)),
                  pl.BlockSpec((C_Q_RANK, C_HEAD_DIM), lambda i, h: (0, h)),
                  pl.BlockSpec((None, C_KV_RANK, C_HEAD_DIM), lambda i, h: (h, 0, 0))],
        out_specs=pl.BlockSpec((None, tm, C_KV_RANK), lambda i, h: (h, i, 0)),
        out_shape=jax.ShapeDtypeStruct((C_HEADS, s, C_KV_RANK), BF16),
        compiler_params=_cparams("parallel", "parallel"),
    )(cq, w_uq, w_uk)


def _sortable_key(score):
    bits = lax.bitcast_convert_type(score + 0.0, jnp.int32)
    return jnp.where(bits < 0, bits ^ jnp.int32(0x7FFFFFFF), bits)


def _dsa_attn_kernel(qabs_ref, qidx_ref, widx_t_ref, kidx_ref, ckv_ref, wuv_ref, bias_ref,
                     o_ref, keys_ref, acc_ref, m_ref, l_ref, *, k_sel):
    i = pl.program_id(0)
    n_tiles = i + 1
    k_local = lax.broadcasted_iota(jnp.int32, (QB, QB), 0)
    q_local = lax.broadcasted_iota(jnp.int32, (QB, QB), 1)
    allowed_diag_t = (k_local >> CHUNK_SHIFT) <= (q_local >> CHUNK_SHIFT)

    widx_t = widx_t_ref[...]

    def score_tile(t, carry):
        off = pl.multiple_of(t * QB, QB)
        kt = kidx_ref[pl.ds(off, QB), :]
        sc = jnp.zeros((QB, QB), F32)
        for j in range(IDX_HEADS):
            d = _dot(kt, qidx_ref[:, j * IDX_DIM:(j + 1) * IDX_DIM], _NT)
            sc = sc + widx_t[j:j + 1, :] * jnp.maximum(d, 0.0)
        key = _sortable_key(sc)
        key = jnp.where(jnp.logical_or(t < i, allowed_diag_t), key, jnp.int32(INT_MIN))
        keys_ref[pl.ds(off, QB), :] = key
        return carry

    lax.fori_loop(0, n_tiles, score_tile, 0)

    def selected(kt, off, thr, cut):
        pos = lax.broadcasted_iota(jnp.int32, kt.shape, 0) + off
        return jnp.logical_or(kt > thr, jnp.logical_and(kt == thr, pos < cut))

    def count(pred):
        def body(t, acc):
            off = pl.multiple_of(t * QB, QB)
            hit = pred(keys_ref[pl.ds(off, QB), :], off).astype(jnp.int32)
            return acc + jnp.sum(hit.reshape(QB // HALO, HALO, QB), axis=0)
        acc = lax.fori_loop(0, n_tiles, body, jnp.zeros((HALO, QB), jnp.int32))
        return jnp.sum(acc.astype(F32), axis=0, keepdims=True).astype(jnp.int32)

    def thr_bit(it, thr):
        cand = thr + jnp.left_shift(jnp.int32(1), 31 - it)
        cnt = count(lambda kt, off: kt >= cand)
        return jnp.where(cnt >= k_sel, cand, thr)

    thr = lax.fori_loop(0, 32, thr_bit, jnp.full((1, QB), INT_MIN, jnp.int32))

    def cut_bit(it, cut):
        cand = cut + jnp.left_shift(jnp.int32(1), CUT_BITS - 1 - it)
        cnt = count(lambda kt, off: selected(kt, off, thr, cand))
        return jnp.where(cnt <= k_sel, cand, cut)

    n_ge = count(lambda kt, off: kt >= thr)
    tied = jnp.logical_and(n_ge > k_sel, thr > INT_MIN)
    any_tied = jnp.max(jnp.where(tied, 1.0, 0.0)) > 0.0
    cut = lax.cond(any_tied,
                   lambda: lax.fori_loop(0, CUT_BITS, cut_bit, jnp.zeros((1, QB), jnp.int32)),
                   lambda: jnp.full((1, QB), 2 ** CUT_BITS - 1, jnp.int32))
    rep_lanes = lambda a, n: jnp.concatenate([a] * (n // LANES), axis=-1)

    m_ref[...] = jnp.full(m_ref.shape, NEG_BIG, F32)
    l_ref[...] = jnp.zeros(l_ref.shape, F32)
    acc_ref[...] = jnp.zeros(acc_ref.shape, F32)

    def attend(off, width, near):
        sel = selected(keys_ref[pl.ds(off, width), :], off, thr, cut)
        if near == 1:
            sel = jnp.logical_and(sel, allowed_diag_t)
        mask_add = jnp.where(sel, 0.0, NEG_BIG).T[None]
        kv = ckv_ref[pl.ds(off, width), :]
        rep = lambda a, n: jnp.concatenate([a] * (n // LANES), axis=-1)

        for g in range(C_HEADS // HEAD_GROUP):
            hs = slice(g * HEAD_GROUP, (g + 1) * HEAD_GROUP)
            q = qabs_ref[hs].reshape(HEAD_GROUP * QB, C_KV_RANK)
            s = _dot(q, kv, _NT).reshape(HEAD_GROUP, QB, width) + mask_add
            if near is not None:
                s = s + bias_ref[near, hs]
            m_old = m_ref[hs]
            row_max = jnp.broadcast_to(jnp.max(s, axis=-1, keepdims=True), m_old.shape)
            m_new = jnp.maximum(m_old, row_max)
            alpha = jnp.exp2(m_old - m_new)
            p = jnp.exp2(s - rep(m_new, width))
            row_sum = jnp.broadcast_to(jnp.sum(p, axis=-1, keepdims=True), m_old.shape)
            l_ref[hs] = alpha * l_ref[hs] + row_sum
            pv = _dot(p.astype(BF16).reshape(HEAD_GROUP * QB, width), kv).reshape(HEAD_GROUP, QB, C_KV_RANK)
            acc_ref[hs] = rep(alpha, C_KV_RANK) * acc_ref[hs] + pv
            m_ref[hs] = m_new

    n_far = jnp.maximum(i - 1, 0)

    def far_pair(t2, carry):
        attend(pl.multiple_of(t2 * (2 * QB), 2 * QB), 2 * QB, None)
        return carry

    lax.fori_loop(0, n_far // 2, far_pair, 0)

    @pl.when(n_far % 2 == 1)
    def _():
        attend(pl.multiple_of((n_far - 1) * QB, QB), QB, None)

    @pl.when(i >= 1)
    def _():
        attend(pl.multiple_of((i - 1) * QB, QB), QB, 0)

    attend(pl.multiple_of(i * QB, QB), QB, 1)

    for h in range(C_HEADS):
        o_lat = (acc_ref[h] / rep_lanes(l_ref[h], C_KV_RANK)).astype(BF16)
        o_ref[:, h * C_HEAD_DIM:(h + 1) * C_HEAD_DIM] = _dot(o_lat, wuv_ref[h]).astype(o_ref.dtype)


def dsa_attention(qabs, qidx, widx_t, kidx, ckv, w_uv, bias_near, *, k_sel):
    _, s, _ = qabs.shape
    full2 = lambda i: (0, 0)
    once = pl.Buffered(1)
    return pl.pallas_call(
        functools.partial(_dsa_attn_kernel, k_sel=k_sel),
        grid=(s // QB,),
        in_specs=[pl.BlockSpec((C_HEADS, QB, C_KV_RANK), lambda i: (0, i, 0)),
                  pl.BlockSpec((QB, IDX_HEADS * IDX_DIM), lambda i: (i, 0)),
                  pl.BlockSpec((IDX_HEADS, QB), lambda i: (0, i)),
                  pl.BlockSpec((s, IDX_DIM), full2, pipeline_mode=once),
                  pl.BlockSpec((s, C_KV_RANK), full2, pipeline_mode=once),
                  pl.BlockSpec((C_HEADS, C_KV_RANK, C_HEAD_DIM), lambda i: (0, 0, 0), pipeline_mode=once),
                  pl.BlockSpec((2, C_HEADS, QB, QB), lambda i: (0, 0, 0, 0), pipeline_mode=once)],
        out_specs=pl.BlockSpec((QB, C_HEADS * C_HEAD_DIM), lambda i: (i, 0)),
        out_shape=jax.ShapeDtypeStruct((s, C_HEADS * C_HEAD_DIM), BF16),
        scratch_shapes=[pltpu.VMEM((s, QB), jnp.int32),
                        pltpu.VMEM((C_HEADS, QB, C_KV_RANK), F32),
                        pltpu.VMEM((C_HEADS, QB, LANES), F32),
                        pltpu.VMEM((C_HEADS, QB, LANES), F32)],
        compiler_params=_cparams("parallel"),
    )(qabs, qidx, widx_t, kidx, ckv, w_uv, bias_near)


def _t5_bucket(rel):
    nb = REL_BUCKETS // 2
    max_exact = nb // 2
    ret = jnp.where(rel > 0, nb, 0)
    n = jnp.abs(rel)
    nf = jnp.maximum(n, 1).astype(jnp.float32)
    large = max_exact + (jnp.log(nf / max_exact) / math.log(REL_MAX_DIST / max_exact) * (nb - max_exact)).astype(jnp.int32)
    large = jnp.minimum(large, nb - 1)
    return ret + jnp.where(n < max_exact, n, large)


def _near_bias(rel_bias):
    ql = jnp.arange(QB)[:, None]
    sl = jnp.arange(QB)[None, :]
    rel = jnp.stack([sl - ql - QB, sl - ql])
    far = rel_bias[_t5_bucket(jnp.array(-2 * QB))]
    table = ((rel_bias - far) * LOG2E).astype(F32)
    onehot = jax.nn.one_hot(_t5_bucket(rel), REL_BUCKETS, dtype=F32)
    return jnp.einsum('tqsb,bh->thqs', onehot, table, precision=lax.Precision.HIGHEST)


def _pad_rows(w, at, total):
    return jnp.zeros((total, w.shape[1]), w.dtype).at[at:at + w.shape[0]].set(w)


def _block_diag(w):
    n, d, e = w.shape
    eye = jnp.eye(n, dtype=w.dtype)
    return (eye[:, None, :, None] * w[:, :, None, :]).reshape(n * d, n * e)


def _even_mixer(x, xb, w_in, w_out, mu, w0, w2, a0, a2, g2, k_k, k_a, r_k, gn_g, gn_b,
                conv_w, conv_b, w_r, b_r, w_i, b_i, lam, ln_g, ln_b, tm):
    w_in = w_in.astype(BF16)
    n_rkv = 3 * A_WIDTH
    z_rkv = matmul(xb, w_in[:, :n_rkv], tm=tm, tn=1024)
    w_lo = jnp.pad(w_in[:, n_rkv:A_COLS], ((0, 0), (0, A_LORA_PAD - A_LORA)))
    z_lo = matmul(xb, w_lo, tm=tm, tn=A_LORA_PAD)
    z_b = matmul(xb, w_in[:, A_COLS:], tm=tm, tn=1024)

    mu_lo = jnp.pad(mu[n_rkv:], (0, A_LORA_PAD - A_LORA))
    w2p = _pad_rows(w2, 0, A_LORA_PAD).astype(BF16)
    a2p = _pad_rows(a2, A_DECAY_LORA, A_LORA_PAD).astype(BF16)
    g2p = _pad_rows(g2, A_DECAY_LORA + A_ICL_LORA, A_LORA_PAD).astype(BF16)
    bd = _block_diag(jnp.ones((A_HEADS, A_HEAD, A_HEAD), BF16))
    r, lw, k2, v, kap, b, g, bonus = rwkv_prep(
        z_rkv, z_lo, mu[:n_rkv], mu_lo, w0, w2p, a0, a2p, g2p, k_k, k_a, r_k.reshape(-1), bd,
        tm=min(tm, 256))
    y = rwkv_chunks(r, lw, k2, v, kap, b)
    y_a = rwkv_post(y, bonus, g, gn_g, gn_b, tm=tm)

    y_b = rglru(z_b, conv_w, conv_b, _block_diag(w_r).astype(BF16), b_r,
                _block_diag(w_i).astype(BF16), b_i, lam, tm=min(tm, 256))
    y = jnp.concatenate([y_a, y_b], axis=1)
    return matmul_residual_ln(y, w_out.astype(BF16), x, ln_g, ln_b, tm=tm, tk=y.shape[1])


def _odd_mixer(x, xb, w_in, w_out, q_norm, kv_norm, w_uq, w_uk, w_uv, w_qidx, kidx_g, kidx_b,
               bias_near, ln_g, ln_b, tm):
    s = x.shape[0]
    d = w_in.shape[0]
    n_qkv = C_Q_RANK + C_KV_RANK
    w_pad = jnp.zeros((d, ODD_PAD), BF16)
    w_pad = w_pad.at[:, :n_qkv].set(w_in[:, :n_qkv].astype(BF16))
    w_pad = w_pad.at[:, ODD_KIDX_AT:ODD_KIDX_AT + IDX_DIM].set(w_in[:, n_qkv:n_qkv + IDX_DIM].astype(BF16))
    w_pad = w_pad.at[:, ODD_WIDX_AT:ODD_WIDX_AT + IDX_HEADS].set(w_in[:, n_qkv + IDX_DIM:].astype(BF16))
    cq, ckv, kidx, widx = dsa_in(xb, w_pad, q_norm, kv_norm, kidx_g, kidx_b, tm=tm)
    qabs = dsa_qabs(cq, w_uq.astype(BF16), w_uk.astype(BF16), tm=tm)
    qidx = matmul(cq, w_qidx.astype(BF16), tm=tm, tn=IDX_HEADS * IDX_DIM, out_dtype=BF16)
    o = dsa_attention(qabs, qidx, widx.T, kidx, ckv, w_uv.astype(BF16), bias_near,
                      k_sel=min(TOPK_MAX, s // 4))
    return matmul_residual_ln(o, w_out.astype(BF16), x, ln_g, ln_b, tm=tm, tk=o.shape[1])


def kernel(x, p, rel_bias, ln1_g, ln1_b, ln2_g, ln2_b, ffn_w_up, ffn_conv_w, ffn_conv_b, ffn_w_down, ple_w_proj, ple_w_gate, ev_w_in, ev_w_out, a_mu, a_w0, a_w2, a_a0, a_a2, a_g2, a_k_k, a_k_a, a_r_k, a_gn_g, a_gn_b, b_conv_w, b_conv_b, b_w_r, b_b_r, b_w_i, b_b_i, b_lambda, od_w_in, od_w_out, c_q_norm, c_kv_norm, c_w_uq, c_w_uk, c_w_uv, c_w_qidx, c_kidx_g, c_kidx_b):
    bsz, s, d = x.shape
    assert bsz == 1 and s % QB == 0 and s <= 2 ** CUT_BITS
    tm = min(512, s)
    x = x[0]
    xb = x.astype(BF16)
    bias_near = _near_bias(rel_bias)
    for layer in range(DEPTH):
        j = layer // 2
        if layer % 2 == 0:
            x, xb = _even_mixer(x, xb, ev_w_in[j], ev_w_out[j], a_mu[j], a_w0[j], a_w2[j], a_a0[j], a_a2[j],
                                a_g2[j], a_k_k[j], a_k_a[j], a_r_k[j], a_gn_g[j], a_gn_b[j],
                                b_conv_w[j], b_conv_b[j], b_w_r[j], b_b_r[j], b_w_i[j], b_b_i[j], b_lambda[j],
                                ln1_g[layer], ln1_b[layer], tm)
        else:
            x, xb = _odd_mixer(x, xb, od_w_in[j], od_w_out[j], c_q_norm[j], c_kv_norm[j], c_w_uq[j], c_w_uk[j],
                               c_w_uv[j], c_w_qidx[j], c_kidx_g[j], c_kidx_b[j], bias_near,
                               ln1_g[layer], ln1_b[layer], tm)
        hm = ffn_up(xb, ffn_w_up[layer].astype(BF16), ffn_conv_w[layer], ffn_conv_b[layer], tm=tm, tc=512)
        x, xb = matmul_residual_ln(hm, ffn_w_down[layer].astype(BF16), x, ln2_g[layer], ln2_b[layer],
                                   tm=tm, tk=D_FF // 4)
        x, xb = ple_update(xb, ple_w_gate[layer].astype(BF16), p[layer, 0].astype(BF16),
                           ple_w_proj[layer].astype(BF16), x, tm=tm, tn=1024)
    return x[None]
```

```python
import functools
import math

import jax
import jax.numpy as jnp
import numpy as np
from jax import lax
from jax.experimental import pallas as pl
from jax.experimental.pallas import tpu as pltpu

F32 = jnp.float32
BF16 = jnp.bfloat16

D_MODEL = 2048
DEPTH = 4
CHUNK = 64
DN_ALPHA = (2 * DEPTH) ** 0.25
LN_EPS = 1e-5
A_WIDTH = 1024
A_HEAD = 64
A_HEADS = 16
A_DECAY_LORA = 64
A_ICL_LORA = 64
A_GATE_LORA = 160
A_LORA = A_DECAY_LORA + A_ICL_LORA + A_GATE_LORA
A_LORA_PAD = 384
A_COLS = 3 * A_WIDTH + A_LORA
A_GN_EPS = 64e-5
B_WIDTH = 1024
B_BLOCKS = 16
B_BLOCK = 64
B_CONV = 4
B_C = 8.0
C_HEADS = 16
C_HEAD_DIM = 128
C_Q_RANK = 512
C_KV_RANK = 256
IDX_HEADS = 16
IDX_DIM = 64
TOPK_MAX = 256
REL_BUCKETS = 32
REL_MAX_DIST = 128
D_FF = 5632
FFN_CONV = 3
PLE_DIM = 256

VMEM_LIMIT_BYTES = 56 * 1024 * 1024
NEG_BIG = -1e30
HALO = 8
LANES = 128
LOG2E = math.log2(math.e)


def _cparams(*sem):
    return pltpu.CompilerParams(dimension_semantics=sem, vmem_limit_bytes=VMEM_LIMIT_BYTES)


def _split_bf16(a):
    hi = a.astype(BF16)
    lo = (a - hi.astype(F32)).astype(BF16)
    return hi, lo


def _dot(a, b, dims=(((1,), (0,)), ((), ()))):
    return lax.dot_general(a, b, dims, preferred_element_type=F32)


_NT = (((1,), (1,)), ((), ()))
_TN = (((0,), (0,)), ((), ()))
_NN = (((1,), (0,)), ((), ()))


def _dot3(a, b, dims=_NN):
    ah, al = _split_bf16(a)
    bh, bl = _split_bf16(b)
    return _dot(ah, bh, dims) + (_dot(ah, bl, dims) + _dot(al, bh, dims))


def _dot_exact_rhs(a, b_bf16, dims=_NN):
    hi = a.astype(BF16)
    r1 = a - hi.astype(F32)
    mid = r1.astype(BF16)
    lo = (r1 - mid.astype(F32)).astype(BF16)
    return _dot(hi, b_bf16, dims) + (_dot(mid, b_bf16, dims) + _dot(lo, b_bf16, dims))


def _layer_norm_rows(v, g, b, eps):
    mu = jnp.mean(v, axis=-1, keepdims=True)
    d = v - mu
    var = jnp.mean(d * d, axis=-1, keepdims=True)
    return d * lax.rsqrt(var + eps) * g + b


def _softplus(x):
    return jnp.maximum(x, 0.0) + jnp.log1p(jnp.exp(-jnp.abs(x)))


def _mm_kernel(a_ref, b_ref, o_ref):
    o_ref[...] = _dot(a_ref[...], b_ref[...]).astype(o_ref.dtype)


def matmul(a, b, *, tm, tn, out_dtype=F32):
    m, k = a.shape
    _, n = b.shape
    assert m % tm == 0 and n % tn == 0
    return pl.pallas_call(
        _mm_kernel,
        grid=(n // tn, m // tm),
        in_specs=[pl.BlockSpec((tm, k), lambda j, i: (i, 0)),
                  pl.BlockSpec((k, tn), lambda j, i: (0, j))],
        out_specs=pl.BlockSpec((tm, tn), lambda j, i: (i, j)),
        out_shape=jax.ShapeDtypeStruct((m, n), out_dtype),
        compiler_params=_cparams("parallel", "parallel"),
    )(a, b)


def _mm_ln_kernel(a_ref, b_ref, x_ref, g_ref, beta_ref, o_ref, ob_ref, acc_ref, *, nk):
    kk = pl.program_id(1)

    @pl.when(kk == 0)
    def _():
        acc_ref[...] = jnp.zeros_like(acc_ref)

    acc_ref[...] += _dot(a_ref[...], b_ref[...])

    @pl.when(kk == nk - 1)
    def _():
        v = DN_ALPHA * x_ref[...] + acc_ref[...]
        y = _layer_norm_rows(v, g_ref[...], beta_ref[...], LN_EPS)
        o_ref[...] = y
        ob_ref[...] = y.astype(BF16)


def matmul_residual_ln(a, b, layer, x, g, beta, *, tm, tk):
    m, k = a.shape
    n = b.shape[2]
    assert m % tm == 0 and k % tk == 0
    nk = k // tk
    return pl.pallas_call(
        functools.partial(_mm_ln_kernel, nk=nk),
        grid=(m // tm, nk),
        in_specs=[pl.BlockSpec((tm, tk), lambda i, kk: (i, kk)),
                  pl.BlockSpec((None, tk, n), lambda i, kk: (layer, kk, 0)),
                  pl.BlockSpec((tm, n), lambda i, kk: (i, 0)),
                  pl.BlockSpec((1, n), lambda i, kk: (0, 0)),
                  pl.BlockSpec((1, n), lambda i, kk: (0, 0))],
        out_specs=[pl.BlockSpec((tm, n), lambda i, kk: (i, 0)),
                   pl.BlockSpec((tm, n), lambda i, kk: (i, 0))],
        out_shape=[jax.ShapeDtypeStruct((m, n), F32), jax.ShapeDtypeStruct((m, n), BF16)],
        scratch_shapes=[pltpu.VMEM((tm, n), F32)],
        compiler_params=_cparams("parallel", "arbitrary"),
    )(a, b, x, g.reshape(1, n), beta.reshape(1, n))


def _mm2_ln_kernel(a1_ref, a2_ref, b1_ref, b2_ref, x_ref, g_ref, beta_ref, o_ref, ob_ref):
    acc = _dot(a1_ref[...], b1_ref[...]) + _dot(a2_ref[...], b2_ref[...])
    y = _layer_norm_rows(DN_ALPHA * x_ref[...] + acc, g_ref[...], beta_ref[...], LN_EPS)
    o_ref[...] = y
    ob_ref[...] = y.astype(BF16)


def matmul2_residual_ln(a1, a2, b, layer, x, g, beta, *, tm):
    m, k1 = a1.shape
    k2 = a2.shape[1]
    n = b.shape[2]
    assert k1 == k2
    row = lambda i: (i, 0)
    const = lambda i: (0, 0)
    return pl.pallas_call(
        _mm2_ln_kernel,
        grid=(m // tm,),
        in_specs=[pl.BlockSpec((tm, k1), row), pl.BlockSpec((tm, k2), row),
                  pl.BlockSpec((None, k1, n), lambda i: (layer, 0, 0)),
                  pl.BlockSpec((None, k2, n), lambda i: (layer, 1, 0)),
                  pl.BlockSpec((tm, n), row), pl.BlockSpec((1, n), const), pl.BlockSpec((1, n), const)],
        out_specs=[pl.BlockSpec((tm, n), row), pl.BlockSpec((tm, n), row)],
        out_shape=[jax.ShapeDtypeStruct((m, n), F32), jax.ShapeDtypeStruct((m, n), BF16)],
        compiler_params=_cparams("parallel"),
    )(a1, a2, b, b, x, g.reshape(1, n), beta.reshape(1, n))


def _ple_kernel(xb_ref, wg_ref, p_ref, wp_ref, x_ref, o_ref, ob_ref):
    gate = jax.nn.sigmoid(_dot(xb_ref[...], wg_ref[...]))
    proj = _dot(p_ref[...], wp_ref[...])
    y = x_ref[...] + gate * proj
    o_ref[...] = y
    ob_ref[...] = y.astype(BF16)


def ple_update(xb, wg, pb, wp, layer, x, *, tm, tn):
    m, d = xb.shape
    pd = pb.shape[-1]
    return pl.pallas_call(
        _ple_kernel,
        grid=(d // tn, m // tm),
        in_specs=[pl.BlockSpec((tm, d), lambda j, i: (i, 0)),
                  pl.BlockSpec((None, d, tn), lambda j, i: (layer, 0, j)),
                  pl.BlockSpec((None, None, tm, pd), lambda j, i: (layer, 0, i, 0)),
                  pl.BlockSpec((None, pd, tn), lambda j, i: (layer, 0, j)),
                  pl.BlockSpec((tm, tn), lambda j, i: (i, j))],
        out_specs=[pl.BlockSpec((tm, tn), lambda j, i: (i, j)),
                   pl.BlockSpec((tm, tn), lambda j, i: (i, j))],
        out_shape=[jax.ShapeDtypeStruct((m, d), F32), jax.ShapeDtypeStruct((m, d), BF16)],
        compiler_params=_cparams("parallel", "parallel"),
    )(xb, wg, pb, wp, x)


def _ffn_up_kernel(x_ref, wg_ref, wu_ref, cwg_ref, cwu_ref, cbg_ref, cbu_ref, o_ref, eg_ref, eu_ref, *, tm, sub):
    i = pl.program_id(1)

    @pl.when(i == 0)
    def _():
        eg_ref[0:HALO, :] = jnp.zeros((HALO, eg_ref.shape[1]), F32)
        eu_ref[0:HALO, :] = jnp.zeros((HALO, eu_ref.shape[1]), F32)

    @pl.when(i > 0)
    def _():
        eg_ref[0:HALO, :] = eg_ref[tm:tm + HALO, :]
        eu_ref[0:HALO, :] = eu_ref[tm:tm + HALO, :]

    tc = o_ref.shape[1]
    halves = ((0, tc // 2), (tc // 2, tc))

    def project(k, e_ref, w_ref):
        e_ref[HALO + k * sub:HALO + (k + 1) * sub, :] = _dot(x_ref[k * sub:(k + 1) * sub, :], w_ref[...])

    def conv(e_ref, w_ref, b_ref, base, c0, c1):
        acc = b_ref[:, c0:c1] + w_ref[FFN_CONV - 1:FFN_CONV, c0:c1] * e_ref[base:base + sub, c0:c1]
        for d in range(1, FFN_CONV):
            acc = acc + w_ref[FFN_CONV - 1 - d:FFN_CONV - d, c0:c1] * e_ref[base - d:base - d + sub, c0:c1]
        return acc

    def gate_rows(k, c0, c1):
        base = HALO + k * sub
        gate = conv(eg_ref, cwg_ref, cbg_ref, base, c0, c1)
        up = conv(eu_ref, cwu_ref, cbu_ref, base, c0, c1)
        o_ref[k * sub:(k + 1) * sub, c0:c1] = (jax.nn.gelu(gate) * up).astype(o_ref.dtype)

    project(0, eg_ref, wg_ref)
    project(0, eu_ref, wu_ref)
    for k in range(tm // sub):
        more = (k + 1) * sub < tm
        if more:
            project(k + 1, eg_ref, wg_ref)
        gate_rows(k, *halves[0])
        if more:
            project(k + 1, eu_ref, wu_ref)
        gate_rows(k, *halves[1])


def ffn_up(xb, w_up, layer, conv_w, conv_b, *, tm, sub, tc):
    s, d = xb.shape
    two_ff = w_up.shape[2]
    ff = two_ff // 2
    nc = ff // tc
    cb = conv_b.reshape(1, two_ff)
    return pl.pallas_call(
        functools.partial(_ffn_up_kernel, tm=tm, sub=sub),
        grid=(nc, s // tm),
        in_specs=[pl.BlockSpec((tm, d), lambda j, i: (i, 0)),
                  pl.BlockSpec((None, d, tc), lambda j, i: (layer, 0, j)),
                  pl.BlockSpec((None, d, tc), lambda j, i: (layer, 0, j + nc)),
                  pl.BlockSpec((FFN_CONV, tc), lambda j, i: (0, j)),
                  pl.BlockSpec((FFN_CONV, tc), lambda j, i: (0, j + nc)),
                  pl.BlockSpec((1, tc), lambda j, i: (0, j)),
                  pl.BlockSpec((1, tc), lambda j, i: (0, j + nc))],
        out_specs=pl.BlockSpec((tm, tc), lambda j, i: (i, j)),
        out_shape=jax.ShapeDtypeStruct((s, ff), BF16),
        scratch_shapes=[pltpu.VMEM((tm + HALO, tc), F32), pltpu.VMEM((tm + HALO, tc), F32)],
        compiler_params=_cparams("parallel", "arbitrary"),
    )(xb, w_up, w_up, conv_w, conv_w, cb, cb)


def _shift_mix(z, halo, mu, live):
    prev = pltpu.roll(z, 1, 0)
    row0 = lax.broadcasted_iota(jnp.int32, z.shape, 0) == 0
    prev = jnp.where(row0, halo[HALO - 1:HALO, :] * live, prev)
    return z + (prev - z) * mu


def _rwkv_prep_kernel(z_ref, zh_ref, lo_ref, loh_ref, mu_ref, mulo_ref, w0_ref, w2_ref, a0_ref, a2_ref,
                      g2_ref, kk_ref, ka_ref, rk_ref, bd_ref,
                      r_o, lw_o, k_o, v_o, kap_o, b_o, g_o, bonus_o):
    live = (pl.program_id(0) > 0).astype(F32)
    z = _shift_mix(z_ref[...], zh_ref[...], mu_ref[...], live)
    lo = _shift_mix(lo_ref[...], loh_ref[...], mulo_ref[...], live)
    r = z[:, 0:A_WIDTH]
    k = z[:, A_WIDTH:2 * A_WIDTH]
    v = z[:, 2 * A_WIDTH:3 * A_WIDTH]
    w = -_softplus(-(w0_ref[...] + _dot(jnp.tanh(lo).astype(BF16), w2_ref[...]))) - 0.5
    lw = -jnp.exp(w)
    a = jax.nn.sigmoid(a0_ref[...] + _dot(lo.astype(BF16), a2_ref[...]))
    g = _dot(jax.nn.sigmoid(lo).astype(BF16), g2_ref[...])
    kk = k * kk_ref[...]
    bd = bd_ref[...]
    ssq = _dot_exact_rhs(kk * kk, bd)
    kap = kk / jnp.maximum(jnp.sqrt(ssq), 1e-12)
    k2 = k * (1.0 + (a - 1.0) * ka_ref[...])
    bonus = _dot_exact_rhs(r * k2 * rk_ref[...], bd) * v
    b = kap * a
    g_o[...] = g
    bonus_o[...] = bonus
    for h in range(A_HEADS):
        sl = slice(h * A_HEAD, (h + 1) * A_HEAD)
        r_o[h] = r[:, sl]
        lw_o[h] = lw[:, sl]
        k_o[h] = k2[:, sl]
        v_o[h] = v[:, sl]
        kap_o[h] = kap[:, sl]
        b_o[h] = b[:, sl]


def rwkv_prep(z_rkv, z_lo, mu_rkv, mu_lo, w0, w2p, a0, a2p, g2p, k_k, k_a, r_k, bd, *, tm):
    s = z_rkv.shape[0]
    hb = tm // HALO
    row = lambda i: (i, 0)
    halo = lambda i: (jnp.maximum(i * hb - 1, 0), 0)
    const = lambda i: (0, 0)
    hm = jax.ShapeDtypeStruct((A_HEADS, s, A_HEAD), F32)
    hm_spec = pl.BlockSpec((A_HEADS, tm, A_HEAD), lambda i: (0, i, 0))
    full = jax.ShapeDtypeStruct((s, A_WIDTH), F32)
    vec = lambda a: a.reshape(1, -1)
    return pl.pallas_call(
        _rwkv_prep_kernel,
        grid=(s // tm,),
        in_specs=[pl.BlockSpec((tm, 3 * A_WIDTH), row), pl.BlockSpec((HALO, 3 * A_WIDTH), halo),
                  pl.BlockSpec((tm, A_LORA_PAD), row), pl.BlockSpec((HALO, A_LORA_PAD), halo),
                  pl.BlockSpec((1, 3 * A_WIDTH), const), pl.BlockSpec((1, A_LORA_PAD), const),
                  pl.BlockSpec((1, A_WIDTH), const), pl.BlockSpec((A_LORA_PAD, A_WIDTH), const),
                  pl.BlockSpec((1, A_WIDTH), const), pl.BlockSpec((A_LORA_PAD, A_WIDTH), const),
                  pl.BlockSpec((A_LORA_PAD, A_WIDTH), const),
                  pl.BlockSpec((1, A_WIDTH), const), pl.BlockSpec((1, A_WIDTH), const),
                  pl.BlockSpec((1, A_WIDTH), const), pl.BlockSpec((A_WIDTH, A_WIDTH), const)],
        out_specs=[hm_spec] * 6 + [pl.BlockSpec((tm, A_WIDTH), row)] * 2,
        out_shape=[hm] * 6 + [full] * 2,
        compiler_params=_cparams("parallel"),
    )(z_rkv, z_rkv, z_lo, z_lo, vec(mu_rkv), vec(mu_lo), vec(w0), w2p, vec(a0), a2p, g2p,
      vec(k_k), vec(k_a), vec(r_k), bd)


_BNN = (((2,), (1,)), ((0,), (0,)))
_BNT = (((2,), (2,)), ((0,), (0,)))
_BTN = (((1,), (1,)), ((0,), (0,)))


def _tri_inverse(a, row, col):
    mm1 = lambda p, q: _dot(p.astype(BF16), q.astype(BF16), _BNN)
    mm3 = lambda p, q: _dot3(p, q, _BNN)
    eye = (row == col).astype(F32)
    ad = jnp.where((row >> 3) == (col >> 3), a, 0.0)
    t = eye - ad
    a2 = mm1(ad, ad)
    t = t + mm1(t, a2)
    a4 = mm1(a2, a2)
    t = t + mm1(t, a4)
    for sh, mm in ((3, mm1), (4, mm3), (5, mm3)):
        inner = (row >> sh) == (col >> sh)
        outer = (row >> (sh + 1)) == (col >> (sh + 1))
        aoff = jnp.where(jnp.logical_and(outer, jnp.logical_not(inner)), a, 0.0)
        t = t - mm(mm(t, aoff), t)
    return t


def _rwkv_chunk_kernel(r_ref, lw_ref, k_ref, v_ref, kap_ref, b_ref, y_ref, state_ref, *, nch):
    c = CHUNK
    nb = A_HEADS * nch

    @pl.when(pl.program_id(0) == 0)
    def _():
        state_ref[...] = jnp.zeros_like(state_ref)

    row = lax.broadcasted_iota(jnp.int32, (1, c, c), 1)
    col = lax.broadcasted_iota(jnp.int32, (1, c, c), 2)
    tril = row >= col
    stril = row > col
    eye = row == col
    lower_ones = jnp.broadcast_to(tril.astype(BF16), (nb, c, c))

    load = lambda ref: ref[...].reshape(nb, c, A_HEAD)
    r, lw, k, v, kap, b = load(r_ref), load(lw_ref), load(k_ref), load(v_ref), load(kap_ref), load(b_ref)
    hi = lw.astype(BF16)
    r1 = lw - hi.astype(F32)
    mid = r1.astype(BF16)
    lo = (r1 - mid.astype(F32)).astype(BF16)
    ci = _dot(lower_ones, hi, _BNN) + (_dot(lower_ones, mid, _BNN) + _dot(lower_ones, lo, _BNN))
    ce = ci - lw
    cend = ci[:, c - 1:c, :]
    gn = jnp.exp(-ci)
    gend = jnp.exp(cend - ci)
    kap_h = kap * jnp.exp(ce)
    r_h = r * jnp.exp(ci)
    b_h = b * gn
    k_h = k * gn
    b_t = b * gend
    k_t = k * gend
    p = _dot3(jnp.concatenate([kap_h, r_h], axis=1), jnp.concatenate([b_h, k_h], axis=1), _BNT)
    a_ab = jnp.where(stril, p[:, :c, :c], 0.0)
    a_ak = jnp.where(stril, p[:, :c, c:], 0.0)
    r_b = jnp.where(tril, p[:, c:, :c], 0.0)
    r_k = jnp.where(tril, p[:, c:, c:], 0.0)
    t = _tri_inverse(a_ab, row, col)
    x = _dot3(t, jnp.concatenate([kap_h, _dot3(a_ak, v, _BNN)], axis=2), _BNN)
    wr = jnp.concatenate([x[:, :, :A_HEAD], r_h], axis=1)
    rbk = jnp.concatenate([r_b, r_k], axis=2)
    btk = jnp.concatenate([b_t, k_t, jnp.where(eye, jnp.exp(cend), 0.0)], axis=1)

    pick = lambda a, ch: a.reshape(A_HEADS, nch, *a.shape[1:])[:, ch]
    m = state_ref[...]
    for ch in range(nch):
        wm = _dot3(pick(wr, ch), m, _BNN)
        u = -(wm[:, :c] + pick(x, ch)[:, :, A_HEAD:])
        uv = jnp.concatenate([u, pick(v, ch)], axis=1)
        y = wm[:, c:] + _dot3(pick(rbk, ch), uv, _BNN)
        m = _dot3(pick(btk, ch), jnp.concatenate([uv, m], axis=1), _BTN)
        ym = jnp.mean(y, axis=-1, keepdims=True)
        yc = y - ym
        yv = jnp.mean(yc * yc, axis=-1, keepdims=True)
        y_ref[:, ch * c:(ch + 1) * c, :] = yc * lax.rsqrt(yv + A_GN_EPS)
    state_ref[...] = m


RWKV_CHUNKS_PER_STEP = 2


def rwkv_chunks(r, lw, k, v, kap, b):
    _, s, _ = r.shape
    nch = RWKV_CHUNKS_PER_STEP
    spec = pl.BlockSpec((A_HEADS, nch * CHUNK, A_HEAD), lambda n: (0, n, 0))
    return pl.pallas_call(
        functools.partial(_rwkv_chunk_kernel, nch=nch),
        grid=(s // (nch * CHUNK),),
        in_specs=[spec] * 6,
        out_specs=spec,
        out_shape=jax.ShapeDtypeStruct((A_HEADS, s, A_HEAD), F32),
        scratch_shapes=[pltpu.VMEM((A_HEADS, A_HEAD, A_HEAD), F32)],
        compiler_params=_cparams("arbitrary"),
    )(r, lw, k, v, kap, b)


def _rwkv_post_kernel(y_ref, bonus_ref, g_ref, gg_ref, gb_ref, o_ref):
    y = jnp.concatenate([y_ref[h] for h in range(A_HEADS)], axis=1)
    o_ref[...] = ((y * gg_ref[...] + gb_ref[...] + bonus_ref[...]) * g_ref[...]).astype(o_ref.dtype)


def rwkv_post(y, bonus, g, gn_g, gn_b, *, tm):
    _, s, _ = y.shape
    row = lambda i: (i, 0)
    const = lambda i: (0, 0)
    return pl.pallas_call(
        _rwkv_post_kernel,
        grid=(s // tm,),
        in_specs=[pl.BlockSpec((A_HEADS, tm, A_HEAD), lambda i: (0, i, 0)),
                  pl.BlockSpec((tm, A_WIDTH), row), pl.BlockSpec((tm, A_WIDTH), row),
                  pl.BlockSpec((1, A_WIDTH), const), pl.BlockSpec((1, A_WIDTH), const)],
        out_specs=pl.BlockSpec((tm, A_WIDTH), row),
        out_shape=jax.ShapeDtypeStruct((s, A_WIDTH), BF16),
        compiler_params=_cparams("parallel"),
    )(y, bonus, g, gn_g.reshape(1, -1), gn_b.reshape(1, -1))


def _rglru_kernel(xb_ref, gate_ref, halo_ref, cw_ref, cb_ref, wr_ref, br_ref, wi_ref, bi_ref, lam_ref,
                  o_ref, xe_ref, a_ref, u_ref, h_ref, carry_ref, *, tm):
    i = pl.program_id(0)

    @pl.when(i == 0)
    def _():
        carry_ref[...] = jnp.zeros_like(carry_ref)

    xe_ref[0:HALO, :] = halo_ref[...] * (i > 0).astype(F32)
    xe_ref[HALO:, :] = xb_ref[...]
    xc = cb_ref[...] + cw_ref[B_CONV - 1:B_CONV, :] * xe_ref[HALO:, :]
    for d in range(1, B_CONV):
        xc = xc + cw_ref[B_CONV - 1 - d:B_CONV - d, :] * xe_ref[HALO - d:HALO - d + tm, :]
    xcb = xc.astype(BF16)
    r = jax.nn.sigmoid(_dot(xcb, wr_ref[...]) + br_ref[...])
    gi = jax.nn.sigmoid(_dot(xcb, wi_ref[...]) + bi_ref[...])
    log_a = -B_C * r * _softplus(-lam_ref[...])
    a = jnp.exp(log_a)
    a_ref[...] = a
    u_ref[...] = jnp.sqrt(-jnp.tanh(log_a) * (a * a + 1.0)) * (gi * xc)

    def group(gidx, h):
        base = pl.multiple_of(gidx * HALO, HALO)
        a8 = a_ref[pl.ds(base, HALO), :]
        u8 = u_ref[pl.ds(base, HALO), :]
        rows = []
        for rr in range(HALO):
            h = a8[rr:rr + 1, :] * h + u8[rr:rr + 1, :]
            rows.append(h)
        h_ref[pl.ds(base, HALO), :] = jnp.concatenate(rows, axis=0)
        return h

    carry_ref[...] = lax.fori_loop(0, tm // HALO, group, carry_ref[...])
    o_ref[...] = (jax.nn.gelu(gate_ref[...]) * h_ref[...]).astype(o_ref.dtype)


def rglru(z_b, conv_w, conv_b, wr, b_r, wi, b_i, lam, *, tm):
    s = z_b.shape[0]
    hb = tm // HALO
    row = lambda i: (i, 0)
    const = lambda i: (0, 0)
    vec = lambda a: a.reshape(1, -1)
    return pl.pallas_call(
        functools.partial(_rglru_kernel, tm=tm),
        grid=(s // tm,),
        in_specs=[pl.BlockSpec((tm, B_WIDTH), row), pl.BlockSpec((tm, B_WIDTH), lambda i: (i, 1)),
                  pl.BlockSpec((HALO, B_WIDTH), lambda i: (jnp.maximum(i * hb - 1, 0), 0)),
                  pl.BlockSpec((B_CONV, B_WIDTH), const), pl.BlockSpec((1, B_WIDTH), const),
                  pl.BlockSpec((B_WIDTH, B_WIDTH), const), pl.BlockSpec((1, B_WIDTH), const),
                  pl.BlockSpec((B_WIDTH, B_WIDTH), const), pl.BlockSpec((1, B_WIDTH), const),
                  pl.BlockSpec((1, B_WIDTH), const)],
        out_specs=pl.BlockSpec((tm, B_WIDTH), row),
        out_shape=jax.ShapeDtypeStruct((s, B_WIDTH), BF16),
        scratch_shapes=[pltpu.VMEM((tm + HALO, B_WIDTH), F32), pltpu.VMEM((tm, B_WIDTH), F32),
                        pltpu.VMEM((tm, B_WIDTH), F32), pltpu.VMEM((tm, B_WIDTH), F32),
                        pltpu.VMEM((1, B_WIDTH), F32)],
        compiler_params=_cparams("arbitrary"),
    )(z_b, z_b, z_b, conv_w, vec(conv_b), wr, vec(b_r), wi, vec(b_i), vec(lam))


ODD_PAD = 1024
ODD_KIDX_AT = 768
ODD_WIDX_AT = 896
QB = 256
SCORE_SCALE = (IDX_HEADS ** -0.5) * (IDX_DIM ** -0.5)
INT_MIN = -(2 ** 31)
CHUNK_SHIFT = 6
CUT_BITS = 14
HEAD_GROUP = 2


def _dsa_in_kernel(x_ref, w_ref, qn_ref, kvn_ref, kg_ref, kb_ref, cq_o, ckv_o, kidx_o, widx_o):
    acc = _dot(x_ref[...], w_ref[...])
    cq = acc[:, 0:C_Q_RANK]
    ckv = acc[:, C_Q_RANK:C_Q_RANK + C_KV_RANK]
    kidx = acc[:, ODD_KIDX_AT:ODD_KIDX_AT + IDX_DIM]
    widx = acc[:, ODD_WIDX_AT:ODD_WIDX_AT + IDX_HEADS]
    rms = lambda t, g: t * lax.rsqrt(jnp.mean(t * t, axis=-1, keepdims=True) + 1e-6) * g
    cq_o[...] = rms(cq, qn_ref[...]).astype(BF16)
    ckv_o[...] = rms(ckv, kvn_ref[...]).astype(BF16)
    kidx_o[...] = _layer_norm_rows(kidx, kg_ref[...], kb_ref[...], LN_EPS).astype(BF16)
    widx_o[...] = widx * SCORE_SCALE


def dsa_in(xb, w_pad, q_norm, kv_norm, kidx_g, kidx_b, *, tm):
    s, d = xb.shape
    row = lambda i: (i, 0)
    const = lambda i: (0, 0)
    vec = lambda a: a.reshape(1, -1)
    return pl.pallas_call(
        _dsa_in_kernel,
        grid=(s // tm,),
        in_specs=[pl.BlockSpec((tm, d), row), pl.BlockSpec((d, ODD_PAD), const),
                  pl.BlockSpec((1, C_Q_RANK), const), pl.BlockSpec((1, C_KV_RANK), const),
                  pl.BlockSpec((1, IDX_DIM), const), pl.BlockSpec((1, IDX_DIM), const)],
        out_specs=[pl.BlockSpec((tm, C_Q_RANK), row), pl.BlockSpec((tm, C_KV_RANK), row),
                   pl.BlockSpec((tm, IDX_DIM), row), pl.BlockSpec((tm, IDX_HEADS), row)],
        out_shape=[jax.ShapeDtypeStruct((s, C_Q_RANK), BF16), jax.ShapeDtypeStruct((s, C_KV_RANK), BF16),
                   jax.ShapeDtypeStruct((s, IDX_DIM), BF16), jax.ShapeDtypeStruct((s, IDX_HEADS), F32)],
        compiler_params=_cparams("parallel"),
    )(xb, w_pad, vec(q_norm), vec(kv_norm), vec(kidx_g), vec(kidx_b))


def _qabs_kernel(cq_ref, wuq_ref, wuk_ref, o_ref):
    q = _dot(cq_ref[...], wuq_ref[...]).astype(BF16)
    qa = _dot(q, wuk_ref[...], _NT)
    o_ref[...] = (qa * (C_HEAD_DIM ** -0.5 * LOG2E)).astype(BF16)


def dsa_qabs(cq, w_uq, w_uk, *, tm):
    s = cq.shape[0]
    return pl.pallas_call(
        _qabs_kernel,
        grid=(s // tm, C_HEADS),
        in_specs=[pl.BlockSpec((tm, C_Q_RANK), lambda i, h: (i, 0)),
                  pl.BlockSpec((C_Q_RANK, C_HEAD_DIM), lambda i, h: (0, h)),
                  pl.BlockSpec((None, C_KV_RANK, C_HEAD_DIM), lambda i, h: (h, 0, 0))],
        out_specs=pl.BlockSpec((None, tm, C_KV_RANK), lambda i, h: (h, i, 0)),
        out_shape=jax.ShapeDtypeStruct((C_HEADS, s, C_KV_RANK), BF16),
        compiler_params=_cparams("parallel", "parallel"),
    )(cq, w_uq, w_uk)


def _sortable_key(score):
    bits = lax.bitcast_convert_type(score + 0.0, jnp.int32)
    return jnp.where(bits < 0, bits ^ jnp.int32(0x7FFFFFFF), bits)


def _dsa_attn_kernel(qabs_ref, qidx_ref, widx_t_ref, kidx_ref, ckv_ref, wuv_ref, bias_ref,
                     o_ref, keys_ref, acc_ref, m_ref, l_ref, *, k_sel):
    i = pl.program_id(0)
    n_tiles = i + 1
    k_local = lax.broadcasted_iota(jnp.int32, (QB, QB), 0)
    q_local = lax.broadcasted_iota(jnp.int32, (QB, QB), 1)
    allowed_diag_t = (k_local >> CHUNK_SHIFT) <= (q_local >> CHUNK_SHIFT)

    widx_t = widx_t_ref[...]

    def score_tile(t, carry):
        off = pl.multiple_of(t * QB, QB)
        kt = kidx_ref[pl.ds(off, QB), :]
        sc = jnp.zeros((QB, QB), F32)
        for j in range(IDX_HEADS):
            d = _dot(kt, qidx_ref[:, j * IDX_DIM:(j + 1) * IDX_DIM], _NT)
            sc = sc + widx_t[j:j + 1, :] * jnp.maximum(d, 0.0)
        key = _sortable_key(sc)
        key = jnp.where(jnp.logical_or(t < i, allowed_diag_t), key, jnp.int32(INT_MIN))
        keys_ref[pl.ds(off, QB), :] = key
        return carry

    lax.fori_loop(0, n_tiles, score_tile, 0)

    def selected(kt, off, thr, cut):
        pos = lax.broadcasted_iota(jnp.int32, kt.shape, 0) + off
        return jnp.logical_or(kt > thr, jnp.logical_and(kt == thr, pos < cut))

    def count(pred):
        def body(t, acc):
            off = pl.multiple_of(t * QB, QB)
            hit = pred(keys_ref[pl.ds(off, QB), :], off).astype(jnp.int32)
            return acc + jnp.sum(hit.reshape(QB // HALO, HALO, QB), axis=0)
        acc = lax.fori_loop(0, n_tiles, body, jnp.zeros((HALO, QB), jnp.int32))
        return jnp.sum(acc.astype(F32), axis=0, keepdims=True).astype(jnp.int32)

    def thr_bit(it, thr):
        cand = thr + jnp.left_shift(jnp.int32(1), 31 - it)
        cnt = count(lambda kt, off: kt >= cand)
        return jnp.where(cnt >= k_sel, cand, thr)

    thr = lax.fori_loop(0, 32, thr_bit, jnp.full((1, QB), INT_MIN, jnp.int32))

    def cut_bit(it, cut):
        cand = cut + jnp.left_shift(jnp.int32(1), CUT_BITS - 1 - it)
        cnt = count(lambda kt, off: selected(kt, off, thr, cand))
        return jnp.where(cnt <= k_sel, cand, cut)

    n_ge = count(lambda kt, off: kt >= thr)
    tied = jnp.logical_and(n_ge > k_sel, thr > INT_MIN)
    any_tied = jnp.max(jnp.where(tied, 1.0, 0.0)) > 0.0
    cut = lax.cond(any_tied,
                   lambda: lax.fori_loop(0, CUT_BITS, cut_bit, jnp.zeros((1, QB), jnp.int32)),
                   lambda: jnp.full((1, QB), 2 ** CUT_BITS - 1, jnp.int32))
    rep_lanes = lambda a, n: jnp.concatenate([a] * (n // LANES), axis=-1)

    m_ref[...] = jnp.full(m_ref.shape, NEG_BIG, F32)
    l_ref[...] = jnp.zeros(l_ref.shape, F32)
    acc_ref[...] = jnp.zeros(acc_ref.shape, F32)

    def attend(off, width, near):
        sel = selected(keys_ref[pl.ds(off, width), :], off, thr, cut)
        if near == 1:
            sel = jnp.logical_and(sel, allowed_diag_t)
        mask_add = jnp.where(sel, 0.0, NEG_BIG).T[None]
        kv = ckv_ref[pl.ds(off, width), :]
        rep = lambda a, n: jnp.concatenate([a] * (n // LANES), axis=-1)

        for g in range(C_HEADS // HEAD_GROUP):
            hs = slice(g * HEAD_GROUP, (g + 1) * HEAD_GROUP)
            q = qabs_ref[hs].reshape(HEAD_GROUP * QB, C_KV_RANK)
            s = _dot(q, kv, _NT).reshape(HEAD_GROUP, QB, width) + mask_add
            if near is not None:
                s = s + bias_ref[near, hs]
            m_old = m_ref[hs]
            row_max = jnp.broadcast_to(jnp.max(s, axis=-1, keepdims=True), m_old.shape)
            m_new = jnp.maximum(m_old, row_max)
            alpha = jnp.exp2(m_old - m_new)
            p = jnp.exp2(s - rep(m_new, width))
            row_sum = jnp.broadcast_to(jnp.sum(p, axis=-1, keepdims=True), m_old.shape)
            l_ref[hs] = alpha * l_ref[hs] + row_sum
            pv = _dot(p.astype(BF16).reshape(HEAD_GROUP * QB, width), kv).reshape(HEAD_GROUP, QB, C_KV_RANK)
            acc_ref[hs] = rep(alpha, C_KV_RANK) * acc_ref[hs] + pv
            m_ref[hs] = m_new

    n_far = jnp.maximum(i - 1, 0)

    def far_pair(t2, carry):
        attend(pl.multiple_of(t2 * (2 * QB), 2 * QB), 2 * QB, None)
        return carry

    lax.fori_loop(0, n_far // 2, far_pair, 0)

    @pl.when(n_far % 2 == 1)
    def _():
        attend(pl.multiple_of((n_far - 1) * QB, QB), QB, None)

    @pl.when(i >= 1)
    def _():
        attend(pl.multiple_of((i - 1) * QB, QB), QB, 0)

    attend(pl.multiple_of(i * QB, QB), QB, 1)

    for h in range(C_HEADS):
        o_lat = (acc_ref[h] / rep_lanes(l_ref[h], C_KV_RANK)).astype(BF16)
        o_ref[:, h * C_HEAD_DIM:(h + 1) * C_HEAD_DIM] = _dot(o_lat, wuv_ref[h]).astype(o_ref.dtype)


def dsa_attention(qabs, qidx, widx_t, kidx, ckv, w_uv, bias_near, *, k_sel):
    _, s, _ = qabs.shape
    full2 = lambda i: (0, 0)
    once = pl.Buffered(1)
    return pl.pallas_call(
        functools.partial(_dsa_attn_kernel, k_sel=k_sel),
        grid=(s // QB,),
        in_specs=[pl.BlockSpec((C_HEADS, QB, C_KV_RANK), lambda i: (0, i, 0)),
                  pl.BlockSpec((QB, IDX_HEADS * IDX_DIM), lambda i: (i, 0)),
                  pl.BlockSpec((IDX_HEADS, QB), lambda i: (0, i)),
                  pl.BlockSpec((s, IDX_DIM), full2, pipeline_mode=once),
                  pl.BlockSpec((s, C_KV_RANK), full2, pipeline_mode=once),
                  pl.BlockSpec((C_HEADS, C_KV_RANK, C_HEAD_DIM), lambda i: (0, 0, 0), pipeline_mode=once),
                  pl.BlockSpec((2, C_HEADS, QB, QB), lambda i: (0, 0, 0, 0), pipeline_mode=once)],
        out_specs=pl.BlockSpec((QB, C_HEADS * C_HEAD_DIM), lambda i: (i, 0)),
        out_shape=jax.ShapeDtypeStruct((s, C_HEADS * C_HEAD_DIM), BF16),
        scratch_shapes=[pltpu.VMEM((s, QB), jnp.int32),
                        pltpu.VMEM((C_HEADS, QB, C_KV_RANK), F32),
                        pltpu.VMEM((C_HEADS, QB, LANES), F32),
                        pltpu.VMEM((C_HEADS, QB, LANES), F32)],
        compiler_params=_cparams("parallel"),
    )(qabs, qidx, widx_t, kidx, ckv, w_uv, bias_near)


def _t5_bucket(rel):
    nb = REL_BUCKETS // 2
    max_exact = nb // 2
    ret = jnp.where(rel > 0, nb, 0)
    n = jnp.abs(rel)
    nf = jnp.maximum(n, 1).astype(jnp.float32)
    large = max_exact + (jnp.log(nf / max_exact) / math.log(REL_MAX_DIST / max_exact) * (nb - max_exact)).astype(jnp.int32)
    large = jnp.minimum(large, nb - 1)
    return ret + jnp.where(n < max_exact, n, large)


def _near_bias(rel_bias):
    ql = jnp.arange(QB)[:, None]
    sl = jnp.arange(QB)[None, :]
    rel = jnp.stack([sl - ql - QB, sl - ql])
    far = rel_bias[_t5_bucket(jnp.array(-2 * QB))]
    table = ((rel_bias - far) * LOG2E).astype(F32)
    onehot = jax.nn.one_hot(_t5_bucket(rel), REL_BUCKETS, dtype=F32)
    return jnp.einsum('tqsb,bh->thqs', onehot, table, precision=lax.Precision.HIGHEST)


def _pad_rows(w, at, total):
    return jnp.zeros((total, w.shape[1]), w.dtype).at[at:at + w.shape[0]].set(w)


def _block_diag(w):
    n, d, e = w.shape
    eye = jnp.eye(n, dtype=w.dtype)
    return (eye[:, None, :, None] * w[:, :, None, :]).reshape(n * d, n * e)


def _even_mixer(x, xb, w_in, w_out, j, mu, w0, w2, a0, a2, g2, k_k, k_a, r_k, gn_g, gn_b,
                conv_w, conv_b, w_r, b_r, w_i, b_i, lam, ln_g, ln_b, tm):
    w_in = w_in.astype(BF16)
    n_rkv = 3 * A_WIDTH
    z_rkv = matmul(xb, w_in[:, :n_rkv], tm=tm, tn=1024)
    w_lo = jnp.pad(w_in[:, n_rkv:A_COLS], ((0, 0), (0, A_LORA_PAD - A_LORA)))
    z_lo = matmul(xb, w_lo, tm=tm, tn=A_LORA_PAD)
    z_b = matmul(xb, w_in[:, A_COLS:], tm=tm, tn=1024)

    mu_lo = jnp.pad(mu[n_rkv:], (0, A_LORA_PAD - A_LORA))
    w2p = _pad_rows(w2, 0, A_LORA_PAD).astype(BF16)
    a2p = _pad_rows(a2, A_DECAY_LORA, A_LORA_PAD).astype(BF16)
    g2p = _pad_rows(g2, A_DECAY_LORA + A_ICL_LORA, A_LORA_PAD).astype(BF16)
    bd = _block_diag(jnp.ones((A_HEADS, A_HEAD, A_HEAD), BF16))
    r, lw, k2, v, kap, b, g, bonus = rwkv_prep(
        z_rkv, z_lo, mu[:n_rkv], mu_lo, w0, w2p, a0, a2p, g2p, k_k, k_a, r_k.reshape(-1), bd,
        tm=min(tm, 256))
    y = rwkv_chunks(r, lw, k2, v, kap, b)
    y_a = rwkv_post(y, bonus, g, gn_g, gn_b, tm=tm)

    y_b = rglru(z_b, conv_w, conv_b, _block_diag(w_r).astype(BF16), b_r,
                _block_diag(w_i).astype(BF16), b_i, lam, tm=min(tm, 256))
    return matmul2_residual_ln(y_a, y_b, w_out, j, x, ln_g, ln_b, tm=tm)


def _odd_mixer(x, xb, w_in, w_out, j, q_norm, kv_norm, w_uq, w_uk, w_uv, w_qidx, kidx_g, kidx_b,
               bias_near, ln_g, ln_b, tm):
    s = x.shape[0]
    d = w_in.shape[0]
    n_qkv = C_Q_RANK + C_KV_RANK
    w_pad = jnp.zeros((d, ODD_PAD), BF16)
    w_pad = w_pad.at[:, :n_qkv].set(w_in[:, :n_qkv].astype(BF16))
    w_pad = w_pad.at[:, ODD_KIDX_AT:ODD_KIDX_AT + IDX_DIM].set(w_in[:, n_qkv:n_qkv + IDX_DIM].astype(BF16))
    w_pad = w_pad.at[:, ODD_WIDX_AT:ODD_WIDX_AT + IDX_HEADS].set(w_in[:, n_qkv + IDX_DIM:].astype(BF16))
    cq, ckv, kidx, widx = dsa_in(xb, w_pad, q_norm, kv_norm, kidx_g, kidx_b, tm=tm)
    qabs = dsa_qabs(cq, w_uq.astype(BF16), w_uk.astype(BF16), tm=tm)
    qidx = matmul(cq, w_qidx.astype(BF16), tm=tm, tn=IDX_HEADS * IDX_DIM, out_dtype=BF16)
    o = dsa_attention(qabs, qidx, widx.T, kidx, ckv, w_uv.astype(BF16), bias_near,
                      k_sel=min(TOPK_MAX, s // 4))
    return matmul_residual_ln(o, w_out, j, x, ln_g, ln_b, tm=tm, tk=o.shape[1])


def kernel(x, p, rel_bias, ln1_g, ln1_b, ln2_g, ln2_b, ffn_w_up, ffn_conv_w, ffn_conv_b, ffn_w_down, ple_w_proj, ple_w_gate, ev_w_in, ev_w_out, a_mu, a_w0, a_w2, a_a0, a_a2, a_g2, a_k_k, a_k_a, a_r_k, a_gn_g, a_gn_b, b_conv_w, b_conv_b, b_w_r, b_b_r, b_w_i, b_b_i, b_lambda, od_w_in, od_w_out, c_q_norm, c_kv_norm, c_w_uq, c_w_uk, c_w_uv, c_w_qidx, c_kidx_g, c_kidx_b):
    bsz, s, d = x.shape
    assert bsz == 1 and s % QB == 0 and s <= 2 ** CUT_BITS
    tm = min(512, s)
    x = x[0]
    xb = x.astype(BF16)
    bias_near = _near_bias(rel_bias)
    ev_w_out_b, od_w_out_b = ev_w_out.astype(BF16), od_w_out.astype(BF16)
    w_up_b, w_down_b = ffn_w_up.astype(BF16), ffn_w_down.astype(BF16)
    w_gate_b, w_proj_b, p_b = ple_w_gate.astype(BF16), ple_w_proj.astype(BF16), p.astype(BF16)
    for layer in range(DEPTH):
        j = layer // 2
        if layer % 2 == 0:
            x, xb = _even_mixer(x, xb, ev_w_in[j], ev_w_out_b, j, a_mu[j], a_w0[j], a_w2[j], a_a0[j], a_a2[j],
                                a_g2[j], a_k_k[j], a_k_a[j], a_r_k[j], a_gn_g[j], a_gn_b[j],
                                b_conv_w[j], b_conv_b[j], b_w_r[j], b_b_r[j], b_w_i[j], b_b_i[j], b_lambda[j],
                                ln1_g[layer], ln1_b[layer], tm)
        else:
            x, xb = _odd_mixer(x, xb, od_w_in[j], od_w_out_b, j, c_q_norm[j], c_kv_norm[j], c_w_uq[j], c_w_uk[j],
                               c_w_uv[j], c_w_qidx[j], c_kidx_g[j], c_kidx_b[j], bias_near,
                               ln1_g[layer], ln1_b[layer], tm)
        hm = ffn_up(xb, w_up_b, layer, ffn_conv_w[layer], ffn_conv_b[layer], tm=min(4 * tm, s), sub=tm, tc=512)
        x, xb = matmul_residual_ln(hm, w_down_b, layer, x, ln2_g[layer], ln2_b[layer], tm=tm, tk=D_FF // 4)
        x, xb = ple_update(xb, w_gate_b, p_b, w_proj_b, layer, x, tm=tm, tn=1024)
    return x[None]
```

```python
import functools
import math

import jax
import jax.numpy as jnp
from jax import lax
from jax.experimental import pallas as pl
from jax.experimental.pallas import tpu as pltpu

F32 = jnp.float32
BF16 = jnp.bfloat16

D_MODEL = 2048
DEPTH = 4
CHUNK = 64
DN_ALPHA = (2 * DEPTH) ** 0.25
LN_EPS = 1e-5
A_WIDTH = 1024
A_HEAD = 64
A_HEADS = 16
A_DECAY_LORA = 64
A_ICL_LORA = 64
A_GATE_LORA = 160
A_LORA = A_DECAY_LORA + A_ICL_LORA + A_GATE_LORA
A_LORA_PAD = 384
A_COLS = 3 * A_WIDTH + A_LORA
A_GN_EPS = 64e-5
B_WIDTH = 1024
B_BLOCKS = 16
B_BLOCK = 64
B_CONV = 4
B_C = 8.0
C_HEADS = 16
C_HEAD_DIM = 128
C_Q_RANK = 512
C_KV_RANK = 256
IDX_HEADS = 16
IDX_DIM = 64
TOPK_MAX = 256
REL_BUCKETS = 32
REL_MAX_DIST = 128
D_FF = 5632
FFN_CONV = 3
PLE_DIM = 256

VMEM_LIMIT_BYTES = 56 * 1024 * 1024
NEG_BIG = -1e30
HALO = 8
LANES = 128
LOG2E = math.log2(math.e)


def _cparams(*sem):
    return pltpu.CompilerParams(dimension_semantics=sem, vmem_limit_bytes=VMEM_LIMIT_BYTES)


def _split_bf16(a):
    hi = a.astype(BF16)
    lo = (a - hi.astype(F32)).astype(BF16)
    return hi, lo


def _dot(a, b, dims=(((1,), (0,)), ((), ()))):
    return lax.dot_general(a, b, dims, preferred_element_type=F32)


_NT = (((1,), (1,)), ((), ()))
_TN = (((0,), (0,)), ((), ()))
_NN = (((1,), (0,)), ((), ()))


def _dot3(a, b, dims=_NN):
    ah, al = _split_bf16(a)
    bh, bl = _split_bf16(b)
    return _dot(ah, bh, dims) + (_dot(ah, bl, dims) + _dot(al, bh, dims))


def _dot_exact_rhs(a, b_bf16, dims=_NN):
    hi = a.astype(BF16)
    r1 = a - hi.astype(F32)
    mid = r1.astype(BF16)
    lo = (r1 - mid.astype(F32)).astype(BF16)
    return _dot(hi, b_bf16, dims) + (_dot(mid, b_bf16, dims) + _dot(lo, b_bf16, dims))


def _layer_norm_rows(v, g, b, eps):
    mu = jnp.mean(v, axis=-1, keepdims=True)
    d = v - mu
    var = jnp.mean(d * d, axis=-1, keepdims=True)
    return d * lax.rsqrt(var + eps) * g + b


def _softplus(x):
    return jnp.maximum(x, 0.0) + jnp.log1p(jnp.exp(-jnp.abs(x)))


def _mm_kernel(a_ref, b_ref, o_ref):
    o_ref[...] = _dot(a_ref[...], b_ref[...]).astype(o_ref.dtype)


def matmul(a, b, *, tm, tn, out_dtype=F32):
    m, k = a.shape
    _, n = b.shape
    assert m % tm == 0 and n % tn == 0
    return pl.pallas_call(
        _mm_kernel,
        grid=(n // tn, m // tm),
        in_specs=[pl.BlockSpec((tm, k), lambda j, i: (i, 0)),
                  pl.BlockSpec((k, tn), lambda j, i: (0, j))],
        out_specs=pl.BlockSpec((tm, tn), lambda j, i: (i, j)),
        out_shape=jax.ShapeDtypeStruct((m, n), out_dtype),
        compiler_params=_cparams("parallel", "parallel"),
    )(a, b)


def _mm_ln_kernel(a_ref, b_ref, x_ref, g_ref, beta_ref, o_ref, ob_ref, acc_ref, *, nk):
    kk = pl.program_id(1)

    @pl.when(kk == 0)
    def _():
        acc_ref[...] = jnp.zeros_like(acc_ref)

    acc_ref[...] += _dot(a_ref[...], b_ref[...])

    @pl.when(kk == nk - 1)
    def _():
        v = DN_ALPHA * x_ref[...] + acc_ref[...]
        y = _layer_norm_rows(v, g_ref[...], beta_ref[...], LN_EPS)
        o_ref[...] = y
        ob_ref[...] = y.astype(BF16)


def matmul_residual_ln(a, b, layer, x, g, beta, *, tm, tk):
    m, k = a.shape
    n = b.shape[2]
    assert m % tm == 0 and k % tk == 0
    nk = k // tk
    return pl.pallas_call(
        functools.partial(_mm_ln_kernel, nk=nk),
        grid=(m // tm, nk),
        in_specs=[pl.BlockSpec((tm, tk), lambda i, kk: (i, kk)),
                  pl.BlockSpec((None, tk, n), lambda i, kk: (layer, kk, 0)),
                  pl.BlockSpec((tm, n), lambda i, kk: (i, 0)),
                  pl.BlockSpec((1, n), lambda i, kk: (0, 0)),
                  pl.BlockSpec((1, n), lambda i, kk: (0, 0))],
        out_specs=[pl.BlockSpec((tm, n), lambda i, kk: (i, 0)),
                   pl.BlockSpec((tm, n), lambda i, kk: (i, 0))],
        out_shape=[jax.ShapeDtypeStruct((m, n), F32), jax.ShapeDtypeStruct((m, n), BF16)],
        scratch_shapes=[pltpu.VMEM((tm, n), F32)],
        compiler_params=_cparams("parallel", "arbitrary"),
    )(a, b, x, g.reshape(1, n), beta.reshape(1, n))


def _mm2_ln_kernel(a1_ref, a2_ref, b1_ref, b2_ref, x_ref, g_ref, beta_ref, o_ref, ob_ref):
    acc = _dot(a1_ref[...], b1_ref[...]) + _dot(a2_ref[...], b2_ref[...])
    y = _layer_norm_rows(DN_ALPHA * x_ref[...] + acc, g_ref[...], beta_ref[...], LN_EPS)
    o_ref[...] = y
    ob_ref[...] = y.astype(BF16)


def matmul2_residual_ln(a1, a2, b, layer, x, g, beta, *, tm):
    m, k1 = a1.shape
    k2 = a2.shape[1]
    n = b.shape[2]
    assert k1 == k2
    row = lambda i: (i, 0)
    const = lambda i: (0, 0)
    return pl.pallas_call(
        _mm2_ln_kernel,
        grid=(m // tm,),
        in_specs=[pl.BlockSpec((tm, k1), row), pl.BlockSpec((tm, k2), row),
                  pl.BlockSpec((None, k1, n), lambda i: (layer, 0, 0)),
                  pl.BlockSpec((None, k2, n), lambda i: (layer, 1, 0)),
                  pl.BlockSpec((tm, n), row), pl.BlockSpec((1, n), const), pl.BlockSpec((1, n), const)],
        out_specs=[pl.BlockSpec((tm, n), row), pl.BlockSpec((tm, n), row)],
        out_shape=[jax.ShapeDtypeStruct((m, n), F32), jax.ShapeDtypeStruct((m, n), BF16)],
        compiler_params=_cparams("parallel"),
    )(a1, a2, b, b, x, g.reshape(1, n), beta.reshape(1, n))


def _ple_kernel(xb_ref, wg_ref, p_ref, wp_ref, x_ref, o_ref, ob_ref):
    gate = jax.nn.sigmoid(_dot(xb_ref[...], wg_ref[...]))
    proj = _dot(p_ref[...], wp_ref[...])
    y = x_ref[...] + gate * proj
    o_ref[...] = y
    ob_ref[...] = y.astype(BF16)


def ple_update(xb, wg, pb, wp, layer, x, *, tm, tn):
    m, d = xb.shape
    pd = pb.shape[-1]
    return pl.pallas_call(
        _ple_kernel,
        grid=(d // tn, m // tm),
        in_specs=[pl.BlockSpec((tm, d), lambda j, i: (i, 0)),
                  pl.BlockSpec((None, d, tn), lambda j, i: (layer, 0, j)),
                  pl.BlockSpec((None, None, tm, pd), lambda j, i: (layer, 0, i, 0)),
                  pl.BlockSpec((None, pd, tn), lambda j, i: (layer, 0, j)),
                  pl.BlockSpec((tm, tn), lambda j, i: (i, j))],
        out_specs=[pl.BlockSpec((tm, tn), lambda j, i: (i, j)),
                   pl.BlockSpec((tm, tn), lambda j, i: (i, j))],
        out_shape=[jax.ShapeDtypeStruct((m, d), F32), jax.ShapeDtypeStruct((m, d), BF16)],
        compiler_params=_cparams("parallel", "parallel"),
    )(xb, wg, pb, wp, x)


def _ffn_up_kernel(x_ref, wg_ref, wu_ref, cwg_ref, cwu_ref, cbg_ref, cbu_ref, o_ref, eg_ref, eu_ref, *, tm, sub):
    i = pl.program_id(1)

    @pl.when(i == 0)
    def _():
        eg_ref[0:HALO, :] = jnp.zeros((HALO, eg_ref.shape[1]), F32)
        eu_ref[0:HALO, :] = jnp.zeros((HALO, eu_ref.shape[1]), F32)

    @pl.when(i > 0)
    def _():
        eg_ref[0:HALO, :] = eg_ref[tm:tm + HALO, :]
        eu_ref[0:HALO, :] = eu_ref[tm:tm + HALO, :]

    tc = o_ref.shape[1]
    halves = ((0, tc // 2), (tc // 2, tc))

    def project(k, e_ref, w_ref):
        e_ref[HALO + k * sub:HALO + (k + 1) * sub, :] = _dot(x_ref[k * sub:(k + 1) * sub, :], w_ref[...])

    def conv(e_ref, w_ref, b_ref, base, c0, c1):
        acc = b_ref[:, c0:c1] + w_ref[FFN_CONV - 1:FFN_CONV, c0:c1] * e_ref[base:base + sub, c0:c1]
        for d in range(1, FFN_CONV):
            acc = acc + w_ref[FFN_CONV - 1 - d:FFN_CONV - d, c0:c1] * e_ref[base - d:base - d + sub, c0:c1]
        return acc

    def gate_rows(k, c0, c1):
        base = HALO + k * sub
        gate = conv(eg_ref, cwg_ref, cbg_ref, base, c0, c1)
        up = conv(eu_ref, cwu_ref, cbu_ref, base, c0, c1)
        o_ref[k * sub:(k + 1) * sub, c0:c1] = (jax.nn.gelu(gate) * up).astype(o_ref.dtype)

    project(0, eg_ref, wg_ref)
    project(0, eu_ref, wu_ref)
    for k in range(tm // sub):
        more = (k + 1) * sub < tm
        if more:
            project(k + 1, eg_ref, wg_ref)
        gate_rows(k, *halves[0])
        if more:
            project(k + 1, eu_ref, wu_ref)
        gate_rows(k, *halves[1])


def ffn_up(xb, w_up, layer, conv_w, conv_b, *, tm, sub, tc):
    s, d = xb.shape
    two_ff = w_up.shape[2]
    ff = two_ff // 2
    nc = ff // tc
    cb = conv_b.reshape(1, two_ff)
    return pl.pallas_call(
        functools.partial(_ffn_up_kernel, tm=tm, sub=sub),
        grid=(nc, s // tm),
        in_specs=[pl.BlockSpec((tm, d), lambda j, i: (i, 0)),
                  pl.BlockSpec((None, d, tc), lambda j, i: (layer, 0, j)),
                  pl.BlockSpec((None, d, tc), lambda j, i: (layer, 0, j + nc)),
                  pl.BlockSpec((FFN_CONV, tc), lambda j, i: (0, j)),
                  pl.BlockSpec((FFN_CONV, tc), lambda j, i: (0, j + nc)),
                  pl.BlockSpec((1, tc), lambda j, i: (0, j)),
                  pl.BlockSpec((1, tc), lambda j, i: (0, j + nc))],
        out_specs=pl.BlockSpec((tm, tc), lambda j, i: (i, j)),
        out_shape=jax.ShapeDtypeStruct((s, ff), BF16),
        scratch_shapes=[pltpu.VMEM((tm + HALO, tc), F32), pltpu.VMEM((tm + HALO, tc), F32)],
        compiler_params=_cparams("parallel", "arbitrary"),
    )(xb, w_up, w_up, conv_w, conv_w, cb, cb)


def _shift_mix(z, halo, mu, live):
    prev = pltpu.roll(z, 1, 0)
    row0 = lax.broadcasted_iota(jnp.int32, z.shape, 0) == 0
    prev = jnp.where(row0, halo[HALO - 1:HALO, :] * live, prev)
    return z + (prev - z) * mu


def _rwkv_prep_kernel(z_ref, zh_ref, lo_ref, loh_ref, mu_ref, mulo_ref, w0_ref, w2_ref, a0_ref, a2_ref,
                      g2_ref, kk_ref, ka_ref, rk_ref, bd_ref,
                      r_o, lw_o, k_o, v_o, kap_o, b_o, g_o, bonus_o):
    live = (pl.program_id(0) > 0).astype(F32)
    z = _shift_mix(z_ref[...], zh_ref[...], mu_ref[...], live)
    lo = _shift_mix(lo_ref[...], loh_ref[...], mulo_ref[...], live)
    r = z[:, 0:A_WIDTH]
    k = z[:, A_WIDTH:2 * A_WIDTH]
    v = z[:, 2 * A_WIDTH:3 * A_WIDTH]
    w = -_softplus(-(w0_ref[...] + _dot(jnp.tanh(lo).astype(BF16), w2_ref[...]))) - 0.5
    lw = -jnp.exp(w)
    a = jax.nn.sigmoid(a0_ref[...] + _dot(lo.astype(BF16), a2_ref[...]))
    g = _dot(jax.nn.sigmoid(lo).astype(BF16), g2_ref[...])
    kk = k * kk_ref[...]
    bd = bd_ref[...]
    ssq = _dot_exact_rhs(kk * kk, bd)
    kap = kk / jnp.maximum(jnp.sqrt(ssq), 1e-12)
    k2 = k * (1.0 + (a - 1.0) * ka_ref[...])
    bonus = _dot_exact_rhs(r * k2 * rk_ref[...], bd) * v
    b = kap * a
    g_o[...] = g
    bonus_o[...] = bonus
    for h in range(A_HEADS):
        sl = slice(h * A_HEAD, (h + 1) * A_HEAD)
        r_o[h] = r[:, sl]
        lw_o[h] = lw[:, sl]
        k_o[h] = k2[:, sl]
        v_o[h] = v[:, sl]
        kap_o[h] = kap[:, sl]
        b_o[h] = b[:, sl]


def rwkv_prep(z_rkv, z_lo, mu_rkv, mu_lo, w0, w2p, a0, a2p, g2p, k_k, k_a, r_k, bd, *, tm):
    s = z_rkv.shape[0]
    hb = tm // HALO
    row = lambda i: (i, 0)
    halo = lambda i: (jnp.maximum(i * hb - 1, 0), 0)
    const = lambda i: (0, 0)
    hm = jax.ShapeDtypeStruct((A_HEADS, s, A_HEAD), F32)
    hm_spec = pl.BlockSpec((A_HEADS, tm, A_HEAD), lambda i: (0, i, 0))
    full = jax.ShapeDtypeStruct((s, A_WIDTH), F32)
    vec = lambda a: a.reshape(1, -1)
    return pl.pallas_call(
        _rwkv_prep_kernel,
        grid=(s // tm,),
        in_specs=[pl.BlockSpec((tm, 3 * A_WIDTH), row), pl.BlockSpec((HALO, 3 * A_WIDTH), halo),
                  pl.BlockSpec((tm, A_LORA_PAD), row), pl.BlockSpec((HALO, A_LORA_PAD), halo),
                  pl.BlockSpec((1, 3 * A_WIDTH), const), pl.BlockSpec((1, A_LORA_PAD), const),
                  pl.BlockSpec((1, A_WIDTH), const), pl.BlockSpec((A_LORA_PAD, A_WIDTH), const),
                  pl.BlockSpec((1, A_WIDTH), const), pl.BlockSpec((A_LORA_PAD, A_WIDTH), const),
                  pl.BlockSpec((A_LORA_PAD, A_WIDTH), const),
                  pl.BlockSpec((1, A_WIDTH), const), pl.BlockSpec((1, A_WIDTH), const),
                  pl.BlockSpec((1, A_WIDTH), const), pl.BlockSpec((A_WIDTH, A_WIDTH), const)],
        out_specs=[hm_spec] * 6 + [pl.BlockSpec((tm, A_WIDTH), row)] * 2,
        out_shape=[hm] * 6 + [full] * 2,
        compiler_params=_cparams("parallel"),
    )(z_rkv, z_rkv, z_lo, z_lo, vec(mu_rkv), vec(mu_lo), vec(w0), w2p, vec(a0), a2p, g2p,
      vec(k_k), vec(k_a), vec(r_k), bd)


_BNN = (((2,), (1,)), ((0,), (0,)))
_BNT = (((2,), (2,)), ((0,), (0,)))
_BTN = (((1,), (1,)), ((0,), (0,)))


def _tri_inverse(a, row, col):
    mm1 = lambda p, q: _dot(p.astype(BF16), q.astype(BF16), _BNN)
    mm3 = lambda p, q: _dot3(p, q, _BNN)
    eye = (row == col).astype(F32)
    ad = jnp.where((row >> 3) == (col >> 3), a, 0.0)
    t = eye - ad
    a2 = mm1(ad, ad)
    t = t + mm1(t, a2)
    a4 = mm1(a2, a2)
    t = t + mm1(t, a4)
    for sh, mm in ((3, mm1), (4, mm3), (5, mm3)):
        inner = (row >> sh) == (col >> sh)
        outer = (row >> (sh + 1)) == (col >> (sh + 1))
        aoff = jnp.where(jnp.logical_and(outer, jnp.logical_not(inner)), a, 0.0)
        t = t - mm(mm(t, aoff), t)
    return t


def _rwkv_chunk_kernel(r_ref, lw_ref, k_ref, v_ref, kap_ref, b_ref, y_ref, state_ref, *, nch):
    c = CHUNK
    nb = A_HEADS * nch

    @pl.when(pl.program_id(0) == 0)
    def _():
        state_ref[...] = jnp.zeros_like(state_ref)

    row = lax.broadcasted_iota(jnp.int32, (1, c, c), 1)
    col = lax.broadcasted_iota(jnp.int32, (1, c, c), 2)
    tril = row >= col
    stril = row > col
    eye = row == col
    lower_ones = jnp.broadcast_to(tril.astype(BF16), (nb, c, c))

    load = lambda ref: ref[...].reshape(nb, c, A_HEAD)
    r, lw, k, v, kap, b = load(r_ref), load(lw_ref), load(k_ref), load(v_ref), load(kap_ref), load(b_ref)
    hi = lw.astype(BF16)
    r1 = lw - hi.astype(F32)
    mid = r1.astype(BF16)
    lo = (r1 - mid.astype(F32)).astype(BF16)
    ci = _dot(lower_ones, hi, _BNN) + (_dot(lower_ones, mid, _BNN) + _dot(lower_ones, lo, _BNN))
    ce = ci - lw
    cend = ci[:, c - 1:c, :]
    gn = jnp.exp(-ci)
    gend = jnp.exp(cend - ci)
    kap_h = kap * jnp.exp(ce)
    r_h = r * jnp.exp(ci)
    b_h = b * gn
    k_h = k * gn
    b_t = b * gend
    k_t = k * gend
    p = _dot3(jnp.concatenate([kap_h, r_h], axis=1), jnp.concatenate([b_h, k_h], axis=1), _BNT)
    a_ab = jnp.where(stril, p[:, :c, :c], 0.0)
    a_ak = jnp.where(stril, p[:, :c, c:], 0.0)
    r_b = jnp.where(tril, p[:, c:, :c], 0.0)
    r_k = jnp.where(tril, p[:, c:, c:], 0.0)
    t = _tri_inverse(a_ab, row, col)
    x = _dot3(t, jnp.concatenate([kap_h, _dot3(a_ak, v, _BNN)], axis=2), _BNN)
    wr = jnp.concatenate([x[:, :, :A_HEAD], r_h], axis=1)
    rbk = jnp.concatenate([r_b, r_k], axis=2)
    btk = jnp.concatenate([b_t, k_t, jnp.where(eye, jnp.exp(cend), 0.0)], axis=1)

    pick = lambda a, ch: a.reshape(A_HEADS, nch, *a.shape[1:])[:, ch]
    m = state_ref[...]
    for ch in range(nch):
        wm = _dot3(pick(wr, ch), m, _BNN)
        u = -(wm[:, :c] + pick(x, ch)[:, :, A_HEAD:])
        uv = jnp.concatenate([u, pick(v, ch)], axis=1)
        y = wm[:, c:] + _dot3(pick(rbk, ch), uv, _BNN)
        m = _dot3(pick(btk, ch), jnp.concatenate([uv, m], axis=1), _BTN)
        ym = jnp.mean(y, axis=-1, keepdims=True)
        yc = y - ym
        yv = jnp.mean(yc * yc, axis=-1, keepdims=True)
        y_ref[:, ch * c:(ch + 1) * c, :] = yc * lax.rsqrt(yv + A_GN_EPS)
    state_ref[...] = m


RWKV_CHUNKS_PER_STEP = 2


def rwkv_chunks(r, lw, k, v, kap, b):
    _, s, _ = r.shape
    nch = RWKV_CHUNKS_PER_STEP
    spec = pl.BlockSpec((A_HEADS, nch * CHUNK, A_HEAD), lambda n: (0, n, 0))
    return pl.pallas_call(
        functools.partial(_rwkv_chunk_kernel, nch=nch),
        grid=(s // (nch * CHUNK),),
        in_specs=[spec] * 6,
        out_specs=spec,
        out_shape=jax.ShapeDtypeStruct((A_HEADS, s, A_HEAD), F32),
        scratch_shapes=[pltpu.VMEM((A_HEADS, A_HEAD, A_HEAD), F32)],
        compiler_params=_cparams("arbitrary"),
    )(r, lw, k, v, kap, b)


def _rwkv_post_kernel(y_ref, bonus_ref, g_ref, gg_ref, gb_ref, o_ref):
    y = jnp.concatenate([y_ref[h] for h in range(A_HEADS)], axis=1)
    o_ref[...] = ((y * gg_ref[...] + gb_ref[...] + bonus_ref[...]) * g_ref[...]).astype(o_ref.dtype)


def rwkv_post(y, bonus, g, gn_g, gn_b, *, tm):
    _, s, _ = y.shape
    row = lambda i: (i, 0)
    const = lambda i: (0, 0)
    return pl.pallas_call(
        _rwkv_post_kernel,
        grid=(s // tm,),
        in_specs=[pl.BlockSpec((A_HEADS, tm, A_HEAD), lambda i: (0, i, 0)),
                  pl.BlockSpec((tm, A_WIDTH), row), pl.BlockSpec((tm, A_WIDTH), row),
                  pl.BlockSpec((1, A_WIDTH), const), pl.BlockSpec((1, A_WIDTH), const)],
        out_specs=pl.BlockSpec((tm, A_WIDTH), row),
        out_shape=jax.ShapeDtypeStruct((s, A_WIDTH), BF16),
        compiler_params=_cparams("parallel"),
    )(y, bonus, g, gn_g.reshape(1, -1), gn_b.reshape(1, -1))


def _rglru_kernel(xb_ref, gate_ref, halo_ref, cw_ref, cb_ref, wr_ref, br_ref, wi_ref, bi_ref, lam_ref,
                  o_ref, xe_ref, a_ref, u_ref, h_ref, carry_ref, *, tm):
    i = pl.program_id(0)

    @pl.when(i == 0)
    def _():
        carry_ref[...] = jnp.zeros_like(carry_ref)

    xe_ref[0:HALO, :] = halo_ref[...] * (i > 0).astype(F32)
    xe_ref[HALO:, :] = xb_ref[...]
    xc = cb_ref[...] + cw_ref[B_CONV - 1:B_CONV, :] * xe_ref[HALO:, :]
    for d in range(1, B_CONV):
        xc = xc + cw_ref[B_CONV - 1 - d:B_CONV - d, :] * xe_ref[HALO - d:HALO - d + tm, :]
    xcb = xc.astype(BF16)
    r = jax.nn.sigmoid(_dot(xcb, wr_ref[...]) + br_ref[...])
    gi = jax.nn.sigmoid(_dot(xcb, wi_ref[...]) + bi_ref[...])
    log_a = -B_C * r * _softplus(-lam_ref[...])
    a = jnp.exp(log_a)
    a_ref[...] = a
    u_ref[...] = jnp.sqrt(-jnp.tanh(log_a) * (a * a + 1.0)) * (gi * xc)

    def group(gidx, h):
        base = pl.multiple_of(gidx * HALO, HALO)
        a8 = a_ref[pl.ds(base, HALO), :]
        u8 = u_ref[pl.ds(base, HALO), :]
        rows = []
        for rr in range(HALO):
            h = a8[rr:rr + 1, :] * h + u8[rr:rr + 1, :]
            rows.append(h)
        h_ref[pl.ds(base, HALO), :] = jnp.concatenate(rows, axis=0)
        return h

    carry_ref[...] = lax.fori_loop(0, tm // HALO, group, carry_ref[...])
    o_ref[...] = (jax.nn.gelu(gate_ref[...]) * h_ref[...]).astype(o_ref.dtype)


def rglru(z_b, conv_w, conv_b, wr, b_r, wi, b_i, lam, *, tm):
    s = z_b.shape[0]
    hb = tm // HALO
    row = lambda i: (i, 0)
    const = lambda i: (0, 0)
    vec = lambda a: a.reshape(1, -1)
    return pl.pallas_call(
        functools.partial(_rglru_kernel, tm=tm),
        grid=(s // tm,),
        in_specs=[pl.BlockSpec((tm, B_WIDTH), row), pl.BlockSpec((tm, B_WIDTH), lambda i: (i, 1)),
                  pl.BlockSpec((HALO, B_WIDTH), lambda i: (jnp.maximum(i * hb - 1, 0), 0)),
                  pl.BlockSpec((B_CONV, B_WIDTH), const), pl.BlockSpec((1, B_WIDTH), const),
                  pl.BlockSpec((B_WIDTH, B_WIDTH), const), pl.BlockSpec((1, B_WIDTH), const),
                  pl.BlockSpec((B_WIDTH, B_WIDTH), const), pl.BlockSpec((1, B_WIDTH), const),
                  pl.BlockSpec((1, B_WIDTH), const)],
        out_specs=pl.BlockSpec((tm, B_WIDTH), row),
        out_shape=jax.ShapeDtypeStruct((s, B_WIDTH), BF16),
        scratch_shapes=[pltpu.VMEM((tm + HALO, B_WIDTH), F32), pltpu.VMEM((tm, B_WIDTH), F32),
                        pltpu.VMEM((tm, B_WIDTH), F32), pltpu.VMEM((tm, B_WIDTH), F32),
                        pltpu.VMEM((1, B_WIDTH), F32)],
        compiler_params=_cparams("arbitrary"),
    )(z_b, z_b, z_b, conv_w, vec(conv_b), wr, vec(b_r), wi, vec(b_i), vec(lam))


ODD_PAD = 1024
ODD_KIDX_AT = 768
ODD_WIDX_AT = 896
QB = 256
SCORE_SCALE = (IDX_HEADS ** -0.5) * (IDX_DIM ** -0.5)
INT_MIN = -(2 ** 31)
CHUNK_SHIFT = 6
CUT_BITS = 14
HEAD_GROUP = 2
FAR_TILES = 2


def _dsa_in_kernel(x_ref, w_ref, qn_ref, kvn_ref, kg_ref, kb_ref, cq_o, ckv_o, kidx_o, widx_o):
    acc = _dot(x_ref[...], w_ref[...])
    cq = acc[:, 0:C_Q_RANK]
    ckv = acc[:, C_Q_RANK:C_Q_RANK + C_KV_RANK]
    kidx = acc[:, ODD_KIDX_AT:ODD_KIDX_AT + IDX_DIM]
    widx = acc[:, ODD_WIDX_AT:ODD_WIDX_AT + IDX_HEADS]
    rms = lambda t, g: t * lax.rsqrt(jnp.mean(t * t, axis=-1, keepdims=True) + 1e-6) * g
    cq_o[...] = rms(cq, qn_ref[...]).astype(BF16)
    ckv_o[...] = rms(ckv, kvn_ref[...]).astype(BF16)
    kidx_o[...] = _layer_norm_rows(kidx, kg_ref[...], kb_ref[...], LN_EPS).astype(BF16)
    widx_o[...] = widx * SCORE_SCALE


def dsa_in(xb, w_pad, q_norm, kv_norm, kidx_g, kidx_b, *, tm):
    s, d = xb.shape
    row = lambda i: (i, 0)
    const = lambda i: (0, 0)
    vec = lambda a: a.reshape(1, -1)
    return pl.pallas_call(
        _dsa_in_kernel,
        grid=(s // tm,),
        in_specs=[pl.BlockSpec((tm, d), row), pl.BlockSpec((d, ODD_PAD), const),
                  pl.BlockSpec((1, C_Q_RANK), const), pl.BlockSpec((1, C_KV_RANK), const),
                  pl.BlockSpec((1, IDX_DIM), const), pl.BlockSpec((1, IDX_DIM), const)],
        out_specs=[pl.BlockSpec((tm, C_Q_RANK), row), pl.BlockSpec((tm, C_KV_RANK), row),
                   pl.BlockSpec((tm, IDX_DIM), row), pl.BlockSpec((tm, IDX_HEADS), row)],
        out_shape=[jax.ShapeDtypeStruct((s, C_Q_RANK), BF16), jax.ShapeDtypeStruct((s, C_KV_RANK), BF16),
                   jax.ShapeDtypeStruct((s, IDX_DIM), BF16), jax.ShapeDtypeStruct((s, IDX_HEADS), F32)],
        compiler_params=_cparams("parallel"),
    )(xb, w_pad, vec(q_norm), vec(kv_norm), vec(kidx_g), vec(kidx_b))


def _qabs_kernel(cq_ref, wuq_ref, wuk_ref, o_ref):
    q = _dot(cq_ref[...], wuq_ref[...]).astype(BF16)
    for h in range(C_HEADS):
        qa = _dot(q[:, h * C_HEAD_DIM:(h + 1) * C_HEAD_DIM], wuk_ref[h], _NT)
        o_ref[h] = (qa * (C_HEAD_DIM ** -0.5 * LOG2E)).astype(BF16)


def dsa_qabs(cq, w_uq, w_uk, *, tm):
    s = cq.shape[0]
    return pl.pallas_call(
        _qabs_kernel,
        grid=(s // tm,),
        in_specs=[pl.BlockSpec((tm, C_Q_RANK), lambda i: (i, 0)),
                  pl.BlockSpec((C_Q_RANK, C_HEADS * C_HEAD_DIM), lambda i: (0, 0)),
                  pl.BlockSpec((C_HEADS, C_KV_RANK, C_HEAD_DIM), lambda i: (0, 0, 0))],
        out_specs=pl.BlockSpec((C_HEADS, tm, C_KV_RANK), lambda i: (0, i, 0)),
        out_shape=jax.ShapeDtypeStruct((C_HEADS, s, C_KV_RANK), BF16),
        compiler_params=_cparams("parallel"),
    )(cq, w_uq, w_uk)


def _sortable_key(score):
    bits = lax.bitcast_convert_type(score + 0.0, jnp.int32)
    return jnp.where(bits < 0, bits ^ jnp.int32(0x7FFFFFFF), bits)


def _dsa_attn_kernel(qabs_ref, qidx_ref, widx_t_ref, kidx_ref, ckv_ref, wuv_ref, bias_ref,
                     o_ref, keys_ref, acc_ref, m_ref, l_ref, *, k_sel):
    i = pl.program_id(0)
    n_tiles = i + 1
    k_local = lax.broadcasted_iota(jnp.int32, (QB, QB), 0)
    q_local = lax.broadcasted_iota(jnp.int32, (QB, QB), 1)
    allowed_diag_t = (k_local >> CHUNK_SHIFT) <= (q_local >> CHUNK_SHIFT)

    widx_t = widx_t_ref[...]

    def score_tile(t, carry):
        off = pl.multiple_of(t * QB, QB)
        kt = kidx_ref[pl.ds(off, QB), :]
        sc = jnp.zeros((QB, QB), F32)
        for j in range(IDX_HEADS):
            d = _dot(kt, qidx_ref[:, j * IDX_DIM:(j + 1) * IDX_DIM], _NT)
            sc = sc + widx_t[j:j + 1, :] * jnp.maximum(d, 0.0)
        key = _sortable_key(sc)
        key = jnp.where(jnp.logical_or(t < i, allowed_diag_t), key, jnp.int32(INT_MIN))
        keys_ref[pl.ds(off, QB), :] = key
        return carry

    lax.fori_loop(0, n_tiles, score_tile, 0)

    def selected(kt, off, thr, cut):
        pos = lax.broadcasted_iota(jnp.int32, kt.shape, 0) + off
        return jnp.logical_or(kt > thr, jnp.logical_and(kt == thr, pos < cut))

    def count(pred):
        def body(t, acc):
            off = pl.multiple_of(t * QB, QB)
            hit = pred(keys_ref[pl.ds(off, QB), :], off).astype(jnp.int32)
            return acc + jnp.sum(hit.reshape(QB // HALO, HALO, QB), axis=0)
        acc = lax.fori_loop(0, n_tiles, body, jnp.zeros((HALO, QB), jnp.int32))
        return jnp.sum(acc.astype(F32), axis=0, keepdims=True).astype(jnp.int32)

    def thr_bit(it, thr):
        cand = thr + jnp.left_shift(jnp.int32(1), 31 - it)
        cnt = count(lambda kt, off: kt >= cand)
        return jnp.where(cnt >= k_sel, cand, thr)

    thr = lax.fori_loop(0, 32, thr_bit, jnp.full((1, QB), INT_MIN, jnp.int32))

    def cut_bit(it, cut):
        cand = cut + jnp.left_shift(jnp.int32(1), CUT_BITS - 1 - it)
        cnt = count(lambda kt, off: selected(kt, off, thr, cand))
        return jnp.where(cnt <= k_sel, cand, cut)

    n_ge = count(lambda kt, off: kt >= thr)
    tied = jnp.logical_and(n_ge > k_sel, thr > INT_MIN)
    any_tied = jnp.max(jnp.where(tied, 1.0, 0.0)) > 0.0
    cut = lax.cond(any_tied,
                   lambda: lax.fori_loop(0, CUT_BITS, cut_bit, jnp.zeros((1, QB), jnp.int32)),
                   lambda: jnp.full((1, QB), 2 ** CUT_BITS - 1, jnp.int32))
    rep_lanes = lambda a, n: jnp.concatenate([a] * (n // LANES), axis=-1)

    m_ref[...] = jnp.full(m_ref.shape, NEG_BIG, F32)
    l_ref[...] = jnp.zeros(l_ref.shape, F32)
    acc_ref[...] = jnp.zeros(acc_ref.shape, F32)

    def attend(off, width, near):
        sel = selected(keys_ref[pl.ds(off, width), :], off, thr, cut)
        if near == 1:
            sel = jnp.logical_and(sel, allowed_diag_t)
        mask_add = jnp.where(sel, 0.0, NEG_BIG).T[None]
        kv = ckv_ref[pl.ds(off, width), :]
        rep = lambda a, n: jnp.concatenate([a] * (n // LANES), axis=-1)

        for g in range(C_HEADS // HEAD_GROUP):
            hs = slice(g * HEAD_GROUP, (g + 1) * HEAD_GROUP)
            q = qabs_ref[hs].reshape(HEAD_GROUP * QB, C_KV_RANK)
            s = _dot(q, kv, _NT).reshape(HEAD_GROUP, QB, width) + mask_add
            if near is not None:
                s = s + bias_ref[near, hs]
            m_old = m_ref[hs]
            row_max = jnp.broadcast_to(jnp.max(s, axis=-1, keepdims=True), m_old.shape)
            m_new = jnp.maximum(m_old, row_max)
            alpha = jnp.exp2(m_old - m_new)
            p = jnp.exp2(s - rep(m_new, width))
            row_sum = jnp.broadcast_to(jnp.sum(p, axis=-1, keepdims=True), m_old.shape)
            l_ref[hs] = alpha * l_ref[hs] + row_sum
            pv = _dot(p.astype(BF16).reshape(HEAD_GROUP * QB, width), kv).reshape(HEAD_GROUP, QB, C_KV_RANK)
            acc_ref[hs] = rep(alpha, C_KV_RANK) * acc_ref[hs] + pv
            m_ref[hs] = m_new

    n_far = jnp.maximum(i - 1, 0)

    n_wide = n_far // FAR_TILES

    def far_wide(tw, carry):
        attend(pl.multiple_of(tw * (FAR_TILES * QB), FAR_TILES * QB), FAR_TILES * QB, None)
        return carry

    def far_single(t, carry):
        attend(pl.multiple_of(t * QB, QB), QB, None)
        return carry

    lax.fori_loop(0, n_wide, far_wide, 0)
    lax.fori_loop(n_wide * FAR_TILES, n_far, far_single, 0)

    @pl.when(i >= 1)
    def _():
        attend(pl.multiple_of((i - 1) * QB, QB), QB, 0)

    attend(pl.multiple_of(i * QB, QB), QB, 1)

    for h in range(C_HEADS):
        o_lat = (acc_ref[h] / rep_lanes(l_ref[h], C_KV_RANK)).astype(BF16)
        o_ref[:, h * C_HEAD_DIM:(h + 1) * C_HEAD_DIM] = _dot(o_lat, wuv_ref[h]).astype(o_ref.dtype)


def dsa_attention(qabs, qidx, widx_t, kidx, ckv, w_uv, bias_near, *, k_sel):
    _, s, _ = qabs.shape
    full2 = lambda i: (0, 0)
    once = pl.Buffered(1)
    return pl.pallas_call(
        functools.partial(_dsa_attn_kernel, k_sel=k_sel),
        grid=(s // QB,),
        in_specs=[pl.BlockSpec((C_HEADS, QB, C_KV_RANK), lambda i: (0, i, 0)),
                  pl.BlockSpec((QB, IDX_HEADS * IDX_DIM), lambda i: (i, 0)),
                  pl.BlockSpec((IDX_HEADS, QB), lambda i: (0, i)),
                  pl.BlockSpec((s, IDX_DIM), full2, pipeline_mode=once),
                  pl.BlockSpec((s, C_KV_RANK), full2, pipeline_mode=once),
                  pl.BlockSpec((C_HEADS, C_KV_RANK, C_HEAD_DIM), lambda i: (0, 0, 0), pipeline_mode=once),
                  pl.BlockSpec((2, C_HEADS, QB, QB), lambda i: (0, 0, 0, 0), pipeline_mode=once)],
        out_specs=pl.BlockSpec((QB, C_HEADS * C_HEAD_DIM), lambda i: (i, 0)),
        out_shape=jax.ShapeDtypeStruct((s, C_HEADS * C_HEAD_DIM), BF16),
        scratch_shapes=[pltpu.VMEM((s, QB), jnp.int32),
                        pltpu.VMEM((C_HEADS, QB, C_KV_RANK), F32),
                        pltpu.VMEM((C_HEADS, QB, LANES), F32),
                        pltpu.VMEM((C_HEADS, QB, LANES), F32)],
        compiler_params=_cparams("parallel"),
    )(qabs, qidx, widx_t, kidx, ckv, w_uv, bias_near)


def _t5_bucket(rel):
    nb = REL_BUCKETS // 2
    max_exact = nb // 2
    ret = jnp.where(rel > 0, nb, 0)
    n = jnp.abs(rel)
    nf = jnp.maximum(n, 1).astype(jnp.float32)
    large = max_exact + (jnp.log(nf / max_exact) / math.log(REL_MAX_DIST / max_exact) * (nb - max_exact)).astype(jnp.int32)
    large = jnp.minimum(large, nb - 1)
    return ret + jnp.where(n < max_exact, n, large)


def _near_bias(rel_bias):
    ql = jnp.arange(QB)[:, None]
    sl = jnp.arange(QB)[None, :]
    rel = jnp.stack([sl - ql - QB, sl - ql])
    far = rel_bias[_t5_bucket(jnp.array(-2 * QB))]
    table = ((rel_bias - far) * LOG2E).astype(F32)
    onehot = jax.nn.one_hot(_t5_bucket(rel), REL_BUCKETS, dtype=F32)
    return jnp.einsum('tqsb,bh->thqs', onehot, table, precision=lax.Precision.HIGHEST)


def _pad_rows(w, at, total):
    return jnp.zeros((total, w.shape[1]), w.dtype).at[at:at + w.shape[0]].set(w)


def _block_diag(w):
    n, d, e = w.shape
    eye = jnp.eye(n, dtype=w.dtype)
    return (eye[:, None, :, None] * w[:, :, None, :]).reshape(n * d, n * e)


def _even_mixer(x, xb, w_in, w_out, j, mu, w0, w2, a0, a2, g2, k_k, k_a, r_k, gn_g, gn_b,
                conv_w, conv_b, w_r, b_r, w_i, b_i, lam, ln_g, ln_b, tm):
    w_in = w_in.astype(BF16)
    n_rkv = 3 * A_WIDTH
    z_rkv = matmul(xb, w_in[:, :n_rkv], tm=tm, tn=1024)
    w_lo = jnp.pad(w_in[:, n_rkv:A_COLS], ((0, 0), (0, A_LORA_PAD - A_LORA)))
    z_lo = matmul(xb, w_lo, tm=tm, tn=A_LORA_PAD)
    z_b = matmul(xb, w_in[:, A_COLS:], tm=tm, tn=1024)

    mu_lo = jnp.pad(mu[n_rkv:], (0, A_LORA_PAD - A_LORA))
    w2p = _pad_rows(w2, 0, A_LORA_PAD).astype(BF16)
    a2p = _pad_rows(a2, A_DECAY_LORA, A_LORA_PAD).astype(BF16)
    g2p = _pad_rows(g2, A_DECAY_LORA + A_ICL_LORA, A_LORA_PAD).astype(BF16)
    bd = _block_diag(jnp.ones((A_HEADS, A_HEAD, A_HEAD), BF16))
    r, lw, k2, v, kap, b, g, bonus = rwkv_prep(
        z_rkv, z_lo, mu[:n_rkv], mu_lo, w0, w2p, a0, a2p, g2p, k_k, k_a, r_k.reshape(-1), bd,
        tm=min(tm, 256))
    y = rwkv_chunks(r, lw, k2, v, kap, b)
    y_a = rwkv_post(y, bonus, g, gn_g, gn_b, tm=tm)

    y_b = rglru(z_b, conv_w, conv_b, _block_diag(w_r).astype(BF16), b_r,
                _block_diag(w_i).astype(BF16), b_i, lam, tm=min(tm, 256))
    return matmul2_residual_ln(y_a, y_b, w_out, j, x, ln_g, ln_b, tm=tm)


def _odd_mixer(x, xb, w_in, w_out, j, q_norm, kv_norm, w_uq, w_uk, w_uv, w_qidx, kidx_g, kidx_b,
               bias_near, ln_g, ln_b, tm):
    s = x.shape[0]
    d = w_in.shape[0]
    n_qkv = C_Q_RANK + C_KV_RANK
    w_pad = jnp.zeros((d, ODD_PAD), BF16)
    w_pad = w_pad.at[:, :n_qkv].set(w_in[:, :n_qkv].astype(BF16))
    w_pad = w_pad.at[:, ODD_KIDX_AT:ODD_KIDX_AT + IDX_DIM].set(w_in[:, n_qkv:n_qkv + IDX_DIM].astype(BF16))
    w_pad = w_pad.at[:, ODD_WIDX_AT:ODD_WIDX_AT + IDX_HEADS].set(w_in[:, n_qkv + IDX_DIM:].astype(BF16))
    cq, ckv, kidx, widx = dsa_in(xb, w_pad, q_norm, kv_norm, kidx_g, kidx_b, tm=tm)
    qabs = dsa_qabs(cq, w_uq.astype(BF16), w_uk.astype(BF16), tm=tm)
    qidx = matmul(cq, w_qidx.astype(BF16), tm=tm, tn=IDX_HEADS * IDX_DIM, out_dtype=BF16)
    o = dsa_attention(qabs, qidx, widx.T, kidx, ckv, w_uv.astype(BF16), bias_near,
                      k_sel=min(TOPK_MAX, s // 4))
    return matmul_residual_ln(o, w_out, j, x, ln_g, ln_b, tm=tm, tk=o.shape[1])


def kernel(x, p, rel_bias, ln1_g, ln1_b, ln2_g, ln2_b, ffn_w_up, ffn_conv_w, ffn_conv_b, ffn_w_down, ple_w_proj, ple_w_gate, ev_w_in, ev_w_out, a_mu, a_w0, a_w2, a_a0, a_a2, a_g2, a_k_k, a_k_a, a_r_k, a_gn_g, a_gn_b, b_conv_w, b_conv_b, b_w_r, b_b_r, b_w_i, b_b_i, b_lambda, od_w_in, od_w_out, c_q_norm, c_kv_norm, c_w_uq, c_w_uk, c_w_uv, c_w_qidx, c_kidx_g, c_kidx_b):
    bsz, s, d = x.shape
    assert bsz == 1 and s % QB == 0 and s <= 2 ** CUT_BITS
    tm = min(512, s)
    x = x[0]
    xb = x.astype(BF16)
    bias_near = _near_bias(rel_bias)
    ev_w_out_b, od_w_out_b = ev_w_out.astype(BF16), od_w_out.astype(BF16)
    w_up_b, w_down_b = ffn_w_up.astype(BF16), ffn_w_down.astype(BF16)
    w_gate_b, w_proj_b, p_b = ple_w_gate.astype(BF16), ple_w_proj.astype(BF16), p.astype(BF16)
    for layer in range(DEPTH):
        j = layer // 2
        if layer % 2 == 0:
            x, xb = _even_mixer(x, xb, ev_w_in[j], ev_w_out_b, j, a_mu[j], a_w0[j], a_w2[j], a_a0[j], a_a2[j],
                                a_g2[j], a_k_k[j], a_k_a[j], a_r_k[j], a_gn_g[j], a_gn_b[j],
                                b_conv_w[j], b_conv_b[j], b_w_r[j], b_b_r[j], b_w_i[j], b_b_i[j], b_lambda[j],
                                ln1_g[layer], ln1_b[layer], tm)
        else:
            x, xb = _odd_mixer(x, xb, od_w_in[j], od_w_out_b, j, c_q_norm[j], c_kv_norm[j], c_w_uq[j], c_w_uk[j],
                               c_w_uv[j], c_w_qidx[j], c_kidx_g[j], c_kidx_b[j], bias_near,
                               ln1_g[layer], ln1_b[layer], tm)
        hm = ffn_up(xb, w_up_b, layer, ffn_conv_w[layer], ffn_conv_b[layer], tm=min(4 * tm, s), sub=min(2 * tm, s), tc=512)
        x, xb = matmul_residual_ln(hm, w_down_b, layer, x, ln2_g[layer], ln2_b[layer], tm=tm, tk=D_FF // 4)
        x, xb = ple_update(xb, w_gate_b, p_b, w_proj_b, layer, x, tm=min(2 * tm, s), tn=1024)
    return x[None]
```

```python
import functools
import math

import jax
import jax.numpy as jnp
from jax import lax
from jax.experimental import pallas as pl
from jax.experimental.pallas import tpu as pltpu

F32 = jnp.float32
BF16 = jnp.bfloat16

D_MODEL = 2048
DEPTH = 4
CHUNK = 64
DN_ALPHA = (2 * DEPTH) ** 0.25
LN_EPS = 1e-5
A_WIDTH = 1024
A_HEAD = 64
A_HEADS = 16
A_DECAY_LORA = 64
A_ICL_LORA = 64
A_GATE_LORA = 160
A_LORA = A_DECAY_LORA + A_ICL_LORA + A_GATE_LORA
A_LORA_PAD = 384
A_COLS = 3 * A_WIDTH + A_LORA
A_GN_EPS = 64e-5
B_WIDTH = 1024
B_BLOCKS = 16
B_BLOCK = 64
B_CONV = 4
B_C = 8.0
C_HEADS = 16
C_HEAD_DIM = 128
C_Q_RANK = 512
C_KV_RANK = 256
IDX_HEADS = 16
IDX_DIM = 64
TOPK_MAX = 256
REL_BUCKETS = 32
REL_MAX_DIST = 128
D_FF = 5632
FFN_CONV = 3
PLE_DIM = 256

VMEM_LIMIT_BYTES = 56 * 1024 * 1024
NEG_BIG = -1e30
HALO = 8
LANES = 128
LOG2E = math.log2(math.e)


def _cparams(*sem):
    return pltpu.CompilerParams(dimension_semantics=sem, vmem_limit_bytes=VMEM_LIMIT_BYTES)


def _split_bf16(a):
    hi = a.astype(BF16)
    lo = (a - hi.astype(F32)).astype(BF16)
    return hi, lo


def _dot(a, b, dims=(((1,), (0,)), ((), ()))):
    return lax.dot_general(a, b, dims, preferred_element_type=F32)


_NT = (((1,), (1,)), ((), ()))
_TN = (((0,), (0,)), ((), ()))
_NN = (((1,), (0,)), ((), ()))


def _dot3(a, b, dims=_NN):
    ah, al = _split_bf16(a)
    bh, bl = _split_bf16(b)
    return _dot(ah, bh, dims) + (_dot(ah, bl, dims) + _dot(al, bh, dims))


def _dot_exact_rhs(a, b_bf16, dims=_NN):
    hi = a.astype(BF16)
    r1 = a - hi.astype(F32)
    mid = r1.astype(BF16)
    lo = (r1 - mid.astype(F32)).astype(BF16)
    return _dot(hi, b_bf16, dims) + (_dot(mid, b_bf16, dims) + _dot(lo, b_bf16, dims))


def _layer_norm_rows(v, g, b, eps):
    mu = jnp.mean(v, axis=-1, keepdims=True)
    d = v - mu
    var = jnp.mean(d * d, axis=-1, keepdims=True)
    return d * lax.rsqrt(var + eps) * g + b


def _softplus(x):
    return jnp.maximum(x, 0.0) + jnp.log1p(jnp.exp(-jnp.abs(x)))


def _mm_kernel(a_ref, b_ref, o_ref):
    o_ref[...] = _dot(a_ref[...], b_ref[...]).astype(o_ref.dtype)


def matmul(a, b, *, tm, tn, out_dtype=F32):
    m, k = a.shape
    _, n = b.shape
    assert m % tm == 0 and n % tn == 0
    return pl.pallas_call(
        _mm_kernel,
        grid=(n // tn, m // tm),
        in_specs=[pl.BlockSpec((tm, k), lambda j, i: (i, 0)),
                  pl.BlockSpec((k, tn), lambda j, i: (0, j))],
        out_specs=pl.BlockSpec((tm, tn), lambda j, i: (i, j)),
        out_shape=jax.ShapeDtypeStruct((m, n), out_dtype),
        compiler_params=_cparams("parallel", "parallel"),
    )(a, b)


def _mm_ln_kernel(a_ref, b_ref, x_ref, g_ref, beta_ref, o_ref, ob_ref, acc_ref, *, nk):
    kk = pl.program_id(1)

    @pl.when(kk == 0)
    def _():
        acc_ref[...] = jnp.zeros_like(acc_ref)

    acc_ref[...] += _dot(a_ref[...], b_ref[...])

    @pl.when(kk == nk - 1)
    def _():
        v = DN_ALPHA * x_ref[...] + acc_ref[...]
        y = _layer_norm_rows(v, g_ref[...], beta_ref[...], LN_EPS)
        o_ref[...] = y
        ob_ref[...] = y.astype(BF16)


def matmul_residual_ln(a, b, layer, x, g, beta, *, tm, tk):
    m, k = a.shape
    n = b.shape[2]
    assert m % tm == 0 and k % tk == 0
    nk = k // tk
    return pl.pallas_call(
        functools.partial(_mm_ln_kernel, nk=nk),
        grid=(m // tm, nk),
        in_specs=[pl.BlockSpec((tm, tk), lambda i, kk: (i, kk)),
                  pl.BlockSpec((None, tk, n), lambda i, kk: (layer, kk, 0)),
                  pl.BlockSpec((tm, n), lambda i, kk: (i, 0)),
                  pl.BlockSpec((1, n), lambda i, kk: (0, 0)),
                  pl.BlockSpec((1, n), lambda i, kk: (0, 0))],
        out_specs=[pl.BlockSpec((tm, n), lambda i, kk: (i, 0)),
                   pl.BlockSpec((tm, n), lambda i, kk: (i, 0))],
        out_shape=[jax.ShapeDtypeStruct((m, n), F32), jax.ShapeDtypeStruct((m, n), BF16)],
        scratch_shapes=[pltpu.VMEM((tm, n), F32)],
        compiler_params=_cparams("parallel", "arbitrary"),
    )(a, b, x, g.reshape(1, n), beta.reshape(1, n))


def _mm2_ln_kernel(a1_ref, a2_ref, b1_ref, b2_ref, x_ref, g_ref, beta_ref, o_ref, ob_ref):
    acc = _dot(a1_ref[...], b1_ref[...]) + _dot(a2_ref[...], b2_ref[...])
    y = _layer_norm_rows(DN_ALPHA * x_ref[...] + acc, g_ref[...], beta_ref[...], LN_EPS)
    o_ref[...] = y
    ob_ref[...] = y.astype(BF16)


def matmul2_residual_ln(a1, a2, b, layer, x, g, beta, *, tm):
    m, k1 = a1.shape
    k2 = a2.shape[1]
    n = b.shape[2]
    assert k1 == k2
    row = lambda i: (i, 0)
    const = lambda i: (0, 0)
    return pl.pallas_call(
        _mm2_ln_kernel,
        grid=(m // tm,),
        in_specs=[pl.BlockSpec((tm, k1), row), pl.BlockSpec((tm, k2), row),
                  pl.BlockSpec((None, k1, n), lambda i: (layer, 0, 0)),
                  pl.BlockSpec((None, k2, n), lambda i: (layer, 1, 0)),
                  pl.BlockSpec((tm, n), row), pl.BlockSpec((1, n), const), pl.BlockSpec((1, n), const)],
        out_specs=[pl.BlockSpec((tm, n), row), pl.BlockSpec((tm, n), row)],
        out_shape=[jax.ShapeDtypeStruct((m, n), F32), jax.ShapeDtypeStruct((m, n), BF16)],
        compiler_params=_cparams("parallel"),
    )(a1, a2, b, b, x, g.reshape(1, n), beta.reshape(1, n))


def _ple_kernel(xb_ref, wg_ref, p_ref, wp_ref, x_ref, o_ref, ob_ref):
    gate = jax.nn.sigmoid(_dot(xb_ref[...], wg_ref[...]))
    proj = _dot(p_ref[...], wp_ref[...])
    y = x_ref[...] + gate * proj
    o_ref[...] = y
    ob_ref[...] = y.astype(BF16)


def ple_update(xb, wg, pb, wp, layer, x, *, tm, tn):
    m, d = xb.shape
    pd = pb.shape[-1]
    return pl.pallas_call(
        _ple_kernel,
        grid=(d // tn, m // tm),
        in_specs=[pl.BlockSpec((tm, d), lambda j, i: (i, 0)),
                  pl.BlockSpec((None, d, tn), lambda j, i: (layer, 0, j)),
                  pl.BlockSpec((None, None, tm, pd), lambda j, i: (layer, 0, i, 0)),
                  pl.BlockSpec((None, pd, tn), lambda j, i: (layer, 0, j)),
                  pl.BlockSpec((tm, tn), lambda j, i: (i, j))],
        out_specs=[pl.BlockSpec((tm, tn), lambda j, i: (i, j)),
                   pl.BlockSpec((tm, tn), lambda j, i: (i, j))],
        out_shape=[jax.ShapeDtypeStruct((m, d), F32), jax.ShapeDtypeStruct((m, d), BF16)],
        compiler_params=_cparams("parallel", "parallel"),
    )(xb, wg, pb, wp, x)


def _ffn_up_kernel(x_ref, wg_ref, wu_ref, cwg_ref, cwu_ref, cbg_ref, cbu_ref, o_ref, eg_ref, eu_ref, *, tm, sub):
    i = pl.program_id(1)

    @pl.when(i == 0)
    def _():
        eg_ref[0:HALO, :] = jnp.zeros((HALO, eg_ref.shape[1]), F32)
        eu_ref[0:HALO, :] = jnp.zeros((HALO, eu_ref.shape[1]), F32)

    @pl.when(i > 0)
    def _():
        eg_ref[0:HALO, :] = eg_ref[tm:tm + HALO, :]
        eu_ref[0:HALO, :] = eu_ref[tm:tm + HALO, :]

    tc = o_ref.shape[1]
    halves = ((0, tc // 2), (tc // 2, tc))

    def project(k, e_ref, w_ref):
        e_ref[HALO + k * sub:HALO + (k + 1) * sub, :] = _dot(x_ref[k * sub:(k + 1) * sub, :], w_ref[...])

    def conv(e_ref, w_ref, b_ref, base, c0, c1):
        acc = b_ref[:, c0:c1] + w_ref[FFN_CONV - 1:FFN_CONV, c0:c1] * e_ref[base:base + sub, c0:c1]
        for d in range(1, FFN_CONV):
            acc = acc + w_ref[FFN_CONV - 1 - d:FFN_CONV - d, c0:c1] * e_ref[base - d:base - d + sub, c0:c1]
        return acc

    def gate_rows(k, c0, c1):
        base = HALO + k * sub
        gate = conv(eg_ref, cwg_ref, cbg_ref, base, c0, c1)
        up = conv(eu_ref, cwu_ref, cbu_ref, base, c0, c1)
        o_ref[k * sub:(k + 1) * sub, c0:c1] = (jax.nn.gelu(gate) * up).astype(o_ref.dtype)

    project(0, eg_ref, wg_ref)
    project(0, eu_ref, wu_ref)
    for k in range(tm // sub):
        more = (k + 1) * sub < tm
        if more:
            project(k + 1, eg_ref, wg_ref)
        gate_rows(k, *halves[0])
        if more:
            project(k + 1, eu_ref, wu_ref)
        gate_rows(k, *halves[1])


def ffn_up(xb, w_up, layer, conv_w, conv_b, *, tm, sub, tc):
    s, d = xb.shape
    two_ff = w_up.shape[2]
    ff = two_ff // 2
    nc = ff // tc
    cb = conv_b.reshape(1, two_ff)
    return pl.pallas_call(
        functools.partial(_ffn_up_kernel, tm=tm, sub=sub),
        grid=(nc, s // tm),
        in_specs=[pl.BlockSpec((tm, d), lambda j, i: (i, 0)),
                  pl.BlockSpec((None, d, tc), lambda j, i: (layer, 0, j)),
                  pl.BlockSpec((None, d, tc), lambda j, i: (layer, 0, j + nc)),
                  pl.BlockSpec((FFN_CONV, tc), lambda j, i: (0, j)),
                  pl.BlockSpec((FFN_CONV, tc), lambda j, i: (0, j + nc)),
                  pl.BlockSpec((1, tc), lambda j, i: (0, j)),
                  pl.BlockSpec((1, tc), lambda j, i: (0, j + nc))],
        out_specs=pl.BlockSpec((tm, tc), lambda j, i: (i, j)),
        out_shape=jax.ShapeDtypeStruct((s, ff), BF16),
        scratch_shapes=[pltpu.VMEM((tm + HALO, tc), F32), pltpu.VMEM((tm + HALO, tc), F32)],
        compiler_params=_cparams("parallel", "arbitrary"),
    )(xb, w_up, w_up, conv_w, conv_w, cb, cb)


def _shift_mix(z, halo, mu, live):
    prev = pltpu.roll(z, 1, 0)
    row0 = lax.broadcasted_iota(jnp.int32, z.shape, 0) == 0
    prev = jnp.where(row0, halo[HALO - 1:HALO, :] * live, prev)
    return z + (prev - z) * mu


def _rwkv_prep_kernel(z_ref, zh_ref, lo_ref, loh_ref, mu_ref, mulo_ref, w0_ref, w2_ref, a0_ref, a2_ref,
                      g2_ref, kk_ref, ka_ref, rk_ref, hsum_ref, hspread_ref,
                      r_o, lw_o, k_o, v_o, kap_o, b_o, g_o, bonus_o):
    live = (pl.program_id(0) > 0).astype(F32)
    z = _shift_mix(z_ref[...], zh_ref[...], mu_ref[...], live)
    lo = _shift_mix(lo_ref[...], loh_ref[...], mulo_ref[...], live)
    r = z[:, 0:A_WIDTH]
    k = z[:, A_WIDTH:2 * A_WIDTH]
    v = z[:, 2 * A_WIDTH:3 * A_WIDTH]
    w = -_softplus(-(w0_ref[...] + _dot(jnp.tanh(lo).astype(BF16), w2_ref[...]))) - 0.5
    lw = -jnp.exp(w)
    a = jax.nn.sigmoid(a0_ref[...] + _dot(lo.astype(BF16), a2_ref[...]))
    g = _dot(jax.nn.sigmoid(lo).astype(BF16), g2_ref[...])
    kk = k * kk_ref[...]
    head_sum = lambda t: _dot_exact_rhs(_dot_exact_rhs(t, hsum_ref[...]), hspread_ref[...])
    kap = kk / jnp.maximum(jnp.sqrt(head_sum(kk * kk)), 1e-12)
    k2 = k * (1.0 + (a - 1.0) * ka_ref[...])
    bonus = head_sum(r * k2 * rk_ref[...]) * v
    b = kap * a
    g_o[...] = g
    bonus_o[...] = bonus
    for h in range(A_HEADS):
        sl = slice(h * A_HEAD, (h + 1) * A_HEAD)
        r_o[h] = r[:, sl]
        lw_o[h] = lw[:, sl]
        k_o[h] = k2[:, sl]
        v_o[h] = v[:, sl]
        kap_o[h] = kap[:, sl]
        b_o[h] = b[:, sl]


def rwkv_prep(z_rkv, z_lo, mu_rkv, mu_lo, w0, w2p, a0, a2p, g2p, k_k, k_a, r_k, *, tm):
    s = z_rkv.shape[0]
    lane_head = jnp.arange(A_WIDTH)[:, None] // A_HEAD
    hsum = (lane_head == jnp.arange(LANES)[None, :]).astype(BF16)
    hspread = hsum.T
    hb = tm // HALO
    row = lambda i: (i, 0)
    halo = lambda i: (jnp.maximum(i * hb - 1, 0), 0)
    const = lambda i: (0, 0)
    hm = jax.ShapeDtypeStruct((A_HEADS, s, A_HEAD), F32)
    hm_spec = pl.BlockSpec((A_HEADS, tm, A_HEAD), lambda i: (0, i, 0))
    full = jax.ShapeDtypeStruct((s, A_WIDTH), F32)
    vec = lambda a: a.reshape(1, -1)
    return pl.pallas_call(
        _rwkv_prep_kernel,
        grid=(s // tm,),
        in_specs=[pl.BlockSpec((tm, 3 * A_WIDTH), row), pl.BlockSpec((HALO, 3 * A_WIDTH), halo),
                  pl.BlockSpec((tm, A_LORA_PAD), row), pl.BlockSpec((HALO, A_LORA_PAD), halo),
                  pl.BlockSpec((1, 3 * A_WIDTH), const), pl.BlockSpec((1, A_LORA_PAD), const),
                  pl.BlockSpec((1, A_WIDTH), const), pl.BlockSpec((A_LORA_PAD, A_WIDTH), const),
                  pl.BlockSpec((1, A_WIDTH), const), pl.BlockSpec((A_LORA_PAD, A_WIDTH), const),
                  pl.BlockSpec((A_LORA_PAD, A_WIDTH), const),
                  pl.BlockSpec((1, A_WIDTH), const), pl.BlockSpec((1, A_WIDTH), const),
                  pl.BlockSpec((1, A_WIDTH), const), pl.BlockSpec((A_WIDTH, LANES), const),
                  pl.BlockSpec((LANES, A_WIDTH), const)],
        out_specs=[hm_spec] * 6 + [pl.BlockSpec((tm, A_WIDTH), row)] * 2,
        out_shape=[hm] * 6 + [full] * 2,
        compiler_params=_cparams("parallel"),
    )(z_rkv, z_rkv, z_lo, z_lo, vec(mu_rkv), vec(mu_lo), vec(w0), w2p, vec(a0), a2p, g2p,
      vec(k_k), vec(k_a), vec(r_k), hsum, hspread)


_BNN = (((2,), (1,)), ((0,), (0,)))
_BNT = (((2,), (2,)), ((0,), (0,)))
_BTN = (((1,), (1,)), ((0,), (0,)))


def _tri_inverse(a, row, col):
    mm1 = lambda p, q: _dot(p.astype(BF16), q.astype(BF16), _BNN)
    mm3 = lambda p, q: _dot3(p, q, _BNN)
    eye = (row == col).astype(F32)
    ad = jnp.where((row >> 3) == (col >> 3), a, 0.0)
    t = eye - ad
    a2 = mm1(ad, ad)
    t = t + mm1(t, a2)
    a4 = mm1(a2, a2)
    t = t + mm1(t, a4)
    for sh, mm in ((3, mm1), (4, mm3), (5, mm3)):
        inner = (row >> sh) == (col >> sh)
        outer = (row >> (sh + 1)) == (col >> (sh + 1))
        aoff = jnp.where(jnp.logical_and(outer, jnp.logical_not(inner)), a, 0.0)
        t = t - mm(mm(t, aoff), t)
    return t


def _rwkv_chunk_kernel(r_ref, lw_ref, k_ref, v_ref, kap_ref, b_ref, y_ref, state_ref, *, nch):
    c = CHUNK
    nb = A_HEADS * nch

    @pl.when(pl.program_id(0) == 0)
    def _():
        state_ref[...] = jnp.zeros_like(state_ref)

    row = lax.broadcasted_iota(jnp.int32, (1, c, c), 1)
    col = lax.broadcasted_iota(jnp.int32, (1, c, c), 2)
    tril = row >= col
    stril = row > col
    eye = row == col
    lower_ones = jnp.broadcast_to(tril.astype(BF16), (nb, c, c))

    load = lambda ref: ref[...].reshape(nb, c, A_HEAD)
    r, lw, k, v, kap, b = load(r_ref), load(lw_ref), load(k_ref), load(v_ref), load(kap_ref), load(b_ref)
    hi = lw.astype(BF16)
    r1 = lw - hi.astype(F32)
    mid = r1.astype(BF16)
    lo = (r1 - mid.astype(F32)).astype(BF16)
    ci = _dot(lower_ones, hi, _BNN) + (_dot(lower_ones, mid, _BNN) + _dot(lower_ones, lo, _BNN))
    ce = ci - lw
    cend = ci[:, c - 1:c, :]
    gn = jnp.exp(-ci)
    gend = jnp.exp(cend - ci)
    kap_h = kap * jnp.exp(ce)
    r_h = r * jnp.exp(ci)
    b_h = b * gn
    k_h = k * gn
    b_t = b * gend
    k_t = k * gend
    p = _dot3(jnp.concatenate([kap_h, r_h], axis=1), jnp.concatenate([b_h, k_h], axis=1), _BNT)
    a_ab = jnp.where(stril, p[:, :c, :c], 0.0)
    a_ak = jnp.where(stril, p[:, :c, c:], 0.0)
    r_b = jnp.where(tril, p[:, c:, :c], 0.0)
    r_k = jnp.where(tril, p[:, c:, c:], 0.0)
    t = _tri_inverse(a_ab, row, col)
    x = _dot3(t, jnp.concatenate([kap_h, _dot3(a_ak, v, _BNN)], axis=2), _BNN)
    wr = jnp.concatenate([x[:, :, :A_HEAD], r_h], axis=1)
    rbk = jnp.concatenate([r_b, r_k], axis=2)
    btk = jnp.concatenate([b_t, k_t, jnp.where(eye, jnp.exp(cend), 0.0)], axis=1)

    pick = lambda a, ch: a.reshape(A_HEADS, nch, *a.shape[1:])[:, ch]
    m = state_ref[...]
    for ch in range(nch):
        wm = _dot3(pick(wr, ch), m, _BNN)
        u = -(wm[:, :c] + pick(x, ch)[:, :, A_HEAD:])
        uv = jnp.concatenate([u, pick(v, ch)], axis=1)
        y = wm[:, c:] + _dot3(pick(rbk, ch), uv, _BNN)
        m = _dot3(pick(btk, ch), jnp.concatenate([uv, m], axis=1), _BTN)
        ym = jnp.mean(y, axis=-1, keepdims=True)
        yc = y - ym
        yv = jnp.mean(yc * yc, axis=-1, keepdims=True)
        y_ref[:, ch * c:(ch + 1) * c, :] = yc * lax.rsqrt(yv + A_GN_EPS)
    state_ref[...] = m


RWKV_CHUNKS_PER_STEP = 2


def rwkv_chunks(r, lw, k, v, kap, b):
    _, s, _ = r.shape
    nch = RWKV_CHUNKS_PER_STEP
    spec = pl.BlockSpec((A_HEADS, nch * CHUNK, A_HEAD), lambda n: (0, n, 0))
    return pl.pallas_call(
        functools.partial(_rwkv_chunk_kernel, nch=nch),
        grid=(s // (nch * CHUNK),),
        in_specs=[spec] * 6,
        out_specs=spec,
        out_shape=jax.ShapeDtypeStruct((A_HEADS, s, A_HEAD), F32),
        scratch_shapes=[pltpu.VMEM((A_HEADS, A_HEAD, A_HEAD), F32)],
        compiler_params=_cparams("arbitrary"),
    )(r, lw, k, v, kap, b)


def _rwkv_post_kernel(y_ref, bonus_ref, g_ref, gg_ref, gb_ref, o_ref):
    y = jnp.concatenate([y_ref[h] for h in range(A_HEADS)], axis=1)
    o_ref[...] = ((y * gg_ref[...] + gb_ref[...] + bonus_ref[...]) * g_ref[...]).astype(o_ref.dtype)


def rwkv_post(y, bonus, g, gn_g, gn_b, *, tm):
    _, s, _ = y.shape
    row = lambda i: (i, 0)
    const = lambda i: (0, 0)
    return pl.pallas_call(
        _rwkv_post_kernel,
        grid=(s // tm,),
        in_specs=[pl.BlockSpec((A_HEADS, tm, A_HEAD), lambda i: (0, i, 0)),
                  pl.BlockSpec((tm, A_WIDTH), row), pl.BlockSpec((tm, A_WIDTH), row),
                  pl.BlockSpec((1, A_WIDTH), const), pl.BlockSpec((1, A_WIDTH), const)],
        out_specs=pl.BlockSpec((tm, A_WIDTH), row),
        out_shape=jax.ShapeDtypeStruct((s, A_WIDTH), BF16),
        compiler_params=_cparams("parallel"),
    )(y, bonus, g, gn_g.reshape(1, -1), gn_b.reshape(1, -1))


def _rglru_kernel(xb_ref, gate_ref, halo_ref, cw_ref, cb_ref, wr_ref, br_ref, wi_ref, bi_ref, lam_ref,
                  o_ref, xe_ref, a_ref, u_ref, h_ref, carry_ref, *, tm):
    i = pl.program_id(0)

    @pl.when(i == 0)
    def _():
        carry_ref[...] = jnp.zeros_like(carry_ref)

    xe_ref[0:HALO, :] = halo_ref[...] * (i > 0).astype(F32)
    xe_ref[HALO:, :] = xb_ref[...]
    xc = cb_ref[...] + cw_ref[B_CONV - 1:B_CONV, :] * xe_ref[HALO:, :]
    for d in range(1, B_CONV):
        xc = xc + cw_ref[B_CONV - 1 - d:B_CONV - d, :] * xe_ref[HALO - d:HALO - d + tm, :]
    xcb = xc.astype(BF16)
    r = jax.nn.sigmoid(_dot(xcb, wr_ref[...]) + br_ref[...])
    gi = jax.nn.sigmoid(_dot(xcb, wi_ref[...]) + bi_ref[...])
    log_a = -B_C * r * _softplus(-lam_ref[...])
    a = jnp.exp(log_a)
    a_ref[...] = a
    u_ref[...] = jnp.sqrt(-jnp.tanh(log_a) * (a * a + 1.0)) * (gi * xc)

    def group(gidx, h):
        base = pl.multiple_of(gidx * HALO, HALO)
        a8 = a_ref[pl.ds(base, HALO), :]
        u8 = u_ref[pl.ds(base, HALO), :]
        rows = []
        for rr in range(HALO):
            h = a8[rr:rr + 1, :] * h + u8[rr:rr + 1, :]
            rows.append(h)
        h_ref[pl.ds(base, HALO), :] = jnp.concatenate(rows, axis=0)
        return h

    carry_ref[...] = lax.fori_loop(0, tm // HALO, group, carry_ref[...])
    o_ref[...] = (jax.nn.gelu(gate_ref[...]) * h_ref[...]).astype(o_ref.dtype)


def rglru(z_b, conv_w, conv_b, wr, b_r, wi, b_i, lam, *, tm):
    s = z_b.shape[0]
    hb = tm // HALO
    row = lambda i: (i, 0)
    const = lambda i: (0, 0)
    vec = lambda a: a.reshape(1, -1)
    return pl.pallas_call(
        functools.partial(_rglru_kernel, tm=tm),
        grid=(s // tm,),
        in_specs=[pl.BlockSpec((tm, B_WIDTH), row), pl.BlockSpec((tm, B_WIDTH), lambda i: (i, 1)),
                  pl.BlockSpec((HALO, B_WIDTH), lambda i: (jnp.maximum(i * hb - 1, 0), 0)),
                  pl.BlockSpec((B_CONV, B_WIDTH), const), pl.BlockSpec((1, B_WIDTH), const),
                  pl.BlockSpec((B_WIDTH, B_WIDTH), const), pl.BlockSpec((1, B_WIDTH), const),
                  pl.BlockSpec((B_WIDTH, B_WIDTH), const), pl.BlockSpec((1, B_WIDTH), const),
                  pl.BlockSpec((1, B_WIDTH), const)],
        out_specs=pl.BlockSpec((tm, B_WIDTH), row),
        out_shape=jax.ShapeDtypeStruct((s, B_WIDTH), BF16),
        scratch_shapes=[pltpu.VMEM((tm + HALO, B_WIDTH), F32), pltpu.VMEM((tm, B_WIDTH), F32),
                        pltpu.VMEM((tm, B_WIDTH), F32), pltpu.VMEM((tm, B_WIDTH), F32),
                        pltpu.VMEM((1, B_WIDTH), F32)],
        compiler_params=_cparams("arbitrary"),
    )(z_b, z_b, z_b, conv_w, vec(conv_b), wr, vec(b_r), wi, vec(b_i), vec(lam))


ODD_PAD = 1024
ODD_KIDX_AT = 768
ODD_WIDX_AT = 896
QB = 256
SCORE_SCALE = (IDX_HEADS ** -0.5) * (IDX_DIM ** -0.5)
INT_MIN = -(2 ** 31)
CHUNK_SHIFT = 6
CUT_BITS = 14
COUNT_ROWS = 32
HEAD_GROUP = 2
FAR_TILES = 2


def _dsa_in_kernel(x_ref, w_ref, qn_ref, kvn_ref, kg_ref, kb_ref, cq_o, ckv_o, kidx_o, widx_o):
    acc = _dot(x_ref[...], w_ref[...])
    cq = acc[:, 0:C_Q_RANK]
    ckv = acc[:, C_Q_RANK:C_Q_RANK + C_KV_RANK]
    kidx = acc[:, ODD_KIDX_AT:ODD_KIDX_AT + IDX_DIM]
    widx = acc[:, ODD_WIDX_AT:ODD_WIDX_AT + IDX_HEADS]
    rms = lambda t, g: t * lax.rsqrt(jnp.mean(t * t, axis=-1, keepdims=True) + 1e-6) * g
    cq_o[...] = rms(cq, qn_ref[...]).astype(BF16)
    ckv_o[...] = rms(ckv, kvn_ref[...]).astype(BF16)
    kidx_o[...] = _layer_norm_rows(kidx, kg_ref[...], kb_ref[...], LN_EPS).astype(BF16)
    widx_o[...] = widx * SCORE_SCALE


def dsa_in(xb, w_pad, q_norm, kv_norm, kidx_g, kidx_b, *, tm):
    s, d = xb.shape
    row = lambda i: (i, 0)
    const = lambda i: (0, 0)
    vec = lambda a: a.reshape(1, -1)
    return pl.pallas_call(
        _dsa_in_kernel,
        grid=(s // tm,),
        in_specs=[pl.BlockSpec((tm, d), row), pl.BlockSpec((d, ODD_PAD), const),
                  pl.BlockSpec((1, C_Q_RANK), const), pl.BlockSpec((1, C_KV_RANK), const),
                  pl.BlockSpec((1, IDX_DIM), const), pl.BlockSpec((1, IDX_DIM), const)],
        out_specs=[pl.BlockSpec((tm, C_Q_RANK), row), pl.BlockSpec((tm, C_KV_RANK), row),
                   pl.BlockSpec((tm, IDX_DIM), row), pl.BlockSpec((tm, IDX_HEADS), row)],
        out_shape=[jax.ShapeDtypeStruct((s, C_Q_RANK), BF16), jax.ShapeDtypeStruct((s, C_KV_RANK), BF16),
                   jax.ShapeDtypeStruct((s, IDX_DIM), BF16), jax.ShapeDtypeStruct((s, IDX_HEADS), F32)],
        compiler_params=_cparams("parallel"),
    )(xb, w_pad, vec(q_norm), vec(kv_norm), vec(kidx_g), vec(kidx_b))


def _qabs_kernel(cq_ref, wuq_ref, wuk_ref, o_ref):
    q = _dot(cq_ref[...], wuq_ref[...]).astype(BF16)
    for h in range(C_HEADS):
        qa = _dot(q[:, h * C_HEAD_DIM:(h + 1) * C_HEAD_DIM], wuk_ref[h], _NT)
        o_ref[h] = (qa * (C_HEAD_DIM ** -0.5 * LOG2E)).astype(BF16)


def dsa_qabs(cq, w_uq, w_uk, *, tm):
    s = cq.shape[0]
    return pl.pallas_call(
        _qabs_kernel,
        grid=(s // tm,),
        in_specs=[pl.BlockSpec((tm, C_Q_RANK), lambda i: (i, 0)),
                  pl.BlockSpec((C_Q_RANK, C_HEADS * C_HEAD_DIM), lambda i: (0, 0)),
                  pl.BlockSpec((C_HEADS, C_KV_RANK, C_HEAD_DIM), lambda i: (0, 0, 0))],
        out_specs=pl.BlockSpec((C_HEADS, tm, C_KV_RANK), lambda i: (0, i, 0)),
        out_shape=jax.ShapeDtypeStruct((C_HEADS, s, C_KV_RANK), BF16),
        compiler_params=_cparams("parallel"),
    )(cq, w_uq, w_uk)


def _sortable_key(score):
    bits = lax.bitcast_convert_type(score + 0.0, jnp.int32)
    return jnp.where(bits < 0, bits ^ jnp.int32(0x7FFFFFFF), bits)


def _dsa_attn_kernel(qabs_ref, qidx_ref, widx_t_ref, kidx_ref, ckv_ref, wuv_ref, bias_ref,
                     o_ref, keys_ref, acc_ref, m_ref, l_ref, *, k_sel):
    i = pl.program_id(0)
    n_tiles = i + 1
    k_local = lax.broadcasted_iota(jnp.int32, (QB, QB), 0)
    q_local = lax.broadcasted_iota(jnp.int32, (QB, QB), 1)
    allowed_diag_t = (k_local >> CHUNK_SHIFT) <= (q_local >> CHUNK_SHIFT)

    widx_t = widx_t_ref[...]

    def score_tile(t, carry):
        off = pl.multiple_of(t * QB, QB)
        kt = kidx_ref[pl.ds(off, QB), :]
        sc = jnp.zeros((QB, QB), F32)
        for j in range(IDX_HEADS):
            d = _dot(kt, qidx_ref[:, j * IDX_DIM:(j + 1) * IDX_DIM], _NT)
            sc = sc + widx_t[j:j + 1, :] * jnp.maximum(d, 0.0)
        key = _sortable_key(sc)
        key = jnp.where(jnp.logical_or(t < i, allowed_diag_t), key, jnp.int32(INT_MIN))
        keys_ref[pl.ds(off, QB), :] = key
        return carry

    lax.fori_loop(0, n_tiles, score_tile, 0)

    def selected(kt, off, thr, cut):
        pos = lax.broadcasted_iota(jnp.int32, kt.shape, 0) + off
        return jnp.logical_or(kt > thr, jnp.logical_and(kt == thr, pos < cut))

    def count(pred):
        def body(t, acc):
            off = pl.multiple_of(t * QB, QB)
            hit = pred(keys_ref[pl.ds(off, QB), :], off).astype(jnp.int32)
            return acc + jnp.sum(hit.reshape(QB // COUNT_ROWS, COUNT_ROWS, QB), axis=0)
        acc = lax.fori_loop(0, n_tiles, body, jnp.zeros((COUNT_ROWS, QB), jnp.int32))
        return jnp.sum(acc.astype(F32), axis=0, keepdims=True).astype(jnp.int32)

    def thr_bit(it, thr):
        cand = thr + jnp.left_shift(jnp.int32(1), 31 - it)
        cnt = count(lambda kt, off: kt >= cand)
        return jnp.where(cnt >= k_sel, cand, thr)

    thr = lax.fori_loop(0, 32, thr_bit, jnp.full((1, QB), INT_MIN, jnp.int32))

    def cut_bit(it, cut):
        cand = cut + jnp.left_shift(jnp.int32(1), CUT_BITS - 1 - it)
        cnt = count(lambda kt, off: selected(kt, off, thr, cand))
        return jnp.where(cnt <= k_sel, cand, cut)

    n_ge = count(lambda kt, off: kt >= thr)
    tied = jnp.logical_and(n_ge > k_sel, thr > INT_MIN)
    any_tied = jnp.max(jnp.where(tied, 1.0, 0.0)) > 0.0
    cut = lax.cond(any_tied,
                   lambda: lax.fori_loop(0, CUT_BITS, cut_bit, jnp.zeros((1, QB), jnp.int32)),
                   lambda: jnp.full((1, QB), 2 ** CUT_BITS - 1, jnp.int32))
    rep_lanes = lambda a, n: jnp.concatenate([a] * (n // LANES), axis=-1)

    m_ref[...] = jnp.full(m_ref.shape, NEG_BIG, F32)
    l_ref[...] = jnp.zeros(l_ref.shape, F32)
    acc_ref[...] = jnp.zeros(acc_ref.shape, F32)

    def attend(off, width, near):
        sel = selected(keys_ref[pl.ds(off, width), :], off, thr, cut)
        if near == 1:
            sel = jnp.logical_and(sel, allowed_diag_t)
        mask_add = jnp.where(sel, 0.0, NEG_BIG).T[None]
        kv = ckv_ref[pl.ds(off, width), :]
        rep = lambda a, n: jnp.concatenate([a] * (n // LANES), axis=-1)

        for g in range(C_HEADS // HEAD_GROUP):
            hs = slice(g * HEAD_GROUP, (g + 1) * HEAD_GROUP)
            q = qabs_ref[hs].reshape(HEAD_GROUP * QB, C_KV_RANK)
            s = _dot(q, kv, _NT).reshape(HEAD_GROUP, QB, width) + mask_add
            if near is not None:
                s = s + bias_ref[near, hs]
            m_old = m_ref[hs]
            row_max = jnp.broadcast_to(jnp.max(s, axis=-1, keepdims=True), m_old.shape)
            m_new = jnp.maximum(m_old, row_max)
            alpha = jnp.exp2(m_old - m_new)
            p = jnp.exp2(s - rep(m_new, width))
            row_sum = jnp.broadcast_to(jnp.sum(p, axis=-1, keepdims=True), m_old.shape)
            l_ref[hs] = alpha * l_ref[hs] + row_sum
            pv = _dot(p.astype(BF16).reshape(HEAD_GROUP * QB, width), kv).reshape(HEAD_GROUP, QB, C_KV_RANK)
            acc_ref[hs] = rep(alpha, C_KV_RANK) * acc_ref[hs] + pv
            m_ref[hs] = m_new

    n_far = jnp.maximum(i - 1, 0)

    n_wide = n_far // FAR_TILES

    def far_wide(tw, carry):
        attend(pl.multiple_of(tw * (FAR_TILES * QB), FAR_TILES * QB), FAR_TILES * QB, None)
        return carry

    def far_single(t, carry):
        attend(pl.multiple_of(t * QB, QB), QB, None)
        return carry

    lax.fori_loop(0, n_wide, far_wide, 0)
    lax.fori_loop(n_wide * FAR_TILES, n_far, far_single, 0)

    @pl.when(i >= 1)
    def _():
        attend(pl.multiple_of((i - 1) * QB, QB), QB, 0)

    attend(pl.multiple_of(i * QB, QB), QB, 1)

    for h in range(C_HEADS):
        o_lat = (acc_ref[h] / rep_lanes(l_ref[h], C_KV_RANK)).astype(BF16)
        o_ref[:, h * C_HEAD_DIM:(h + 1) * C_HEAD_DIM] = _dot(o_lat, wuv_ref[h]).astype(o_ref.dtype)


def dsa_attention(qabs, qidx, widx_t, kidx, ckv, w_uv, bias_near, *, k_sel):
    _, s, _ = qabs.shape
    full2 = lambda i: (0, 0)
    once = pl.Buffered(1)
    return pl.pallas_call(
        functools.partial(_dsa_attn_kernel, k_sel=k_sel),
        grid=(s // QB,),
        in_specs=[pl.BlockSpec((C_HEADS, QB, C_KV_RANK), lambda i: (0, i, 0)),
                  pl.BlockSpec((QB, IDX_HEADS * IDX_DIM), lambda i: (i, 0)),
                  pl.BlockSpec((IDX_HEADS, QB), lambda i: (0, i)),
                  pl.BlockSpec((s, IDX_DIM), full2, pipeline_mode=once),
                  pl.BlockSpec((s, C_KV_RANK), full2, pipeline_mode=once),
                  pl.BlockSpec((C_HEADS, C_KV_RANK, C_HEAD_DIM), lambda i: (0, 0, 0), pipeline_mode=once),
                  pl.BlockSpec((2, C_HEADS, QB, QB), lambda i: (0, 0, 0, 0), pipeline_mode=once)],
        out_specs=pl.BlockSpec((QB, C_HEADS * C_HEAD_DIM), lambda i: (i, 0)),
        out_shape=jax.ShapeDtypeStruct((s, C_HEADS * C_HEAD_DIM), BF16),
        scratch_shapes=[pltpu.VMEM((s, QB), jnp.int32),
                        pltpu.VMEM((C_HEADS, QB, C_KV_RANK), F32),
                        pltpu.VMEM((C_HEADS, QB, LANES), F32),
                        pltpu.VMEM((C_HEADS, QB, LANES), F32)],
        compiler_params=_cparams("parallel"),
    )(qabs, qidx, widx_t, kidx, ckv, w_uv, bias_near)


def _t5_bucket(rel):
    nb = REL_BUCKETS // 2
    max_exact = nb // 2
    ret = jnp.where(rel > 0, nb, 0)
    n = jnp.abs(rel)
    nf = jnp.maximum(n, 1).astype(jnp.float32)
    large = max_exact + (jnp.log(nf / max_exact) / math.log(REL_MAX_DIST / max_exact) * (nb - max_exact)).astype(jnp.int32)
    large = jnp.minimum(large, nb - 1)
    return ret + jnp.where(n < max_exact, n, large)


def _near_bias(rel_bias):
    ql = jnp.arange(QB)[:, None]
    sl = jnp.arange(QB)[None, :]
    rel = jnp.stack([sl - ql - QB, sl - ql])
    far = rel_bias[_t5_bucket(jnp.array(-2 * QB))]
    table = ((rel_bias - far) * LOG2E).astype(F32)
    onehot = jax.nn.one_hot(_t5_bucket(rel), REL_BUCKETS, dtype=F32)
    return jnp.einsum('tqsb,bh->thqs', onehot, table, precision=lax.Precision.HIGHEST)


def _pad_rows(w, at, total):
    return jnp.zeros((total, w.shape[1]), w.dtype).at[at:at + w.shape[0]].set(w)


def _block_diag(w):
    n, d, e = w.shape
    eye = jnp.eye(n, dtype=w.dtype)
    return (eye[:, None, :, None] * w[:, :, None, :]).reshape(n * d, n * e)


def _even_mixer(x, xb, w_in, w_out, j, mu, w0, w2, a0, a2, g2, k_k, k_a, r_k, gn_g, gn_b,
                conv_w, conv_b, w_r, b_r, w_i, b_i, lam, ln_g, ln_b, tm):
    w_in = w_in.astype(BF16)
    n_rkv = 3 * A_WIDTH
    z_rkv = matmul(xb, w_in[:, :n_rkv], tm=tm, tn=1024)
    w_lo = jnp.pad(w_in[:, n_rkv:A_COLS], ((0, 0), (0, A_LORA_PAD - A_LORA)))
    z_lo = matmul(xb, w_lo, tm=tm, tn=A_LORA_PAD)
    z_b = matmul(xb, w_in[:, A_COLS:], tm=tm, tn=1024)

    mu_lo = jnp.pad(mu[n_rkv:], (0, A_LORA_PAD - A_LORA))
    w2p = _pad_rows(w2, 0, A_LORA_PAD).astype(BF16)
    a2p = _pad_rows(a2, A_DECAY_LORA, A_LORA_PAD).astype(BF16)
    g2p = _pad_rows(g2, A_DECAY_LORA + A_ICL_LORA, A_LORA_PAD).astype(BF16)
    r, lw, k2, v, kap, b, g, bonus = rwkv_prep(
        z_rkv, z_lo, mu[:n_rkv], mu_lo, w0, w2p, a0, a2p, g2p, k_k, k_a, r_k.reshape(-1), tm=min(tm, 256))
    y = rwkv_chunks(r, lw, k2, v, kap, b)
    y_a = rwkv_post(y, bonus, g, gn_g, gn_b, tm=tm)

    y_b = rglru(z_b, conv_w, conv_b, _block_diag(w_r).astype(BF16), b_r,
                _block_diag(w_i).astype(BF16), b_i, lam, tm=min(tm, 256))
    return matmul2_residual_ln(y_a, y_b, w_out, j, x, ln_g, ln_b, tm=tm)


def _odd_mixer(x, xb, w_in, w_out, j, q_norm, kv_norm, w_uq, w_uk, w_uv, w_qidx, kidx_g, kidx_b,
               bias_near, ln_g, ln_b, tm):
    s = x.shape[0]
    d = w_in.shape[0]
    n_qkv = C_Q_RANK + C_KV_RANK
    w_pad = jnp.zeros((d, ODD_PAD), BF16)
    w_pad = w_pad.at[:, :n_qkv].set(w_in[:, :n_qkv].astype(BF16))
    w_pad = w_pad.at[:, ODD_KIDX_AT:ODD_KIDX_AT + IDX_DIM].set(w_in[:, n_qkv:n_qkv + IDX_DIM].astype(BF16))
    w_pad = w_pad.at[:, ODD_WIDX_AT:ODD_WIDX_AT + IDX_HEADS].set(w_in[:, n_qkv + IDX_DIM:].astype(BF16))
    cq, ckv, kidx, widx = dsa_in(xb, w_pad, q_norm, kv_norm, kidx_g, kidx_b, tm=tm)
    qabs = dsa_qabs(cq, w_uq.astype(BF16), w_uk.astype(BF16), tm=tm)
    qidx = matmul(cq, w_qidx.astype(BF16), tm=tm, tn=IDX_HEADS * IDX_DIM, out_dtype=BF16)
    o = dsa_attention(qabs, qidx, widx.T, kidx, ckv, w_uv.astype(BF16), bias_near,
                      k_sel=min(TOPK_MAX, s // 4))
    return matmul_residual_ln(o, w_out, j, x, ln_g, ln_b, tm=tm, tk=o.shape[1])


def kernel(x, p, rel_bias, ln1_g, ln1_b, ln2_g, ln2_b, ffn_w_up, ffn_conv_w, ffn_conv_b, ffn_w_down, ple_w_proj, ple_w_gate, ev_w_in, ev_w_out, a_mu, a_w0, a_w2, a_a0, a_a2, a_g2, a_k_k, a_k_a, a_r_k, a_gn_g, a_gn_b, b_conv_w, b_conv_b, b_w_r, b_b_r, b_w_i, b_b_i, b_lambda, od_w_in, od_w_out, c_q_norm, c_kv_norm, c_w_uq, c_w_uk, c_w_uv, c_w_qidx, c_kidx_g, c_kidx_b):
    bsz, s, d = x.shape
    assert bsz == 1 and s % QB == 0 and s <= 2 ** CUT_BITS
    tm = min(512, s)
    x = x[0]
    xb = x.astype(BF16)
    bias_near = _near_bias(rel_bias)
    ev_w_out_b, od_w_out_b = ev_w_out.astype(BF16), od_w_out.astype(BF16)
    w_up_b, w_down_b = ffn_w_up.astype(BF16), ffn_w_down.astype(BF16)
    w_gate_b, w_proj_b, p_b = ple_w_gate.astype(BF16), ple_w_proj.astype(BF16), p.astype(BF16)
    for layer in range(DEPTH):
        j = layer // 2
        if layer % 2 == 0:
            x, xb = _even_mixer(x, xb, ev_w_in[j], ev_w_out_b, j, a_mu[j], a_w0[j], a_w2[j], a_a0[j], a_a2[j],
                                a_g2[j], a_k_k[j], a_k_a[j], a_r_k[j], a_gn_g[j], a_gn_b[j],
                                b_conv_w[j], b_conv_b[j], b_w_r[j], b_b_r[j], b_w_i[j], b_b_i[j], b_lambda[j],
                                ln1_g[layer], ln1_b[layer], tm)
        else:
            x, xb = _odd_mixer(x, xb, od_w_in[j], od_w_out_b, j, c_q_norm[j], c_kv_norm[j], c_w_uq[j], c_w_uk[j],
                               c_w_uv[j], c_w_qidx[j], c_kidx_g[j], c_kidx_b[j], bias_near,
                               ln1_g[layer], ln1_b[layer], tm)
        hm = ffn_up(xb, w_up_b, layer, ffn_conv_w[layer], ffn_conv_b[layer], tm=min(4 * tm, s), sub=min(2 * tm, s), tc=512)
        x, xb = matmul_residual_ln(hm, w_down_b, layer, x, ln2_g[layer], ln2_b[layer], tm=tm, tk=D_FF // 4)
        x, xb = ple_update(xb, w_gate_b, p_b, w_proj_b, layer, x, tm=min(2 * tm, s), tn=1024)
    return x[None]
```

```python
import functools
import math

import jax
import jax.numpy as jnp
from jax import lax
from jax.experimental import pallas as pl
from jax.experimental.pallas import tpu as pltpu

F32 = jnp.float32
BF16 = jnp.bfloat16

D_MODEL = 2048
DEPTH = 4
CHUNK = 64
DN_ALPHA = (2 * DEPTH) ** 0.25
LN_EPS = 1e-5
A_WIDTH = 1024
A_HEAD = 64
A_HEADS = 16
A_DECAY_LORA = 64
A_ICL_LORA = 64
A_GATE_LORA = 160
A_LORA = A_DECAY_LORA + A_ICL_LORA + A_GATE_LORA
A_LORA_PAD = 384
A_COLS = 3 * A_WIDTH + A_LORA
A_GN_EPS = 64e-5
B_WIDTH = 1024
B_BLOCKS = 16
B_BLOCK = 64
B_CONV = 4
B_C = 8.0
C_HEADS = 16
C_HEAD_DIM = 128
C_Q_RANK = 512
C_KV_RANK = 256
IDX_HEADS = 16
IDX_DIM = 64
TOPK_MAX = 256
REL_BUCKETS = 32
REL_MAX_DIST = 128
D_FF = 5632
FFN_CONV = 3
PLE_DIM = 256

VMEM_LIMIT_BYTES = 56 * 1024 * 1024
NEG_BIG = -1e30
HALO = 8
LANES = 128
LOG2E = math.log2(math.e)


def _cparams(*sem):
    return pltpu.CompilerParams(dimension_semantics=sem, vmem_limit_bytes=VMEM_LIMIT_BYTES)


def _split_bf16(a):
    hi = a.astype(BF16)
    lo = (a - hi.astype(F32)).astype(BF16)
    return hi, lo


def _dot(a, b, dims=(((1,), (0,)), ((), ()))):
    return lax.dot_general(a, b, dims, preferred_element_type=F32)


_NT = (((1,), (1,)), ((), ()))
_TN = (((0,), (0,)), ((), ()))
_NN = (((1,), (0,)), ((), ()))


def _dot3(a, b, dims=_NN):
    ah, al = _split_bf16(a)
    bh, bl = _split_bf16(b)
    return _dot(ah, bh, dims) + (_dot(ah, bl, dims) + _dot(al, bh, dims))


def _dot_exact_rhs(a, b_bf16, dims=_NN):
    hi = a.astype(BF16)
    r1 = a - hi.astype(F32)
    mid = r1.astype(BF16)
    lo = (r1 - mid.astype(F32)).astype(BF16)
    return _dot(hi, b_bf16, dims) + (_dot(mid, b_bf16, dims) + _dot(lo, b_bf16, dims))


def _layer_norm_rows(v, g, b, eps):
    mu = jnp.mean(v, axis=-1, keepdims=True)
    d = v - mu
    var = jnp.mean(d * d, axis=-1, keepdims=True)
    return d * lax.rsqrt(var + eps) * g + b


def _softplus(x):
    return jnp.maximum(x, 0.0) + jnp.log1p(jnp.exp(-jnp.abs(x)))


def _mm_kernel(a_ref, b_ref, o_ref):
    o_ref[...] = _dot(a_ref[...], b_ref[...]).astype(o_ref.dtype)


def matmul(a, b, *, tm, tn, out_dtype=F32):
    m, k = a.shape
    _, n = b.shape
    assert m % tm == 0 and n % tn == 0
    return pl.pallas_call(
        _mm_kernel,
        grid=(n // tn, m // tm),
        in_specs=[pl.BlockSpec((tm, k), lambda j, i: (i, 0)),
                  pl.BlockSpec((k, tn), lambda j, i: (0, j))],
        out_specs=pl.BlockSpec((tm, tn), lambda j, i: (i, j)),
        out_shape=jax.ShapeDtypeStruct((m, n), out_dtype),
        compiler_params=_cparams("parallel", "parallel"),
    )(a, b)


def _mm1_ln_kernel(a_ref, b_ref, x_ref, g_ref, beta_ref, o_ref, ob_ref):
    y = _layer_norm_rows(DN_ALPHA * x_ref[...] + _dot(a_ref[...], b_ref[...]), g_ref[...], beta_ref[...], LN_EPS)
    o_ref[...] = y
    ob_ref[...] = y.astype(BF16)


def _mm_ln_kernel(a_ref, b_ref, x_ref, g_ref, beta_ref, o_ref, ob_ref, acc_ref, *, nk):
    kk = pl.program_id(1)

    @pl.when(kk == 0)
    def _():
        acc_ref[...] = jnp.zeros_like(acc_ref)

    acc_ref[...] += _dot(a_ref[...], b_ref[...])

    @pl.when(kk == nk - 1)
    def _():
        v = DN_ALPHA * x_ref[...] + acc_ref[...]
        y = _layer_norm_rows(v, g_ref[...], beta_ref[...], LN_EPS)
        o_ref[...] = y
        ob_ref[...] = y.astype(BF16)


def matmul_residual_ln(a, b, layer, x, g, beta, *, tm, tk):
    m, k = a.shape
    n = b.shape[2]
    assert m % tm == 0 and k % tk == 0
    nk = k // tk
    single = nk == 1
    return pl.pallas_call(
        _mm1_ln_kernel if single else functools.partial(_mm_ln_kernel, nk=nk),
        grid=(m // tm, nk),
        in_specs=[pl.BlockSpec((tm, tk), lambda i, kk: (i, kk)),
                  pl.BlockSpec((None, tk, n), lambda i, kk: (layer, kk, 0)),
                  pl.BlockSpec((tm, n), lambda i, kk: (i, 0)),
                  pl.BlockSpec((1, n), lambda i, kk: (0, 0)),
                  pl.BlockSpec((1, n), lambda i, kk: (0, 0))],
        out_specs=[pl.BlockSpec((tm, n), lambda i, kk: (i, 0)),
                   pl.BlockSpec((tm, n), lambda i, kk: (i, 0))],
        out_shape=[jax.ShapeDtypeStruct((m, n), F32), jax.ShapeDtypeStruct((m, n), BF16)],
        scratch_shapes=[] if single else [pltpu.VMEM((tm, n), F32)],
        compiler_params=_cparams("parallel", "arbitrary"),
    )(a, b, x, g.reshape(1, n), beta.reshape(1, n))


def _mm2_ln_kernel(a1_ref, a2_ref, b1_ref, b2_ref, x_ref, g_ref, beta_ref, o_ref, ob_ref):
    acc = _dot(a1_ref[...], b1_ref[...]) + _dot(a2_ref[...], b2_ref[...])
    y = _layer_norm_rows(DN_ALPHA * x_ref[...] + acc, g_ref[...], beta_ref[...], LN_EPS)
    o_ref[...] = y
    ob_ref[...] = y.astype(BF16)


def matmul2_residual_ln(a1, a2, b, layer, x, g, beta, *, tm):
    m, k1 = a1.shape
    k2 = a2.shape[1]
    n = b.shape[2]
    assert k1 == k2
    row = lambda i: (i, 0)
    const = lambda i: (0, 0)
    return pl.pallas_call(
        _mm2_ln_kernel,
        grid=(m // tm,),
        in_specs=[pl.BlockSpec((tm, k1), row), pl.BlockSpec((tm, k2), row),
                  pl.BlockSpec((None, k1, n), lambda i: (layer, 0, 0)),
                  pl.BlockSpec((None, k2, n), lambda i: (layer, 1, 0)),
                  pl.BlockSpec((tm, n), row), pl.BlockSpec((1, n), const), pl.BlockSpec((1, n), const)],
        out_specs=[pl.BlockSpec((tm, n), row), pl.BlockSpec((tm, n), row)],
        out_shape=[jax.ShapeDtypeStruct((m, n), F32), jax.ShapeDtypeStruct((m, n), BF16)],
        compiler_params=_cparams("parallel"),
    )(a1, a2, b, b, x, g.reshape(1, n), beta.reshape(1, n))


def _ple_kernel(xb_ref, wg_ref, p_ref, wp_ref, x_ref, o_ref, ob_ref):
    gate = jax.nn.sigmoid(_dot(xb_ref[...], wg_ref[...]))
    proj = _dot(p_ref[...], wp_ref[...])
    y = x_ref[...] + gate * proj
    o_ref[...] = y
    ob_ref[...] = y.astype(BF16)


def ple_update(xb, wg, pb, wp, layer, x, *, tm, tn):
    m, d = xb.shape
    pd = pb.shape[-1]
    return pl.pallas_call(
        _ple_kernel,
        grid=(d // tn, m // tm),
        in_specs=[pl.BlockSpec((tm, d), lambda j, i: (i, 0)),
                  pl.BlockSpec((None, d, tn), lambda j, i: (layer, 0, j)),
                  pl.BlockSpec((None, None, tm, pd), lambda j, i: (layer, 0, i, 0)),
                  pl.BlockSpec((None, pd, tn), lambda j, i: (layer, 0, j)),
                  pl.BlockSpec((tm, tn), lambda j, i: (i, j))],
        out_specs=[pl.BlockSpec((tm, tn), lambda j, i: (i, j)),
                   pl.BlockSpec((tm, tn), lambda j, i: (i, j))],
        out_shape=[jax.ShapeDtypeStruct((m, d), F32), jax.ShapeDtypeStruct((m, d), BF16)],
        compiler_params=_cparams("parallel", "parallel"),
    )(xb, wg, pb, wp, x)


def _ffn_up_kernel(x_ref, wg_ref, wu_ref, cwg_ref, cwu_ref, cbg_ref, cbu_ref, o_ref, eg_ref, eu_ref, *, tm, sub):
    i = pl.program_id(1)

    @pl.when(i == 0)
    def _():
        eg_ref[0:HALO, :] = jnp.zeros((HALO, eg_ref.shape[1]), F32)
        eu_ref[0:HALO, :] = jnp.zeros((HALO, eu_ref.shape[1]), F32)

    @pl.when(i > 0)
    def _():
        eg_ref[0:HALO, :] = eg_ref[tm:tm + HALO, :]
        eu_ref[0:HALO, :] = eu_ref[tm:tm + HALO, :]

    tc = o_ref.shape[1]
    halves = ((0, tc // 2), (tc // 2, tc))

    def project(k, e_ref, w_ref):
        e_ref[HALO + k * sub:HALO + (k + 1) * sub, :] = _dot(x_ref[k * sub:(k + 1) * sub, :], w_ref[...])

    def conv(e_ref, w_ref, b_ref, base, c0, c1):
        acc = b_ref[:, c0:c1] + w_ref[FFN_CONV - 1:FFN_CONV, c0:c1] * e_ref[base:base + sub, c0:c1]
        for d in range(1, FFN_CONV):
            acc = acc + w_ref[FFN_CONV - 1 - d:FFN_CONV - d, c0:c1] * e_ref[base - d:base - d + sub, c0:c1]
        return acc

    def gate_rows(k, c0, c1):
        base = HALO + k * sub
        gate = conv(eg_ref, cwg_ref, cbg_ref, base, c0, c1)
        up = conv(eu_ref, cwu_ref, cbu_ref, base, c0, c1)
        o_ref[k * sub:(k + 1) * sub, c0:c1] = (jax.nn.gelu(gate) * up).astype(o_ref.dtype)

    project(0, eg_ref, wg_ref)
    project(0, eu_ref, wu_ref)
    for k in range(tm // sub):
        more = (k + 1) * sub < tm
        if more:
            project(k + 1, eg_ref, wg_ref)
        gate_rows(k, *halves[0])
        if more:
            project(k + 1, eu_ref, wu_ref)
        gate_rows(k, *halves[1])


def ffn_up(xb, w_up, layer, conv_w, conv_b, *, tm, sub, tc):
    s, d = xb.shape
    two_ff = w_up.shape[2]
    ff = two_ff // 2
    nc = ff // tc
    cb = conv_b.reshape(1, two_ff)
    return pl.pallas_call(
        functools.partial(_ffn_up_kernel, tm=tm, sub=sub),
        grid=(nc, s // tm),
        in_specs=[pl.BlockSpec((tm, d), lambda j, i: (i, 0)),
                  pl.BlockSpec((None, d, tc), lambda j, i: (layer, 0, j)),
                  pl.BlockSpec((None, d, tc), lambda j, i: (layer, 0, j + nc)),
                  pl.BlockSpec((FFN_CONV, tc), lambda j, i: (0, j)),
                  pl.BlockSpec((FFN_CONV, tc), lambda j, i: (0, j + nc)),
                  pl.BlockSpec((1, tc), lambda j, i: (0, j)),
                  pl.BlockSpec((1, tc), lambda j, i: (0, j + nc))],
        out_specs=pl.BlockSpec((tm, tc), lambda j, i: (i, j)),
        out_shape=jax.ShapeDtypeStruct((s, ff), BF16),
        scratch_shapes=[pltpu.VMEM((tm + HALO, tc), F32), pltpu.VMEM((tm + HALO, tc), F32)],
        compiler_params=_cparams("parallel", "arbitrary"),
    )(xb, w_up, w_up, conv_w, conv_w, cb, cb)


def _shift_mix(z, halo, mu, live):
    prev = pltpu.roll(z, 1, 0)
    row0 = lax.broadcasted_iota(jnp.int32, z.shape, 0) == 0
    prev = jnp.where(row0, halo[HALO - 1:HALO, :] * live, prev)
    return z + (prev - z) * mu


def _rwkv_prep_kernel(z_ref, zh_ref, lo_ref, loh_ref, mu_ref, mulo_ref, w0_ref, w2_ref, a0_ref, a2_ref,
                      g2_ref, kk_ref, ka_ref, rk_ref, hsum_ref, hspread_ref,
                      r_o, lw_o, k_o, v_o, kap_o, b_o, g_o, bonus_o):
    live = (pl.program_id(0) > 0).astype(F32)
    z = _shift_mix(z_ref[...], zh_ref[...], mu_ref[...], live)
    lo = _shift_mix(lo_ref[...], loh_ref[...], mulo_ref[...], live)
    r = z[:, 0:A_WIDTH]
    k = z[:, A_WIDTH:2 * A_WIDTH]
    v = z[:, 2 * A_WIDTH:3 * A_WIDTH]
    w = -_softplus(-(w0_ref[...] + _dot(jnp.tanh(lo).astype(BF16), w2_ref[...]))) - 0.5
    lw = -jnp.exp(w)
    a = jax.nn.sigmoid(a0_ref[...] + _dot(lo.astype(BF16), a2_ref[...]))
    g = _dot(jax.nn.sigmoid(lo).astype(BF16), g2_ref[...])
    kk = k * kk_ref[...]
    head_sum = lambda t: _dot_exact_rhs(_dot_exact_rhs(t, hsum_ref[...]), hspread_ref[...])
    kap = kk / jnp.maximum(jnp.sqrt(head_sum(kk * kk)), 1e-12)
    k2 = k * (1.0 + (a - 1.0) * ka_ref[...])
    bonus = head_sum(r * k2 * rk_ref[...]) * v
    b = kap * a
    g_o[...] = g
    bonus_o[...] = bonus
    for h in range(A_HEADS):
        sl = slice(h * A_HEAD, (h + 1) * A_HEAD)
        r_o[h] = r[:, sl]
        lw_o[h] = lw[:, sl]
        k_o[h] = k2[:, sl]
        v_o[h] = v[:, sl]
        kap_o[h] = kap[:, sl]
        b_o[h] = b[:, sl]


def rwkv_prep(z_rkv, z_lo, mu_rkv, mu_lo, w0, w2p, a0, a2p, g2p, k_k, k_a, r_k, *, tm):
    s = z_rkv.shape[0]
    lane_head = jnp.arange(A_WIDTH)[:, None] // A_HEAD
    hsum = (lane_head == jnp.arange(LANES)[None, :]).astype(BF16)
    hspread = hsum.T
    hb = tm // HALO
    row = lambda i: (i, 0)
    halo = lambda i: (jnp.maximum(i * hb - 1, 0), 0)
    const = lambda i: (0, 0)
    hm = jax.ShapeDtypeStruct((A_HEADS, s, A_HEAD), F32)
    hm_spec = pl.BlockSpec((A_HEADS, tm, A_HEAD), lambda i: (0, i, 0))
    full = jax.ShapeDtypeStruct((s, A_WIDTH), F32)
    vec = lambda a: a.reshape(1, -1)
    return pl.pallas_call(
        _rwkv_prep_kernel,
        grid=(s // tm,),
        in_specs=[pl.BlockSpec((tm, 3 * A_WIDTH), row), pl.BlockSpec((HALO, 3 * A_WIDTH), halo),
                  pl.BlockSpec((tm, A_LORA_PAD), row), pl.BlockSpec((HALO, A_LORA_PAD), halo),
                  pl.BlockSpec((1, 3 * A_WIDTH), const), pl.BlockSpec((1, A_LORA_PAD), const),
                  pl.BlockSpec((1, A_WIDTH), const), pl.BlockSpec((A_LORA_PAD, A_WIDTH), const),
                  pl.BlockSpec((1, A_WIDTH), const), pl.BlockSpec((A_LORA_PAD, A_WIDTH), const),
                  pl.BlockSpec((A_LORA_PAD, A_WIDTH), const),
                  pl.BlockSpec((1, A_WIDTH), const), pl.BlockSpec((1, A_WIDTH), const),
                  pl.BlockSpec((1, A_WIDTH), const), pl.BlockSpec((A_WIDTH, LANES), const),
                  pl.BlockSpec((LANES, A_WIDTH), const)],
        out_specs=[hm_spec] * 6 + [pl.BlockSpec((tm, A_WIDTH), row)] * 2,
        out_shape=[hm] * 6 + [full] * 2,
        compiler_params=_cparams("parallel"),
    )(z_rkv, z_rkv, z_lo, z_lo, vec(mu_rkv), vec(mu_lo), vec(w0), w2p, vec(a0), a2p, g2p,
      vec(k_k), vec(k_a), vec(r_k), hsum, hspread)


_BNN = (((2,), (1,)), ((0,), (0,)))
_BNT = (((2,), (2,)), ((0,), (0,)))
_BTN = (((1,), (1,)), ((0,), (0,)))


def _tri_inverse(a, row, col):
    mm1 = lambda p, q: _dot(p.astype(BF16), q.astype(BF16), _BNN)
    mm3 = lambda p, q: _dot3(p, q, _BNN)
    eye = (row == col).astype(F32)
    ad = jnp.where((row >> 3) == (col >> 3), a, 0.0)
    t = eye - ad
    a2 = mm1(ad, ad)
    t = t + mm1(t, a2)
    a4 = mm1(a2, a2)
    t = t + mm1(t, a4)
    for sh, mm in ((3, mm1), (4, mm3), (5, mm3)):
        inner = (row >> sh) == (col >> sh)
        outer = (row >> (sh + 1)) == (col >> (sh + 1))
        aoff = jnp.where(jnp.logical_and(outer, jnp.logical_not(inner)), a, 0.0)
        t = t - mm(mm(t, aoff), t)
    return t


def _rwkv_chunk_kernel(r_ref, lw_ref, k_ref, v_ref, kap_ref, b_ref, y_ref, state_ref, *, nch):
    c = CHUNK
    nb = A_HEADS * nch

    @pl.when(pl.program_id(0) == 0)
    def _():
        state_ref[...] = jnp.zeros_like(state_ref)

    row = lax.broadcasted_iota(jnp.int32, (1, c, c), 1)
    col = lax.broadcasted_iota(jnp.int32, (1, c, c), 2)
    tril = row >= col
    stril = row > col
    eye = row == col
    lower_ones = jnp.broadcast_to(tril.astype(BF16), (nb, c, c))

    load = lambda ref: ref[...].reshape(nb, c, A_HEAD)
    r, lw, k, v, kap, b = load(r_ref), load(lw_ref), load(k_ref), load(v_ref), load(kap_ref), load(b_ref)
    hi = lw.astype(BF16)
    r1 = lw - hi.astype(F32)
    mid = r1.astype(BF16)
    lo = (r1 - mid.astype(F32)).astype(BF16)
    ci = _dot(lower_ones, hi, _BNN) + (_dot(lower_ones, mid, _BNN) + _dot(lower_ones, lo, _BNN))
    ce = ci - lw
    cend = ci[:, c - 1:c, :]
    gn = jnp.exp(-ci)
    gend = jnp.exp(cend - ci)
    kap_h = kap * jnp.exp(ce)
    r_h = r * jnp.exp(ci)
    b_h = b * gn
    k_h = k * gn
    b_t = b * gend
    k_t = k * gend
    p = _dot3(jnp.concatenate([kap_h, r_h], axis=1), jnp.concatenate([b_h, k_h], axis=1), _BNT)
    a_ab = jnp.where(stril, p[:, :c, :c], 0.0)
    a_ak = jnp.where(stril, p[:, :c, c:], 0.0)
    r_b = jnp.where(tril, p[:, c:, :c], 0.0)
    r_k = jnp.where(tril, p[:, c:, c:], 0.0)
    t = _tri_inverse(a_ab, row, col)
    x = _dot3(t, jnp.concatenate([kap_h, _dot3(a_ak, v, _BNN)], axis=2), _BNN)
    wr = jnp.concatenate([x[:, :, :A_HEAD], r_h], axis=1)
    rbk = jnp.concatenate([r_b, r_k], axis=2)
    btk = jnp.concatenate([b_t, k_t, jnp.where(eye, jnp.exp(cend), 0.0)], axis=1)

    pick = lambda a, ch: a.reshape(A_HEADS, nch, *a.shape[1:])[:, ch]
    m = state_ref[...]
    for ch in range(nch):
        wm = _dot3(pick(wr, ch), m, _BNN)
        u = -(wm[:, :c] + pick(x, ch)[:, :, A_HEAD:])
        uv = jnp.concatenate([u, pick(v, ch)], axis=1)
        y = wm[:, c:] + _dot3(pick(rbk, ch), uv, _BNN)
        m = _dot3(pick(btk, ch), jnp.concatenate([uv, m], axis=1), _BTN)
        ym = jnp.mean(y, axis=-1, keepdims=True)
        yc = y - ym
        yv = jnp.mean(yc * yc, axis=-1, keepdims=True)
        y_ref[:, ch * c:(ch + 1) * c, :] = yc * lax.rsqrt(yv + A_GN_EPS)
    state_ref[...] = m


RWKV_CHUNKS_PER_STEP = 2


def rwkv_chunks(r, lw, k, v, kap, b):
    _, s, _ = r.shape
    nch = RWKV_CHUNKS_PER_STEP
    spec = pl.BlockSpec((A_HEADS, nch * CHUNK, A_HEAD), lambda n: (0, n, 0))
    return pl.pallas_call(
        functools.partial(_rwkv_chunk_kernel, nch=nch),
        grid=(s // (nch * CHUNK),),
        in_specs=[spec] * 6,
        out_specs=spec,
        out_shape=jax.ShapeDtypeStruct((A_HEADS, s, A_HEAD), F32),
        scratch_shapes=[pltpu.VMEM((A_HEADS, A_HEAD, A_HEAD), F32)],
        compiler_params=_cparams("arbitrary"),
    )(r, lw, k, v, kap, b)


def _rwkv_post_kernel(y_ref, bonus_ref, g_ref, gg_ref, gb_ref, o_ref):
    y = jnp.concatenate([y_ref[h] for h in range(A_HEADS)], axis=1)
    o_ref[...] = ((y * gg_ref[...] + gb_ref[...] + bonus_ref[...]) * g_ref[...]).astype(o_ref.dtype)


def rwkv_post(y, bonus, g, gn_g, gn_b, *, tm):
    _, s, _ = y.shape
    row = lambda i: (i, 0)
    const = lambda i: (0, 0)
    return pl.pallas_call(
        _rwkv_post_kernel,
        grid=(s // tm,),
        in_specs=[pl.BlockSpec((A_HEADS, tm, A_HEAD), lambda i: (0, i, 0)),
                  pl.BlockSpec((tm, A_WIDTH), row), pl.BlockSpec((tm, A_WIDTH), row),
                  pl.BlockSpec((1, A_WIDTH), const), pl.BlockSpec((1, A_WIDTH), const)],
        out_specs=pl.BlockSpec((tm, A_WIDTH), row),
        out_shape=jax.ShapeDtypeStruct((s, A_WIDTH), BF16),
        compiler_params=_cparams("parallel"),
    )(y, bonus, g, gn_g.reshape(1, -1), gn_b.reshape(1, -1))


def _rglru_kernel(xb_ref, gate_ref, halo_ref, cw_ref, cb_ref, wr_ref, br_ref, wi_ref, bi_ref, lam_ref,
                  o_ref, xe_ref, a_ref, u_ref, h_ref, carry_ref, *, tm):
    i = pl.program_id(0)

    @pl.when(i == 0)
    def _():
        carry_ref[...] = jnp.zeros_like(carry_ref)

    xe_ref[0:HALO, :] = halo_ref[...] * (i > 0).astype(F32)
    xe_ref[HALO:, :] = xb_ref[...]
    xc = cb_ref[...] + cw_ref[B_CONV - 1:B_CONV, :] * xe_ref[HALO:, :]
    for d in range(1, B_CONV):
        xc = xc + cw_ref[B_CONV - 1 - d:B_CONV - d, :] * xe_ref[HALO - d:HALO - d + tm, :]
    xcb = xc.astype(BF16)
    r = jax.nn.sigmoid(_dot(xcb, wr_ref[...]) + br_ref[...])
    gi = jax.nn.sigmoid(_dot(xcb, wi_ref[...]) + bi_ref[...])
    log_a = -B_C * r * _softplus(-lam_ref[...])
    a = jnp.exp(log_a)
    a_ref[...] = a
    u_ref[...] = jnp.sqrt(-jnp.tanh(log_a) * (a * a + 1.0)) * (gi * xc)

    def group(gidx, h):
        base = pl.multiple_of(gidx * HALO, HALO)
        a8 = a_ref[pl.ds(base, HALO), :]
        u8 = u_ref[pl.ds(base, HALO), :]
        rows = []
        for rr in range(HALO):
            h = a8[rr:rr + 1, :] * h + u8[rr:rr + 1, :]
            rows.append(h)
        h_ref[pl.ds(base, HALO), :] = jnp.concatenate(rows, axis=0)
        return h

    carry_ref[...] = lax.fori_loop(0, tm // HALO, group, carry_ref[...])
    o_ref[...] = (jax.nn.gelu(gate_ref[...]) * h_ref[...]).astype(o_ref.dtype)


def rglru(z_b, conv_w, conv_b, wr, b_r, wi, b_i, lam, *, tm):
    s = z_b.shape[0]
    hb = tm // HALO
    row = lambda i: (i, 0)
    const = lambda i: (0, 0)
    vec = lambda a: a.reshape(1, -1)
    return pl.pallas_call(
        functools.partial(_rglru_kernel, tm=tm),
        grid=(s // tm,),
        in_specs=[pl.BlockSpec((tm, B_WIDTH), row), pl.BlockSpec((tm, B_WIDTH), lambda i: (i, 1)),
                  pl.BlockSpec((HALO, B_WIDTH), lambda i: (jnp.maximum(i * hb - 1, 0), 0)),
                  pl.BlockSpec((B_CONV, B_WIDTH), const), pl.BlockSpec((1, B_WIDTH), const),
                  pl.BlockSpec((B_WIDTH, B_WIDTH), const), pl.BlockSpec((1, B_WIDTH), const),
                  pl.BlockSpec((B_WIDTH, B_WIDTH), const), pl.BlockSpec((1, B_WIDTH), const),
                  pl.BlockSpec((1, B_WIDTH), const)],
        out_specs=pl.BlockSpec((tm, B_WIDTH), row),
        out_shape=jax.ShapeDtypeStruct((s, B_WIDTH), BF16),
        scratch_shapes=[pltpu.VMEM((tm + HALO, B_WIDTH), F32), pltpu.VMEM((tm, B_WIDTH), F32),
                        pltpu.VMEM((tm, B_WIDTH), F32), pltpu.VMEM((tm, B_WIDTH), F32),
                        pltpu.VMEM((1, B_WIDTH), F32)],
        compiler_params=_cparams("arbitrary"),
    )(z_b, z_b, z_b, conv_w, vec(conv_b), wr, vec(b_r), wi, vec(b_i), vec(lam))


ODD_PAD = 1024
ODD_KIDX_AT = 768
ODD_WIDX_AT = 896
QB = 256
SCORE_SCALE = (IDX_HEADS ** -0.5) * (IDX_DIM ** -0.5)
INT_MIN = -(2 ** 31)
CHUNK_SHIFT = 6
CUT_BITS = 14
COUNT_ROWS = 32
HEAD_GROUP = 2
FAR_TILES = 2


def _dsa_in_kernel(x_ref, w_ref, qn_ref, kvn_ref, kg_ref, kb_ref, cq_o, ckv_o, kidx_o, widx_o):
    acc = _dot(x_ref[...], w_ref[...])
    cq = acc[:, 0:C_Q_RANK]
    ckv = acc[:, C_Q_RANK:C_Q_RANK + C_KV_RANK]
    kidx = acc[:, ODD_KIDX_AT:ODD_KIDX_AT + IDX_DIM]
    widx = acc[:, ODD_WIDX_AT:ODD_WIDX_AT + IDX_HEADS]
    rms = lambda t, g: t * lax.rsqrt(jnp.mean(t * t, axis=-1, keepdims=True) + 1e-6) * g
    cq_o[...] = rms(cq, qn_ref[...]).astype(BF16)
    ckv_o[...] = rms(ckv, kvn_ref[...]).astype(BF16)
    kidx_o[...] = _layer_norm_rows(kidx, kg_ref[...], kb_ref[...], LN_EPS).astype(BF16)
    widx_o[...] = widx * SCORE_SCALE


def dsa_in(xb, w_pad, q_norm, kv_norm, kidx_g, kidx_b, *, tm):
    s, d = xb.shape
    row = lambda i: (i, 0)
    const = lambda i: (0, 0)
    vec = lambda a: a.reshape(1, -1)
    return pl.pallas_call(
        _dsa_in_kernel,
        grid=(s // tm,),
        in_specs=[pl.BlockSpec((tm, d), row), pl.BlockSpec((d, ODD_PAD), const),
                  pl.BlockSpec((1, C_Q_RANK), const), pl.BlockSpec((1, C_KV_RANK), const),
                  pl.BlockSpec((1, IDX_DIM), const), pl.BlockSpec((1, IDX_DIM), const)],
        out_specs=[pl.BlockSpec((tm, C_Q_RANK), row), pl.BlockSpec((tm, C_KV_RANK), row),
                   pl.BlockSpec((tm, IDX_DIM), row), pl.BlockSpec((tm, IDX_HEADS), row)],
        out_shape=[jax.ShapeDtypeStruct((s, C_Q_RANK), BF16), jax.ShapeDtypeStruct((s, C_KV_RANK), BF16),
                   jax.ShapeDtypeStruct((s, IDX_DIM), BF16), jax.ShapeDtypeStruct((s, IDX_HEADS), F32)],
        compiler_params=_cparams("parallel"),
    )(xb, w_pad, vec(q_norm), vec(kv_norm), vec(kidx_g), vec(kidx_b))


def _qabs_kernel(cq_ref, wuq_ref, wuk_ref, o_ref):
    q = _dot(cq_ref[...], wuq_ref[...]).astype(BF16)
    for h in range(C_HEADS):
        qa = _dot(q[:, h * C_HEAD_DIM:(h + 1) * C_HEAD_DIM], wuk_ref[h], _NT)
        o_ref[h] = (qa * (C_HEAD_DIM ** -0.5 * LOG2E)).astype(BF16)


def dsa_qabs(cq, w_uq, w_uk, *, tm):
    s = cq.shape[0]
    return pl.pallas_call(
        _qabs_kernel,
        grid=(s // tm,),
        in_specs=[pl.BlockSpec((tm, C_Q_RANK), lambda i: (i, 0)),
                  pl.BlockSpec((C_Q_RANK, C_HEADS * C_HEAD_DIM), lambda i: (0, 0)),
                  pl.BlockSpec((C_HEADS, C_KV_RANK, C_HEAD_DIM), lambda i: (0, 0, 0))],
        out_specs=pl.BlockSpec((C_HEADS, tm, C_KV_RANK), lambda i: (0, i, 0)),
        out_shape=jax.ShapeDtypeStruct((C_HEADS, s, C_KV_RANK), BF16),
        compiler_params=_cparams("parallel"),
    )(cq, w_uq, w_uk)


def _sortable_key(score):
    bits = lax.bitcast_convert_type(score + 0.0, jnp.int32)
    return jnp.where(bits < 0, bits ^ jnp.int32(0x7FFFFFFF), bits)


def _dsa_attn_kernel(qabs_ref, qidx_ref, widx_t_ref, kidx_ref, ckv_ref, wuv_ref, bias_ref,
                     o_ref, keys_ref, acc_ref, m_ref, l_ref, *, k_sel):
    i = pl.program_id(0)
    n_tiles = i + 1
    k_local = lax.broadcasted_iota(jnp.int32, (QB, QB), 0)
    q_local = lax.broadcasted_iota(jnp.int32, (QB, QB), 1)
    allowed_diag_t = (k_local >> CHUNK_SHIFT) <= (q_local >> CHUNK_SHIFT)

    widx_t = widx_t_ref[...]

    def score_tile(t, carry):
        off = pl.multiple_of(t * QB, QB)
        kt = kidx_ref[pl.ds(off, QB), :]
        sc = jnp.zeros((QB, QB), F32)
        for j in range(IDX_HEADS):
            d = _dot(kt, qidx_ref[:, j * IDX_DIM:(j + 1) * IDX_DIM], _NT)
            sc = sc + widx_t[j:j + 1, :] * jnp.maximum(d, 0.0)
        key = _sortable_key(sc)
        key = jnp.where(jnp.logical_or(t < i, allowed_diag_t), key, jnp.int32(INT_MIN))
        keys_ref[pl.ds(off, QB), :] = key
        return carry

    lax.fori_loop(0, n_tiles, score_tile, 0)

    def selected(kt, off, thr, cut):
        pos = lax.broadcasted_iota(jnp.int32, kt.shape, 0) + off
        return jnp.logical_or(kt > thr, jnp.logical_and(kt == thr, pos < cut))

    def count(pred):
        def body(t, acc):
            off = pl.multiple_of(t * QB, QB)
            hit = pred(keys_ref[pl.ds(off, QB), :], off).astype(jnp.int32)
            return acc + jnp.sum(hit.reshape(QB // COUNT_ROWS, COUNT_ROWS, QB), axis=0)
        acc = lax.fori_loop(0, n_tiles, body, jnp.zeros((COUNT_ROWS, QB), jnp.int32))
        return jnp.sum(acc.astype(F32), axis=0, keepdims=True).astype(jnp.int32)

    def thr_bit(it, thr):
        cand = thr + jnp.left_shift(jnp.int32(1), 31 - it)
        cnt = count(lambda kt, off: kt >= cand)
        return jnp.where(cnt >= k_sel, cand, thr)

    thr = lax.fori_loop(0, 32, thr_bit, jnp.full((1, QB), INT_MIN, jnp.int32))

    def cut_bit(it, cut):
        cand = cut + jnp.left_shift(jnp.int32(1), CUT_BITS - 1 - it)
        cnt = count(lambda kt, off: selected(kt, off, thr, cand))
        return jnp.where(cnt <= k_sel, cand, cut)

    n_ge = count(lambda kt, off: kt >= thr)
    tied = jnp.logical_and(n_ge > k_sel, thr > INT_MIN)
    any_tied = jnp.max(jnp.where(tied, 1.0, 0.0)) > 0.0
    cut = lax.cond(any_tied,
                   lambda: lax.fori_loop(0, CUT_BITS, cut_bit, jnp.zeros((1, QB), jnp.int32)),
                   lambda: jnp.full((1, QB), 2 ** CUT_BITS - 1, jnp.int32))
    rep_lanes = lambda a, n: jnp.concatenate([a] * (n // LANES), axis=-1)

    m_ref[...] = jnp.full(m_ref.shape, NEG_BIG, F32)
    l_ref[...] = jnp.zeros(l_ref.shape, F32)
    acc_ref[...] = jnp.zeros(acc_ref.shape, F32)

    def attend(off, width, near):
        sel = selected(keys_ref[pl.ds(off, width), :], off, thr, cut)
        if near == 1:
            sel = jnp.logical_and(sel, allowed_diag_t)
        mask_add = jnp.where(sel, 0.0, NEG_BIG).T[None]
        kv = ckv_ref[pl.ds(off, width), :]
        rep = lambda a, n: jnp.concatenate([a] * (n // LANES), axis=-1)

        for g in range(C_HEADS // HEAD_GROUP):
            hs = slice(g * HEAD_GROUP, (g + 1) * HEAD_GROUP)
            q = qabs_ref[hs].reshape(HEAD_GROUP * QB, C_KV_RANK)
            s = _dot(q, kv, _NT).reshape(HEAD_GROUP, QB, width) + mask_add
            if near is not None:
                s = s + bias_ref[near, hs]
            m_old = m_ref[hs]
            row_max = jnp.broadcast_to(jnp.max(s, axis=-1, keepdims=True), m_old.shape)
            m_new = jnp.maximum(m_old, row_max)
            alpha = jnp.exp2(m_old - m_new)
            p = jnp.exp2(s - rep(m_new, width))
            row_sum = jnp.broadcast_to(jnp.sum(p, axis=-1, keepdims=True), m_old.shape)
            l_ref[hs] = alpha * l_ref[hs] + row_sum
            pv = _dot(p.astype(BF16).reshape(HEAD_GROUP * QB, width), kv).reshape(HEAD_GROUP, QB, C_KV_RANK)
            acc_ref[hs] = rep(alpha, C_KV_RANK) * acc_ref[hs] + pv
            m_ref[hs] = m_new

    n_far = jnp.maximum(i - 1, 0)

    n_wide = n_far // FAR_TILES

    def far_wide(tw, carry):
        attend(pl.multiple_of(tw * (FAR_TILES * QB), FAR_TILES * QB), FAR_TILES * QB, None)
        return carry

    def far_single(t, carry):
        attend(pl.multiple_of(t * QB, QB), QB, None)
        return carry

    lax.fori_loop(0, n_wide, far_wide, 0)
    lax.fori_loop(n_wide * FAR_TILES, n_far, far_single, 0)

    @pl.when(i >= 1)
    def _():
        attend(pl.multiple_of((i - 1) * QB, QB), QB, 0)

    attend(pl.multiple_of(i * QB, QB), QB, 1)

    for h in range(C_HEADS):
        o_lat = (acc_ref[h] / rep_lanes(l_ref[h], C_KV_RANK)).astype(BF16)
        o_ref[:, h * C_HEAD_DIM:(h + 1) * C_HEAD_DIM] = _dot(o_lat, wuv_ref[h]).astype(o_ref.dtype)


def dsa_attention(qabs, qidx, widx_t, kidx, ckv, w_uv, bias_near, *, k_sel):
    _, s, _ = qabs.shape
    full2 = lambda i: (0, 0)
    once = pl.Buffered(1)
    return pl.pallas_call(
        functools.partial(_dsa_attn_kernel, k_sel=k_sel),
        grid=(s // QB,),
        in_specs=[pl.BlockSpec((C_HEADS, QB, C_KV_RANK), lambda i: (0, i, 0)),
                  pl.BlockSpec((QB, IDX_HEADS * IDX_DIM), lambda i: (i, 0)),
                  pl.BlockSpec((IDX_HEADS, QB), lambda i: (0, i)),
                  pl.BlockSpec((s, IDX_DIM), full2, pipeline_mode=once),
                  pl.BlockSpec((s, C_KV_RANK), full2, pipeline_mode=once),
                  pl.BlockSpec((C_HEADS, C_KV_RANK, C_HEAD_DIM), lambda i: (0, 0, 0), pipeline_mode=once),
                  pl.BlockSpec((2, C_HEADS, QB, QB), lambda i: (0, 0, 0, 0), pipeline_mode=once)],
        out_specs=pl.BlockSpec((QB, C_HEADS * C_HEAD_DIM), lambda i: (i, 0)),
        out_shape=jax.ShapeDtypeStruct((s, C_HEADS * C_HEAD_DIM), BF16),
        scratch_shapes=[pltpu.VMEM((s, QB), jnp.int32),
                        pltpu.VMEM((C_HEADS, QB, C_KV_RANK), F32),
                        pltpu.VMEM((C_HEADS, QB, LANES), F32),
                        pltpu.VMEM((C_HEADS, QB, LANES), F32)],
        compiler_params=_cparams("parallel"),
    )(qabs, qidx, widx_t, kidx, ckv, w_uv, bias_near)


def _t5_bucket(rel):
    nb = REL_BUCKETS // 2
    max_exact = nb // 2
    ret = jnp.where(rel > 0, nb, 0)
    n = jnp.abs(rel)
    nf = jnp.maximum(n, 1).astype(jnp.float32)
    large = max_exact + (jnp.log(nf / max_exact) / math.log(REL_MAX_DIST / max_exact) * (nb - max_exact)).astype(jnp.int32)
    large = jnp.minimum(large, nb - 1)
    return ret + jnp.where(n < max_exact, n, large)


def _near_bias(rel_bias):
    ql = jnp.arange(QB)[:, None]
    sl = jnp.arange(QB)[None, :]
    rel = jnp.stack([sl - ql - QB, sl - ql])
    far = rel_bias[_t5_bucket(jnp.array(-2 * QB))]
    table = ((rel_bias - far) * LOG2E).astype(F32)
    onehot = jax.nn.one_hot(_t5_bucket(rel), REL_BUCKETS, dtype=F32)
    return jnp.einsum('tqsb,bh->thqs', onehot, table, precision=lax.Precision.HIGHEST)


def _pad_rows(w, at, total):
    return jnp.zeros((total, w.shape[1]), w.dtype).at[at:at + w.shape[0]].set(w)


def _block_diag(w):
    n, d, e = w.shape
    eye = jnp.eye(n, dtype=w.dtype)
    return (eye[:, None, :, None] * w[:, :, None, :]).reshape(n * d, n * e)


def _even_mixer(x, xb, w_in, w_out, j, mu, w0, w2, a0, a2, g2, k_k, k_a, r_k, gn_g, gn_b,
                conv_w, conv_b, w_r, b_r, w_i, b_i, lam, ln_g, ln_b, tm):
    w_in = w_in.astype(BF16)
    n_rkv = 3 * A_WIDTH
    z_rkv = matmul(xb, w_in[:, :n_rkv], tm=tm, tn=1024)
    w_lo = jnp.pad(w_in[:, n_rkv:A_COLS], ((0, 0), (0, A_LORA_PAD - A_LORA)))
    z_lo = matmul(xb, w_lo, tm=tm, tn=A_LORA_PAD)
    z_b = matmul(xb, w_in[:, A_COLS:], tm=tm, tn=1024)

    mu_lo = jnp.pad(mu[n_rkv:], (0, A_LORA_PAD - A_LORA))
    w2p = _pad_rows(w2, 0, A_LORA_PAD).astype(BF16)
    a2p = _pad_rows(a2, A_DECAY_LORA, A_LORA_PAD).astype(BF16)
    g2p = _pad_rows(g2, A_DECAY_LORA + A_ICL_LORA, A_LORA_PAD).astype(BF16)
    r, lw, k2, v, kap, b, g, bonus = rwkv_prep(
        z_rkv, z_lo, mu[:n_rkv], mu_lo, w0, w2p, a0, a2p, g2p, k_k, k_a, r_k.reshape(-1), tm=min(tm, 256))
    y = rwkv_chunks(r, lw, k2, v, kap, b)
    y_a = rwkv_post(y, bonus, g, gn_g, gn_b, tm=tm)

    y_b = rglru(z_b, conv_w, conv_b, _block_diag(w_r).astype(BF16), b_r,
                _block_diag(w_i).astype(BF16), b_i, lam, tm=min(tm, 256))
    return matmul2_residual_ln(y_a, y_b, w_out, j, x, ln_g, ln_b, tm=tm)


def _odd_mixer(x, xb, w_in, w_out, j, q_norm, kv_norm, w_uq, w_uk, w_uv, w_qidx, kidx_g, kidx_b,
               bias_near, ln_g, ln_b, tm):
    s = x.shape[0]
    d = w_in.shape[0]
    n_qkv = C_Q_RANK + C_KV_RANK
    w_pad = jnp.zeros((d, ODD_PAD), BF16)
    w_pad = w_pad.at[:, :n_qkv].set(w_in[:, :n_qkv].astype(BF16))
    w_pad = w_pad.at[:, ODD_KIDX_AT:ODD_KIDX_AT + IDX_DIM].set(w_in[:, n_qkv:n_qkv + IDX_DIM].astype(BF16))
    w_pad = w_pad.at[:, ODD_WIDX_AT:ODD_WIDX_AT + IDX_HEADS].set(w_in[:, n_qkv + IDX_DIM:].astype(BF16))
    cq, ckv, kidx, widx = dsa_in(xb, w_pad, q_norm, kv_norm, kidx_g, kidx_b, tm=tm)
    qabs = dsa_qabs(cq, w_uq.astype(BF16), w_uk.astype(BF16), tm=tm)
    qidx = matmul(cq, w_qidx.astype(BF16), tm=tm, tn=IDX_HEADS * IDX_DIM, out_dtype=BF16)
    o = dsa_attention(qabs, qidx, widx.T, kidx, ckv, w_uv.astype(BF16), bias_near,
                      k_sel=min(TOPK_MAX, s // 4))
    return matmul_residual_ln(o, w_out, j, x, ln_g, ln_b, tm=tm, tk=o.shape[1])


def kernel(x, p, rel_bias, ln1_g, ln1_b, ln2_g, ln2_b, ffn_w_up, ffn_conv_w, ffn_conv_b, ffn_w_down, ple_w_proj, ple_w_gate, ev_w_in, ev_w_out, a_mu, a_w0, a_w2, a_a0, a_a2, a_g2, a_k_k, a_k_a, a_r_k, a_gn_g, a_gn_b, b_conv_w, b_conv_b, b_w_r, b_b_r, b_w_i, b_b_i, b_lambda, od_w_in, od_w_out, c_q_norm, c_kv_norm, c_w_uq, c_w_uk, c_w_uv, c_w_qidx, c_kidx_g, c_kidx_b):
    bsz, s, d = x.shape
    assert bsz == 1 and s % QB == 0 and s <= 2 ** CUT_BITS
    tm = min(512, s)
    x = x[0]
    xb = x.astype(BF16)
    bias_near = _near_bias(rel_bias)
    ev_w_out_b, od_w_out_b = ev_w_out.astype(BF16), od_w_out.astype(BF16)
    w_up_b, w_down_b = ffn_w_up.astype(BF16), ffn_w_down.astype(BF16)
    w_gate_b, w_proj_b, p_b = ple_w_gate.astype(BF16), ple_w_proj.astype(BF16), p.astype(BF16)
    for layer in range(DEPTH):
        j = layer // 2
        if layer % 2 == 0:
            x, xb = _even_mixer(x, xb, ev_w_in[j], ev_w_out_b, j, a_mu[j], a_w0[j], a_w2[j], a_a0[j], a_a2[j],
                                a_g2[j], a_k_k[j], a_k_a[j], a_r_k[j], a_gn_g[j], a_gn_b[j],
                                b_conv_w[j], b_conv_b[j], b_w_r[j], b_b_r[j], b_w_i[j], b_b_i[j], b_lambda[j],
                                ln1_g[layer], ln1_b[layer], tm)
        else:
            x, xb = _odd_mixer(x, xb, od_w_in[j], od_w_out_b, j, c_q_norm[j], c_kv_norm[j], c_w_uq[j], c_w_uk[j],
                               c_w_uv[j], c_w_qidx[j], c_kidx_g[j], c_kidx_b[j], bias_near,
                               ln1_g[layer], ln1_b[layer], tm)
        hm = ffn_up(xb, w_up_b, layer, ffn_conv_w[layer], ffn_conv_b[layer], tm=min(4 * tm, s), sub=min(2 * tm, s), tc=512)
        x, xb = matmul_residual_ln(hm, w_down_b, layer, x, ln2_g[layer], ln2_b[layer], tm=tm, tk=D_FF // 2)
        x, xb = ple_update(xb, w_gate_b, p_b, w_proj_b, layer, x, tm=min(2 * tm, s), tn=1024)
    return x[None]
```

```python
import functools
import math

import jax
import jax.numpy as jnp
from jax import lax
from jax.experimental import pallas as pl
from jax.experimental.pallas import tpu as pltpu

F32 = jnp.float32
BF16 = jnp.bfloat16

D_MODEL = 2048
DEPTH = 4
CHUNK = 64
DN_ALPHA = (2 * DEPTH) ** 0.25
LN_EPS = 1e-5
A_WIDTH = 1024
A_HEAD = 64
A_HEADS = 16
A_DECAY_LORA = 64
A_ICL_LORA = 64
A_GATE_LORA = 160
A_LORA = A_DECAY_LORA + A_ICL_LORA + A_GATE_LORA
A_LORA_PAD = 384
A_COLS = 3 * A_WIDTH + A_LORA
A_GN_EPS = 64e-5
B_WIDTH = 1024
B_BLOCKS = 16
B_BLOCK = 64
B_CONV = 4
B_C = 8.0
C_HEADS = 16
C_HEAD_DIM = 128
C_Q_RANK = 512
C_KV_RANK = 256
IDX_HEADS = 16
IDX_DIM = 64
TOPK_MAX = 256
REL_BUCKETS = 32
REL_MAX_DIST = 128
D_FF = 5632
FFN_CONV = 3
PLE_DIM = 256

VMEM_LIMIT_BYTES = 56 * 1024 * 1024
NEG_BIG = -1e30
HALO = 8
LANES = 128
LOG2E = math.log2(math.e)


def _cparams(*sem):
    return pltpu.CompilerParams(dimension_semantics=sem, vmem_limit_bytes=VMEM_LIMIT_BYTES)


def _split_bf16(a):
    hi = a.astype(BF16)
    lo = (a - hi.astype(F32)).astype(BF16)
    return hi, lo


def _dot(a, b, dims=(((1,), (0,)), ((), ()))):
    return lax.dot_general(a, b, dims, preferred_element_type=F32)


_NT = (((1,), (1,)), ((), ()))
_TN = (((0,), (0,)), ((), ()))
_NN = (((1,), (0,)), ((), ()))


def _dot3(a, b, dims=_NN):
    ah, al = _split_bf16(a)
    bh, bl = _split_bf16(b)
    return _dot(ah, bh, dims) + (_dot(ah, bl, dims) + _dot(al, bh, dims))


def _dot_exact_rhs(a, b_bf16, dims=_NN):
    hi = a.astype(BF16)
    r1 = a - hi.astype(F32)
    mid = r1.astype(BF16)
    lo = (r1 - mid.astype(F32)).astype(BF16)
    return _dot(hi, b_bf16, dims) + (_dot(mid, b_bf16, dims) + _dot(lo, b_bf16, dims))


def _layer_norm_rows(v, g, b, eps):
    mu = jnp.mean(v, axis=-1, keepdims=True)
    d = v - mu
    var = jnp.mean(d * d, axis=-1, keepdims=True)
    return d * lax.rsqrt(var + eps) * g + b


def _softplus(x):
    return jnp.maximum(x, 0.0) + jnp.log1p(jnp.exp(-jnp.abs(x)))


def _mm_kernel(a_ref, b_ref, o_ref):
    o_ref[...] = _dot(a_ref[...], b_ref[...]).astype(o_ref.dtype)


def matmul(a, b, *, tm, tn, out_dtype=F32):
    m, k = a.shape
    _, n = b.shape
    assert m % tm == 0 and n % tn == 0
    return pl.pallas_call(
        _mm_kernel,
        grid=(n // tn, m // tm),
        in_specs=[pl.BlockSpec((tm, k), lambda j, i: (i, 0)),
                  pl.BlockSpec((k, tn), lambda j, i: (0, j))],
        out_specs=pl.BlockSpec((tm, tn), lambda j, i: (i, j)),
        out_shape=jax.ShapeDtypeStruct((m, n), out_dtype),
        compiler_params=_cparams("parallel", "parallel"),
    )(a, b)


def _mm1_ln_kernel(a_ref, b_ref, x_ref, g_ref, beta_ref, o_ref, ob_ref):
    y = _layer_norm_rows(DN_ALPHA * x_ref[...] + _dot(a_ref[...], b_ref[...]), g_ref[...], beta_ref[...], LN_EPS)
    o_ref[...] = y
    ob_ref[...] = y.astype(BF16)


def _mm_ln_kernel(a_ref, b_ref, x_ref, g_ref, beta_ref, o_ref, ob_ref, acc_ref, *, nk):
    kk = pl.program_id(1)

    @pl.when(kk == 0)
    def _():
        acc_ref[...] = jnp.zeros_like(acc_ref)

    acc_ref[...] += _dot(a_ref[...], b_ref[...])

    @pl.when(kk == nk - 1)
    def _():
        v = DN_ALPHA * x_ref[...] + acc_ref[...]
        y = _layer_norm_rows(v, g_ref[...], beta_ref[...], LN_EPS)
        o_ref[...] = y
        ob_ref[...] = y.astype(BF16)


def matmul_residual_ln(a, b, layer, x, g, beta, *, tm, tk):
    m, k = a.shape
    n = b.shape[2]
    assert m % tm == 0 and k % tk == 0
    nk = k // tk
    single = nk == 1
    return pl.pallas_call(
        _mm1_ln_kernel if single else functools.partial(_mm_ln_kernel, nk=nk),
        grid=(m // tm, nk),
        in_specs=[pl.BlockSpec((tm, tk), lambda i, kk: (i, kk)),
                  pl.BlockSpec((None, tk, n), lambda i, kk: (layer, kk, 0)),
                  pl.BlockSpec((tm, n), lambda i, kk: (i, 0)),
                  pl.BlockSpec((1, n), lambda i, kk: (0, 0)),
                  pl.BlockSpec((1, n), lambda i, kk: (0, 0))],
        out_specs=[pl.BlockSpec((tm, n), lambda i, kk: (i, 0)),
                   pl.BlockSpec((tm, n), lambda i, kk: (i, 0))],
        out_shape=[jax.ShapeDtypeStruct((m, n), F32), jax.ShapeDtypeStruct((m, n), BF16)],
        scratch_shapes=[] if single else [pltpu.VMEM((tm, n), F32)],
        compiler_params=_cparams("parallel", "arbitrary"),
    )(a, b, x, g.reshape(1, n), beta.reshape(1, n))


def _mm2_ln_kernel(a1_ref, a2_ref, b1_ref, b2_ref, x_ref, g_ref, beta_ref, o_ref, ob_ref):
    acc = _dot(a1_ref[...], b1_ref[...]) + _dot(a2_ref[...], b2_ref[...])
    y = _layer_norm_rows(DN_ALPHA * x_ref[...] + acc, g_ref[...], beta_ref[...], LN_EPS)
    o_ref[...] = y
    ob_ref[...] = y.astype(BF16)


def matmul2_residual_ln(a1, a2, b, layer, x, g, beta, *, tm):
    m, k1 = a1.shape
    k2 = a2.shape[1]
    n = b.shape[2]
    assert k1 == k2
    row = lambda i: (i, 0)
    const = lambda i: (0, 0)
    return pl.pallas_call(
        _mm2_ln_kernel,
        grid=(m // tm,),
        in_specs=[pl.BlockSpec((tm, k1), row), pl.BlockSpec((tm, k2), row),
                  pl.BlockSpec((None, k1, n), lambda i: (layer, 0, 0)),
                  pl.BlockSpec((None, k2, n), lambda i: (layer, 1, 0)),
                  pl.BlockSpec((tm, n), row), pl.BlockSpec((1, n), const), pl.BlockSpec((1, n), const)],
        out_specs=[pl.BlockSpec((tm, n), row), pl.BlockSpec((tm, n), row)],
        out_shape=[jax.ShapeDtypeStruct((m, n), F32), jax.ShapeDtypeStruct((m, n), BF16)],
        compiler_params=_cparams("parallel"),
    )(a1, a2, b, b, x, g.reshape(1, n), beta.reshape(1, n))


def _ple_kernel(xb_ref, wg_ref, p_ref, wp_ref, x_ref, o_ref, ob_ref):
    gate = jax.nn.sigmoid(_dot(xb_ref[...], wg_ref[...]))
    proj = _dot(p_ref[...], wp_ref[...])
    y = x_ref[...] + gate * proj
    o_ref[...] = y
    ob_ref[...] = y.astype(BF16)


def ple_update(xb, wg, pb, wp, layer, x, *, tm, tn):
    m, d = xb.shape
    pd = pb.shape[-1]
    return pl.pallas_call(
        _ple_kernel,
        grid=(d // tn, m // tm),
        in_specs=[pl.BlockSpec((tm, d), lambda j, i: (i, 0)),
                  pl.BlockSpec((None, d, tn), lambda j, i: (layer, 0, j)),
                  pl.BlockSpec((None, None, tm, pd), lambda j, i: (layer, 0, i, 0)),
                  pl.BlockSpec((None, pd, tn), lambda j, i: (layer, 0, j)),
                  pl.BlockSpec((tm, tn), lambda j, i: (i, j))],
        out_specs=[pl.BlockSpec((tm, tn), lambda j, i: (i, j)),
                   pl.BlockSpec((tm, tn), lambda j, i: (i, j))],
        out_shape=[jax.ShapeDtypeStruct((m, d), F32), jax.ShapeDtypeStruct((m, d), BF16)],
        compiler_params=_cparams("parallel", "parallel"),
    )(xb, wg, pb, wp, x)


def _ffn_up_kernel(x_ref, wg_ref, wu_ref, cwg_ref, cwu_ref, cbg_ref, cbu_ref, o_ref, eg_ref, eu_ref, *, tm, sub):
    i = pl.program_id(1)

    @pl.when(i == 0)
    def _():
        eg_ref[0:HALO, :] = jnp.zeros((HALO, eg_ref.shape[1]), F32)
        eu_ref[0:HALO, :] = jnp.zeros((HALO, eu_ref.shape[1]), F32)

    @pl.when(i > 0)
    def _():
        eg_ref[0:HALO, :] = eg_ref[tm:tm + HALO, :]
        eu_ref[0:HALO, :] = eu_ref[tm:tm + HALO, :]

    tc = o_ref.shape[1]
    halves = ((0, tc // 2), (tc // 2, tc))

    def project(k, e_ref, w_ref):
        e_ref[HALO + k * sub:HALO + (k + 1) * sub, :] = _dot(x_ref[k * sub:(k + 1) * sub, :], w_ref[...])

    def conv(e_ref, w_ref, b_ref, base, c0, c1):
        acc = b_ref[:, c0:c1] + w_ref[FFN_CONV - 1:FFN_CONV, c0:c1] * e_ref[base:base + sub, c0:c1]
        for d in range(1, FFN_CONV):
            acc = acc + w_ref[FFN_CONV - 1 - d:FFN_CONV - d, c0:c1] * e_ref[base - d:base - d + sub, c0:c1]
        return acc

    def gate_rows(k, c0, c1):
        base = HALO + k * sub
        gate = conv(eg_ref, cwg_ref, cbg_ref, base, c0, c1)
        up = conv(eu_ref, cwu_ref, cbu_ref, base, c0, c1)
        o_ref[k * sub:(k + 1) * sub, c0:c1] = (jax.nn.gelu(gate) * up).astype(o_ref.dtype)

    project(0, eg_ref, wg_ref)
    project(0, eu_ref, wu_ref)
    for k in range(tm // sub):
        more = (k + 1) * sub < tm
        if more:
            project(k + 1, eg_ref, wg_ref)
        gate_rows(k, *halves[0])
        if more:
            project(k + 1, eu_ref, wu_ref)
        gate_rows(k, *halves[1])


def ffn_up(xb, w_up, layer, conv_w, conv_b, *, tm, sub, tc):
    s, d = xb.shape
    two_ff = w_up.shape[2]
    ff = two_ff // 2
    nc = ff // tc
    cb = conv_b.reshape(1, two_ff)
    return pl.pallas_call(
        functools.partial(_ffn_up_kernel, tm=tm, sub=sub),
        grid=(nc, s // tm),
        in_specs=[pl.BlockSpec((tm, d), lambda j, i: (i, 0)),
                  pl.BlockSpec((None, d, tc), lambda j, i: (layer, 0, j)),
                  pl.BlockSpec((None, d, tc), lambda j, i: (layer, 0, j + nc)),
                  pl.BlockSpec((FFN_CONV, tc), lambda j, i: (0, j)),
                  pl.BlockSpec((FFN_CONV, tc), lambda j, i: (0, j + nc)),
                  pl.BlockSpec((1, tc), lambda j, i: (0, j)),
                  pl.BlockSpec((1, tc), lambda j, i: (0, j + nc))],
        out_specs=pl.BlockSpec((tm, tc), lambda j, i: (i, j)),
        out_shape=jax.ShapeDtypeStruct((s, ff), BF16),
        scratch_shapes=[pltpu.VMEM((tm + HALO, tc), F32), pltpu.VMEM((tm + HALO, tc), F32)],
        compiler_params=_cparams("parallel", "arbitrary"),
    )(xb, w_up, w_up, conv_w, conv_w, cb, cb)


def _shift_mix(z, halo, mu, live):
    prev = pltpu.roll(z, 1, 0)
    row0 = lax.broadcasted_iota(jnp.int32, z.shape, 0) == 0
    prev = jnp.where(row0, halo[HALO - 1:HALO, :] * live, prev)
    return z + (prev - z) * mu


def _rwkv_prep_kernel(z_ref, zh_ref, lo_ref, loh_ref, mu_ref, mulo_ref, w0_ref, w2_ref, a0_ref, a2_ref,
                      g2_ref, kk_ref, ka_ref, rk_ref, hsum_ref, hspread_ref,
                      r_o, lw_o, k_o, v_o, kap_o, b_o, g_o, bonus_o):
    live = (pl.program_id(0) > 0).astype(F32)
    z = _shift_mix(z_ref[...], zh_ref[...], mu_ref[...], live)
    lo = _shift_mix(lo_ref[...], loh_ref[...], mulo_ref[...], live)
    r = z[:, 0:A_WIDTH]
    k = z[:, A_WIDTH:2 * A_WIDTH]
    v = z[:, 2 * A_WIDTH:3 * A_WIDTH]
    w = -_softplus(-(w0_ref[...] + _dot(jnp.tanh(lo).astype(BF16), w2_ref[...]))) - 0.5
    lw = -jnp.exp(w)
    a = jax.nn.sigmoid(a0_ref[...] + _dot(lo.astype(BF16), a2_ref[...]))
    g = _dot(jax.nn.sigmoid(lo).astype(BF16), g2_ref[...])
    kk = k * kk_ref[...]
    head_sum = lambda t: _dot_exact_rhs(_dot_exact_rhs(t, hsum_ref[...]), hspread_ref[...])
    kap = kk / jnp.maximum(jnp.sqrt(head_sum(kk * kk)), 1e-12)
    k2 = k * (1.0 + (a - 1.0) * ka_ref[...])
    bonus = head_sum(r * k2 * rk_ref[...]) * v
    b = kap * a
    g_o[...] = g
    bonus_o[...] = bonus
    for h in range(A_HEADS):
        sl = slice(h * A_HEAD, (h + 1) * A_HEAD)
        r_o[h] = r[:, sl]
        lw_o[h] = lw[:, sl]
        k_o[h] = k2[:, sl]
        v_o[h] = v[:, sl]
        kap_o[h] = kap[:, sl]
        b_o[h] = b[:, sl]


def rwkv_prep(z_rkv, z_lo, mu_rkv, mu_lo, w0, w2p, a0, a2p, g2p, k_k, k_a, r_k, *, tm):
    s = z_rkv.shape[0]
    lane_head = jnp.arange(A_WIDTH)[:, None] // A_HEAD
    hsum = (lane_head == jnp.arange(LANES)[None, :]).astype(BF16)
    hspread = hsum.T
    hb = tm // HALO
    row = lambda i: (i, 0)
    halo = lambda i: (jnp.maximum(i * hb - 1, 0), 0)
    const = lambda i: (0, 0)
    hm = jax.ShapeDtypeStruct((A_HEADS, s, A_HEAD), F32)
    hm_spec = pl.BlockSpec((A_HEADS, tm, A_HEAD), lambda i: (0, i, 0))
    full = jax.ShapeDtypeStruct((s, A_WIDTH), F32)
    vec = lambda a: a.reshape(1, -1)
    return pl.pallas_call(
        _rwkv_prep_kernel,
        grid=(s // tm,),
        in_specs=[pl.BlockSpec((tm, 3 * A_WIDTH), row), pl.BlockSpec((HALO, 3 * A_WIDTH), halo),
                  pl.BlockSpec((tm, A_LORA_PAD), row), pl.BlockSpec((HALO, A_LORA_PAD), halo),
                  pl.BlockSpec((1, 3 * A_WIDTH), const), pl.BlockSpec((1, A_LORA_PAD), const),
                  pl.BlockSpec((1, A_WIDTH), const), pl.BlockSpec((A_LORA_PAD, A_WIDTH), const),
                  pl.BlockSpec((1, A_WIDTH), const), pl.BlockSpec((A_LORA_PAD, A_WIDTH), const),
                  pl.BlockSpec((A_LORA_PAD, A_WIDTH), const),
                  pl.BlockSpec((1, A_WIDTH), const), pl.BlockSpec((1, A_WIDTH), const),
                  pl.BlockSpec((1, A_WIDTH), const), pl.BlockSpec((A_WIDTH, LANES), const),
                  pl.BlockSpec((LANES, A_WIDTH), const)],
        out_specs=[hm_spec] * 6 + [pl.BlockSpec((tm, A_WIDTH), row)] * 2,
        out_shape=[hm] * 6 + [full] * 2,
        compiler_params=_cparams("parallel"),
    )(z_rkv, z_rkv, z_lo, z_lo, vec(mu_rkv), vec(mu_lo), vec(w0), w2p, vec(a0), a2p, g2p,
      vec(k_k), vec(k_a), vec(r_k), hsum, hspread)


_BNN = (((2,), (1,)), ((0,), (0,)))
_BNT = (((2,), (2,)), ((0,), (0,)))
_BTN = (((1,), (1,)), ((0,), (0,)))


def _tri_inverse(a, row, col):
    mm1 = lambda p, q: _dot(p.astype(BF16), q.astype(BF16), _BNN)
    mm3 = lambda p, q: _dot3(p, q, _BNN)
    eye = (row == col).astype(F32)
    ad = jnp.where((row >> 3) == (col >> 3), a, 0.0)
    t = eye - ad
    a2 = mm1(ad, ad)
    t = t + mm1(t, a2)
    a4 = mm1(a2, a2)
    t = t + mm1(t, a4)
    for sh, mm in ((3, mm1), (4, mm3), (5, mm3)):
        inner = (row >> sh) == (col >> sh)
        outer = (row >> (sh + 1)) == (col >> (sh + 1))
        aoff = jnp.where(jnp.logical_and(outer, jnp.logical_not(inner)), a, 0.0)
        t = t - mm(mm(t, aoff), t)
    return t


def _rwkv_chunk_kernel(r_ref, lw_ref, k_ref, v_ref, kap_ref, b_ref, y_ref, state_ref, *, nch):
    c = CHUNK
    nb = A_HEADS * nch

    @pl.when(pl.program_id(0) == 0)
    def _():
        state_ref[...] = jnp.zeros_like(state_ref)

    row = lax.broadcasted_iota(jnp.int32, (1, c, c), 1)
    col = lax.broadcasted_iota(jnp.int32, (1, c, c), 2)
    tril = row >= col
    stril = row > col
    eye = row == col
    lower_ones = jnp.broadcast_to(tril.astype(BF16), (nb, c, c))

    load = lambda ref: ref[...].reshape(nb, c, A_HEAD)
    r, lw, k, v, kap, b = load(r_ref), load(lw_ref), load(k_ref), load(v_ref), load(kap_ref), load(b_ref)
    hi = lw.astype(BF16)
    r1 = lw - hi.astype(F32)
    mid = r1.astype(BF16)
    lo = (r1 - mid.astype(F32)).astype(BF16)
    ci = _dot(lower_ones, hi, _BNN) + (_dot(lower_ones, mid, _BNN) + _dot(lower_ones, lo, _BNN))
    ce = ci - lw
    cend = ci[:, c - 1:c, :]
    gn = jnp.exp(-ci)
    gend = jnp.exp(cend - ci)
    kap_h = kap * jnp.exp(ce)
    r_h = r * jnp.exp(ci)
    b_h = b * gn
    k_h = k * gn
    b_t = b * gend
    k_t = k * gend
    p = _dot3(jnp.concatenate([kap_h, r_h], axis=1), jnp.concatenate([b_h, k_h], axis=1), _BNT)
    a_ab = jnp.where(stril, p[:, :c, :c], 0.0)
    a_ak = jnp.where(stril, p[:, :c, c:], 0.0)
    r_b = jnp.where(tril, p[:, c:, :c], 0.0)
    r_k = jnp.where(tril, p[:, c:, c:], 0.0)
    t = _tri_inverse(a_ab, row, col)
    x = _dot3(t, jnp.concatenate([kap_h, _dot3(a_ak, v, _BNN)], axis=2), _BNN)
    wr = jnp.concatenate([x[:, :, :A_HEAD], r_h], axis=1)
    rbk = jnp.concatenate([r_b, r_k], axis=2)
    btk = jnp.concatenate([b_t, k_t, jnp.where(eye, jnp.exp(cend), 0.0)], axis=1)

    pick = lambda a, ch: a.reshape(A_HEADS, nch, *a.shape[1:])[:, ch]
    m = state_ref[...]
    for ch in range(nch):
        wm = _dot3(pick(wr, ch), m, _BNN)
        u = -(wm[:, :c] + pick(x, ch)[:, :, A_HEAD:])
        uv = jnp.concatenate([u, pick(v, ch)], axis=1)
        y = wm[:, c:] + _dot3(pick(rbk, ch), uv, _BNN)
        m = _dot3(pick(btk, ch), jnp.concatenate([uv, m], axis=1), _BTN)
        ym = jnp.mean(y, axis=-1, keepdims=True)
        yc = y - ym
        yv = jnp.mean(yc * yc, axis=-1, keepdims=True)
        y_ref[:, ch * c:(ch + 1) * c, :] = yc * lax.rsqrt(yv + A_GN_EPS)
    state_ref[...] = m


RWKV_CHUNKS_PER_STEP = 2


def rwkv_chunks(r, lw, k, v, kap, b):
    _, s, _ = r.shape
    nch = RWKV_CHUNKS_PER_STEP
    spec = pl.BlockSpec((A_HEADS, nch * CHUNK, A_HEAD), lambda n: (0, n, 0))
    return pl.pallas_call(
        functools.partial(_rwkv_chunk_kernel, nch=nch),
        grid=(s // (nch * CHUNK),),
        in_specs=[spec] * 6,
        out_specs=spec,
        out_shape=jax.ShapeDtypeStruct((A_HEADS, s, A_HEAD), F32),
        scratch_shapes=[pltpu.VMEM((A_HEADS, A_HEAD, A_HEAD), F32)],
        compiler_params=_cparams("arbitrary"),
    )(r, lw, k, v, kap, b)


def _rwkv_post_kernel(y_ref, bonus_ref, g_ref, gg_ref, gb_ref, o_ref):
    y = jnp.concatenate([y_ref[h] for h in range(A_HEADS)], axis=1)
    o_ref[...] = ((y * gg_ref[...] + gb_ref[...] + bonus_ref[...]) * g_ref[...]).astype(o_ref.dtype)


def rwkv_post(y, bonus, g, gn_g, gn_b, *, tm):
    _, s, _ = y.shape
    row = lambda i: (i, 0)
    const = lambda i: (0, 0)
    return pl.pallas_call(
        _rwkv_post_kernel,
        grid=(s // tm,),
        in_specs=[pl.BlockSpec((A_HEADS, tm, A_HEAD), lambda i: (0, i, 0)),
                  pl.BlockSpec((tm, A_WIDTH), row), pl.BlockSpec((tm, A_WIDTH), row),
                  pl.BlockSpec((1, A_WIDTH), const), pl.BlockSpec((1, A_WIDTH), const)],
        out_specs=pl.BlockSpec((tm, A_WIDTH), row),
        out_shape=jax.ShapeDtypeStruct((s, A_WIDTH), BF16),
        compiler_params=_cparams("parallel"),
    )(y, bonus, g, gn_g.reshape(1, -1), gn_b.reshape(1, -1))


def _rglru_kernel(xb_ref, gate_ref, halo_ref, cw_ref, cb_ref, wr_ref, br_ref, wi_ref, bi_ref, lam_ref,
                  o_ref, xe_ref, a_ref, u_ref, h_ref, carry_ref, *, tm):
    i = pl.program_id(0)

    @pl.when(i == 0)
    def _():
        carry_ref[...] = jnp.zeros_like(carry_ref)

    xe_ref[0:HALO, :] = halo_ref[...] * (i > 0).astype(F32)
    xe_ref[HALO:, :] = xb_ref[...]
    xc = cb_ref[...] + cw_ref[B_CONV - 1:B_CONV, :] * xe_ref[HALO:, :]
    for d in range(1, B_CONV):
        xc = xc + cw_ref[B_CONV - 1 - d:B_CONV - d, :] * xe_ref[HALO - d:HALO - d + tm, :]
    xcb = xc.astype(BF16)
    r = jax.nn.sigmoid(_dot(xcb, wr_ref[...]) + br_ref[...])
    gi = jax.nn.sigmoid(_dot(xcb, wi_ref[...]) + bi_ref[...])
    log_a = -B_C * r * _softplus(-lam_ref[...])
    a = jnp.exp(log_a)
    a_ref[...] = a
    u_ref[...] = jnp.sqrt(-jnp.tanh(log_a) * (a * a + 1.0)) * (gi * xc)

    def group(gidx, h):
        base = pl.multiple_of(gidx * HALO, HALO)
        a8 = a_ref[pl.ds(base, HALO), :]
        u8 = u_ref[pl.ds(base, HALO), :]
        rows = []
        for rr in range(HALO):
            h = a8[rr:rr + 1, :] * h + u8[rr:rr + 1, :]
            rows.append(h)
        h_ref[pl.ds(base, HALO), :] = jnp.concatenate(rows, axis=0)
        return h

    carry_ref[...] = lax.fori_loop(0, tm // HALO, group, carry_ref[...])
    o_ref[...] = (jax.nn.gelu(gate_ref[...]) * h_ref[...]).astype(o_ref.dtype)


def rglru(z_b, conv_w, conv_b, wr, b_r, wi, b_i, lam, *, tm):
    s = z_b.shape[0]
    hb = tm // HALO
    row = lambda i: (i, 0)
    const = lambda i: (0, 0)
    vec = lambda a: a.reshape(1, -1)
    return pl.pallas_call(
        functools.partial(_rglru_kernel, tm=tm),
        grid=(s // tm,),
        in_specs=[pl.BlockSpec((tm, B_WIDTH), row), pl.BlockSpec((tm, B_WIDTH), lambda i: (i, 1)),
                  pl.BlockSpec((HALO, B_WIDTH), lambda i: (jnp.maximum(i * hb - 1, 0), 0)),
                  pl.BlockSpec((B_CONV, B_WIDTH), const), pl.BlockSpec((1, B_WIDTH), const),
                  pl.BlockSpec((B_WIDTH, B_WIDTH), const), pl.BlockSpec((1, B_WIDTH), const),
                  pl.BlockSpec((B_WIDTH, B_WIDTH), const), pl.BlockSpec((1, B_WIDTH), const),
                  pl.BlockSpec((1, B_WIDTH), const)],
        out_specs=pl.BlockSpec((tm, B_WIDTH), row),
        out_shape=jax.ShapeDtypeStruct((s, B_WIDTH), BF16),
        scratch_shapes=[pltpu.VMEM((tm + HALO, B_WIDTH), F32), pltpu.VMEM((tm, B_WIDTH), F32),
                        pltpu.VMEM((tm, B_WIDTH), F32), pltpu.VMEM((tm, B_WIDTH), F32),
                        pltpu.VMEM((1, B_WIDTH), F32)],
        compiler_params=_cparams("arbitrary"),
    )(z_b, z_b, z_b, conv_w, vec(conv_b), wr, vec(b_r), wi, vec(b_i), vec(lam))


ODD_PAD = 1024
ODD_KIDX_AT = 768
ODD_WIDX_AT = 896
QB = 256
SCORE_SCALE = (IDX_HEADS ** -0.5) * (IDX_DIM ** -0.5)
INT_MIN = -(2 ** 31)
CHUNK_SHIFT = 6
CUT_BITS = 14
COUNT_ROWS = 32
HEAD_GROUP = 2
FAR_TILES = 2


def _dsa_in_kernel(x_ref, w_ref, qn_ref, kvn_ref, kg_ref, kb_ref, cq_o, ckv_o, kidx_o, widx_o):
    acc = _dot(x_ref[...], w_ref[...])
    cq = acc[:, 0:C_Q_RANK]
    ckv = acc[:, C_Q_RANK:C_Q_RANK + C_KV_RANK]
    kidx = acc[:, ODD_KIDX_AT:ODD_KIDX_AT + IDX_DIM]
    widx = acc[:, ODD_WIDX_AT:ODD_WIDX_AT + IDX_HEADS]
    rms = lambda t, g: t * lax.rsqrt(jnp.mean(t * t, axis=-1, keepdims=True) + 1e-6) * g
    cq_o[...] = rms(cq, qn_ref[...]).astype(BF16)
    ckv_o[...] = rms(ckv, kvn_ref[...]).astype(BF16)
    kidx_o[...] = _layer_norm_rows(kidx, kg_ref[...], kb_ref[...], LN_EPS).astype(BF16)
    widx_o[...] = widx * SCORE_SCALE


def dsa_in(xb, w_pad, q_norm, kv_norm, kidx_g, kidx_b, *, tm):
    s, d = xb.shape
    row = lambda i: (i, 0)
    const = lambda i: (0, 0)
    vec = lambda a: a.reshape(1, -1)
    return pl.pallas_call(
        _dsa_in_kernel,
        grid=(s // tm,),
        in_specs=[pl.BlockSpec((tm, d), row), pl.BlockSpec((d, ODD_PAD), const),
                  pl.BlockSpec((1, C_Q_RANK), const), pl.BlockSpec((1, C_KV_RANK), const),
                  pl.BlockSpec((1, IDX_DIM), const), pl.BlockSpec((1, IDX_DIM), const)],
        out_specs=[pl.BlockSpec((tm, C_Q_RANK), row), pl.BlockSpec((tm, C_KV_RANK), row),
                   pl.BlockSpec((tm, IDX_DIM), row), pl.BlockSpec((tm, IDX_HEADS), row)],
        out_shape=[jax.ShapeDtypeStruct((s, C_Q_RANK), BF16), jax.ShapeDtypeStruct((s, C_KV_RANK), BF16),
                   jax.ShapeDtypeStruct((s, IDX_DIM), BF16), jax.ShapeDtypeStruct((s, IDX_HEADS), F32)],
        compiler_params=_cparams("parallel"),
    )(xb, w_pad, vec(q_norm), vec(kv_norm), vec(kidx_g), vec(kidx_b))


def _qabs_kernel(cq_ref, wuq_ref, wuk_ref, o_ref):
    q = _dot(cq_ref[...], wuq_ref[...]).astype(BF16)
    for h in range(C_HEADS):
        qa = _dot(q[:, h * C_HEAD_DIM:(h + 1) * C_HEAD_DIM], wuk_ref[h], _NT)
        o_ref[h] = (qa * (C_HEAD_DIM ** -0.5 * LOG2E)).astype(BF16)


def dsa_qabs(cq, w_uq, w_uk, *, tm):
    s = cq.shape[0]
    return pl.pallas_call(
        _qabs_kernel,
        grid=(s // tm,),
        in_specs=[pl.BlockSpec((tm, C_Q_RANK), lambda i: (i, 0)),
                  pl.BlockSpec((C_Q_RANK, C_HEADS * C_HEAD_DIM), lambda i: (0, 0)),
                  pl.BlockSpec((C_HEADS, C_KV_RANK, C_HEAD_DIM), lambda i: (0, 0, 0))],
        out_specs=pl.BlockSpec((C_HEADS, tm, C_KV_RANK), lambda i: (0, i, 0)),
        out_shape=jax.ShapeDtypeStruct((C_HEADS, s, C_KV_RANK), BF16),
        compiler_params=_cparams("parallel"),
    )(cq, w_uq, w_uk)


def _sortable_key(score):
    bits = lax.bitcast_convert_type(score + 0.0, jnp.int32)
    return jnp.where(bits < 0, bits ^ jnp.int32(0x7FFFFFFF), bits)


def _dsa_attn_kernel(qabs_ref, qidx_ref, widx_t_ref, kidx_ref, ckv_ref, wuv_ref, bias_ref,
                     o_ref, keys_ref, acc_ref, m_ref, l_ref, *, k_sel):
    i = pl.program_id(0)
    n_tiles = i + 1
    k_local = lax.broadcasted_iota(jnp.int32, (QB, QB), 0)
    q_local = lax.broadcasted_iota(jnp.int32, (QB, QB), 1)
    allowed_diag_t = (k_local >> CHUNK_SHIFT) <= (q_local >> CHUNK_SHIFT)

    widx_t = widx_t_ref[...]

    def score_keys(off, width):
        kt = kidx_ref[pl.ds(off, width), :]
        sc = jnp.zeros((width, QB), F32)
        for j in range(IDX_HEADS):
            d = _dot(kt, qidx_ref[:, j * IDX_DIM:(j + 1) * IDX_DIM], _NT)
            sc = sc + widx_t[j:j + 1, :] * jnp.maximum(d, 0.0)
        return _sortable_key(sc)

    def score_pair(tp, carry):
        off = pl.multiple_of(tp * (2 * QB), 2 * QB)
        keys_ref[pl.ds(off, 2 * QB), :] = score_keys(off, 2 * QB)
        return carry

    def score_tile(t, carry):
        off = pl.multiple_of(t * QB, QB)
        key = score_keys(off, QB)
        keys_ref[pl.ds(off, QB), :] = jnp.where(jnp.logical_or(t < i, allowed_diag_t), key, jnp.int32(INT_MIN))
        return carry

    lax.fori_loop(0, i // 2, score_pair, 0)
    lax.fori_loop(2 * (i // 2), n_tiles, score_tile, 0)

    def selected(kt, off, thr, cut):
        pos = lax.broadcasted_iota(jnp.int32, kt.shape, 0) + off
        return jnp.logical_or(kt > thr, jnp.logical_and(kt == thr, pos < cut))

    def count(pred):
        def hits(off, width, acc):
            hit = pred(keys_ref[pl.ds(off, width), :], off).astype(jnp.int32)
            return acc + jnp.sum(hit.reshape(width // COUNT_ROWS, COUNT_ROWS, QB), axis=0)

        def pair(tp, acc):
            return hits(pl.multiple_of(tp * (2 * QB), 2 * QB), 2 * QB, acc)

        def single(t, acc):
            return hits(pl.multiple_of(t * QB, QB), QB, acc)

        acc = lax.fori_loop(0, n_tiles // 2, pair, jnp.zeros((COUNT_ROWS, QB), jnp.int32))
        acc = lax.fori_loop(2 * (n_tiles // 2), n_tiles, single, acc)
        return jnp.sum(acc.astype(F32), axis=0, keepdims=True).astype(jnp.int32)

    def thr_bit(it, thr):
        cand = thr + jnp.left_shift(jnp.int32(1), 31 - it)
        cnt = count(lambda kt, off: kt >= cand)
        return jnp.where(cnt >= k_sel, cand, thr)

    thr = lax.fori_loop(0, 32, thr_bit, jnp.full((1, QB), INT_MIN, jnp.int32))

    def cut_bit(it, cut):
        cand = cut + jnp.left_shift(jnp.int32(1), CUT_BITS - 1 - it)
        cnt = count(lambda kt, off: selected(kt, off, thr, cand))
        return jnp.where(cnt <= k_sel, cand, cut)

    n_ge = count(lambda kt, off: kt >= thr)
    tied = jnp.logical_and(n_ge > k_sel, thr > INT_MIN)
    any_tied = jnp.max(jnp.where(tied, 1.0, 0.0)) > 0.0
    cut = lax.cond(any_tied,
                   lambda: lax.fori_loop(0, CUT_BITS, cut_bit, jnp.zeros((1, QB), jnp.int32)),
                   lambda: jnp.full((1, QB), 2 ** CUT_BITS - 1, jnp.int32))
    rep_lanes = lambda a, n: jnp.concatenate([a] * (n // LANES), axis=-1)

    m_ref[...] = jnp.full(m_ref.shape, NEG_BIG, F32)
    l_ref[...] = jnp.zeros(l_ref.shape, F32)
    acc_ref[...] = jnp.zeros(acc_ref.shape, F32)

    def attend(off, width, near):
        sel = selected(keys_ref[pl.ds(off, width), :], off, thr, cut)
        if near == 1:
            sel = jnp.logical_and(sel, allowed_diag_t)
        mask_add = jnp.where(sel, 0.0, NEG_BIG).T[None]
        kv = ckv_ref[pl.ds(off, width), :]
        rep = lambda a, n: jnp.concatenate([a] * (n // LANES), axis=-1)

        for g in range(C_HEADS // HEAD_GROUP):
            hs = slice(g * HEAD_GROUP, (g + 1) * HEAD_GROUP)
            q = qabs_ref[hs].reshape(HEAD_GROUP * QB, C_KV_RANK)
            s = _dot(q, kv, _NT).reshape(HEAD_GROUP, QB, width) + mask_add
            if near is not None:
                s = s + bias_ref[near, hs]
            m_old = m_ref[hs]
            row_max = jnp.broadcast_to(jnp.max(s, axis=-1, keepdims=True), m_old.shape)
            m_new = jnp.maximum(m_old, row_max)
            alpha = jnp.exp2(m_old - m_new)
            p = jnp.exp2(s - rep(m_new, width))
            row_sum = jnp.broadcast_to(jnp.sum(p, axis=-1, keepdims=True), m_old.shape)
            l_ref[hs] = alpha * l_ref[hs] + row_sum
            pv = _dot(p.astype(BF16).reshape(HEAD_GROUP * QB, width), kv).reshape(HEAD_GROUP, QB, C_KV_RANK)
            acc_ref[hs] = rep(alpha, C_KV_RANK) * acc_ref[hs] + pv
            m_ref[hs] = m_new

    n_far = jnp.maximum(i - 1, 0)

    n_wide = n_far // FAR_TILES

    def far_wide(tw, carry):
        attend(pl.multiple_of(tw * (FAR_TILES * QB), FAR_TILES * QB), FAR_TILES * QB, None)
        return carry

    def far_single(t, carry):
        attend(pl.multiple_of(t * QB, QB), QB, None)
        return carry

    lax.fori_loop(0, n_wide, far_wide, 0)
    lax.fori_loop(n_wide * FAR_TILES, n_far, far_single, 0)

    @pl.when(i >= 1)
    def _():
        attend(pl.multiple_of((i - 1) * QB, QB), QB, 0)

    attend(pl.multiple_of(i * QB, QB), QB, 1)

    for h in range(C_HEADS):
        o_lat = (acc_ref[h] / rep_lanes(l_ref[h], C_KV_RANK)).astype(BF16)
        o_ref[:, h * C_HEAD_DIM:(h + 1) * C_HEAD_DIM] = _dot(o_lat, wuv_ref[h]).astype(o_ref.dtype)


def dsa_attention(qabs, qidx, widx_t, kidx, ckv, w_uv, bias_near, *, k_sel):
    _, s, _ = qabs.shape
    full2 = lambda i: (0, 0)
    once = pl.Buffered(1)
    return pl.pallas_call(
        functools.partial(_dsa_attn_kernel, k_sel=k_sel),
        grid=(s // QB,),
        in_specs=[pl.BlockSpec((C_HEADS, QB, C_KV_RANK), lambda i: (0, i, 0)),
                  pl.BlockSpec((QB, IDX_HEADS * IDX_DIM), lambda i: (i, 0)),
                  pl.BlockSpec((IDX_HEADS, QB), lambda i: (0, i)),
                  pl.BlockSpec((s, IDX_DIM), full2, pipeline_mode=once),
                  pl.BlockSpec((s, C_KV_RANK), full2, pipeline_mode=once),
                  pl.BlockSpec((C_HEADS, C_KV_RANK, C_HEAD_DIM), lambda i: (0, 0, 0), pipeline_mode=once),
                  pl.BlockSpec((2, C_HEADS, QB, QB), lambda i: (0, 0, 0, 0), pipeline_mode=once)],
        out_specs=pl.BlockSpec((QB, C_HEADS * C_HEAD_DIM), lambda i: (i, 0)),
        out_shape=jax.ShapeDtypeStruct((s, C_HEADS * C_HEAD_DIM), BF16),
        scratch_shapes=[pltpu.VMEM((s, QB), jnp.int32),
                        pltpu.VMEM((C_HEADS, QB, C_KV_RANK), F32),
                        pltpu.VMEM((C_HEADS, QB, LANES), F32),
                        pltpu.VMEM((C_HEADS, QB, LANES), F32)],
        compiler_params=_cparams("parallel"),
    )(qabs, qidx, widx_t, kidx, ckv, w_uv, bias_near)


def _t5_bucket(rel):
    nb = REL_BUCKETS // 2
    max_exact = nb // 2
    ret = jnp.where(rel > 0, nb, 0)
    n = jnp.abs(rel)
    nf = jnp.maximum(n, 1).astype(jnp.float32)
    large = max_exact + (jnp.log(nf / max_exact) / math.log(REL_MAX_DIST / max_exact) * (nb - max_exact)).astype(jnp.int32)
    large = jnp.minimum(large, nb - 1)
    return ret + jnp.where(n < max_exact, n, large)


def _near_bias(rel_bias):
    ql = jnp.arange(QB)[:, None]
    sl = jnp.arange(QB)[None, :]
    rel = jnp.stack([sl - ql - QB, sl - ql])
    far = rel_bias[_t5_bucket(jnp.array(-2 * QB))]
    table = ((rel_bias - far) * LOG2E).astype(F32)
    onehot = jax.nn.one_hot(_t5_bucket(rel), REL_BUCKETS, dtype=F32)
    return jnp.einsum('tqsb,bh->thqs', onehot, table, precision=lax.Precision.HIGHEST)


def _pad_rows(w, at, total):
    return jnp.zeros((total, w.shape[1]), w.dtype).at[at:at + w.shape[0]].set(w)


def _block_diag(w):
    n, d, e = w.shape
    eye = jnp.eye(n, dtype=w.dtype)
    return (eye[:, None, :, None] * w[:, :, None, :]).reshape(n * d, n * e)


def _even_mixer(x, xb, w_in, w_out, j, mu, w0, w2, a0, a2, g2, k_k, k_a, r_k, gn_g, gn_b,
                conv_w, conv_b, w_r, b_r, w_i, b_i, lam, ln_g, ln_b, tm):
    w_in = w_in.astype(BF16)
    n_rkv = 3 * A_WIDTH
    z_rkv = matmul(xb, w_in[:, :n_rkv], tm=tm, tn=1024)
    w_lo = jnp.pad(w_in[:, n_rkv:A_COLS], ((0, 0), (0, A_LORA_PAD - A_LORA)))
    z_lo = matmul(xb, w_lo, tm=tm, tn=A_LORA_PAD)
    z_b = matmul(xb, w_in[:, A_COLS:], tm=tm, tn=1024)

    mu_lo = jnp.pad(mu[n_rkv:], (0, A_LORA_PAD - A_LORA))
    w2p = _pad_rows(w2, 0, A_LORA_PAD).astype(BF16)
    a2p = _pad_rows(a2, A_DECAY_LORA, A_LORA_PAD).astype(BF16)
    g2p = _pad_rows(g2, A_DECAY_LORA + A_ICL_LORA, A_LORA_PAD).astype(BF16)
    r, lw, k2, v, kap, b, g, bonus = rwkv_prep(
        z_rkv, z_lo, mu[:n_rkv], mu_lo, w0, w2p, a0, a2p, g2p, k_k, k_a, r_k.reshape(-1), tm=min(tm, 256))
    y = rwkv_chunks(r, lw, k2, v, kap, b)
    y_a = rwkv_post(y, bonus, g, gn_g, gn_b, tm=tm)

    y_b = rglru(z_b, conv_w, conv_b, _block_diag(w_r).astype(BF16), b_r,
                _block_diag(w_i).astype(BF16), b_i, lam, tm=min(tm, 256))
    return matmul2_residual_ln(y_a, y_b, w_out, j, x, ln_g, ln_b, tm=tm)


def _odd_mixer(x, xb, w_in, w_out, j, q_norm, kv_norm, w_uq, w_uk, w_uv, w_qidx, kidx_g, kidx_b,
               bias_near, ln_g, ln_b, tm):
    s = x.shape[0]
    d = w_in.shape[0]
    n_qkv = C_Q_RANK + C_KV_RANK
    w_pad = jnp.zeros((d, ODD_PAD), BF16)
    w_pad = w_pad.at[:, :n_qkv].set(w_in[:, :n_qkv].astype(BF16))
    w_pad = w_pad.at[:, ODD_KIDX_AT:ODD_KIDX_AT + IDX_DIM].set(w_in[:, n_qkv:n_qkv + IDX_DIM].astype(BF16))
    w_pad = w_pad.at[:, ODD_WIDX_AT:ODD_WIDX_AT + IDX_HEADS].set(w_in[:, n_qkv + IDX_DIM:].astype(BF16))
    cq, ckv, kidx, widx = dsa_in(xb, w_pad, q_norm, kv_norm, kidx_g, kidx_b, tm=tm)
    qabs = dsa_qabs(cq, w_uq.astype(BF16), w_uk.astype(BF16), tm=tm)
    qidx = matmul(cq, w_qidx.astype(BF16), tm=tm, tn=IDX_HEADS * IDX_DIM, out_dtype=BF16)
    o = dsa_attention(qabs, qidx, widx.T, kidx, ckv, w_uv.astype(BF16), bias_near,
                      k_sel=min(TOPK_MAX, s // 4))
    return matmul_residual_ln(o, w_out, j, x, ln_g, ln_b, tm=tm, tk=o.shape[1])


def kernel(x, p, rel_bias, ln1_g, ln1_b, ln2_g, ln2_b, ffn_w_up, ffn_conv_w, ffn_conv_b, ffn_w_down, ple_w_proj, ple_w_gate, ev_w_in, ev_w_out, a_mu, a_w0, a_w2, a_a0, a_a2, a_g2, a_k_k, a_k_a, a_r_k, a_gn_g, a_gn_b, b_conv_w, b_conv_b, b_w_r, b_b_r, b_w_i, b_b_i, b_lambda, od_w_in, od_w_out, c_q_norm, c_kv_norm, c_w_uq, c_w_uk, c_w_uv, c_w_qidx, c_kidx_g, c_kidx_b):
    bsz, s, d = x.shape
    assert bsz == 1 and s % QB == 0 and s <= 2 ** CUT_BITS
    tm = min(512, s)
    x = x[0]
    xb = x.astype(BF16)
    bias_near = _near_bias(rel_bias)
    ev_w_out_b, od_w_out_b = ev_w_out.astype(BF16), od_w_out.astype(BF16)
    w_up_b, w_down_b = ffn_w_up.astype(BF16), ffn_w_down.astype(BF16)
    w_gate_b, w_proj_b, p_b = ple_w_gate.astype(BF16), ple_w_proj.astype(BF16), p.astype(BF16)
    for layer in range(DEPTH):
        j = layer // 2
        if layer % 2 == 0:
            x, xb = _even_mixer(x, xb, ev_w_in[j], ev_w_out_b, j, a_mu[j], a_w0[j], a_w2[j], a_a0[j], a_a2[j],
                                a_g2[j], a_k_k[j], a_k_a[j], a_r_k[j], a_gn_g[j], a_gn_b[j],
                                b_conv_w[j], b_conv_b[j], b_w_r[j], b_b_r[j], b_w_i[j], b_b_i[j], b_lambda[j],
                                ln1_g[layer], ln1_b[layer], tm)
        else:
            x, xb = _odd_mixer(x, xb, od_w_in[j], od_w_out_b, j, c_q_norm[j], c_kv_norm[j], c_w_uq[j], c_w_uk[j],
                               c_w_uv[j], c_w_qidx[j], c_kidx_g[j], c_kidx_b[j], bias_near,
                               ln1_g[layer], ln1_b[layer], tm)
        hm = ffn_up(xb, w_up_b, layer, ffn_conv_w[layer], ffn_conv_b[layer], tm=min(4 * tm, s), sub=min(2 * tm, s), tc=512)
        x, xb = matmul_residual_ln(hm, w_down_b, layer, x, ln2_g[layer], ln2_b[layer], tm=tm, tk=D_FF // 2)
        x, xb = ple_update(xb, w_gate_b, p_b, w_proj_b, layer, x, tm=min(2 * tm, s), tn=1024)
    return x[None]
```

```python
import functools
import math

import jax
import jax.numpy as jnp
from jax import lax
from jax.experimental import pallas as pl
from jax.experimental.pallas import tpu as pltpu

F32 = jnp.float32
BF16 = jnp.bfloat16

D_MODEL = 2048
DEPTH = 4
CHUNK = 64
DN_ALPHA = (2 * DEPTH) ** 0.25
LN_EPS = 1e-5
A_WIDTH = 1024
A_HEAD = 64
A_HEADS = 16
A_DECAY_LORA = 64
A_ICL_LORA = 64
A_GATE_LORA = 160
A_LORA = A_DECAY_LORA + A_ICL_LORA + A_GATE_LORA
A_LORA_PAD = 384
A_COLS = 3 * A_WIDTH + A_LORA
A_GN_EPS = 64e-5
B_WIDTH = 1024
B_BLOCKS = 16
B_BLOCK = 64
B_CONV = 4
B_C = 8.0
C_HEADS = 16
C_HEAD_DIM = 128
C_Q_RANK = 512
C_KV_RANK = 256
IDX_HEADS = 16
IDX_DIM = 64
TOPK_MAX = 256
REL_BUCKETS = 32
REL_MAX_DIST = 128
D_FF = 5632
FFN_CONV = 3
PLE_DIM = 256

VMEM_LIMIT_BYTES = 56 * 1024 * 1024
NEG_BIG = -1e30
HALO = 8
LANES = 128
LOG2E = math.log2(math.e)


def _cparams(*sem):
    return pltpu.CompilerParams(dimension_semantics=sem, vmem_limit_bytes=VMEM_LIMIT_BYTES)


def _split_bf16(a):
    hi = a.astype(BF16)
    lo = (a - hi.astype(F32)).astype(BF16)
    return hi, lo


def _dot(a, b, dims=(((1,), (0,)), ((), ()))):
    return lax.dot_general(a, b, dims, preferred_element_type=F32)


_NT = (((1,), (1,)), ((), ()))
_TN = (((0,), (0,)), ((), ()))
_NN = (((1,), (0,)), ((), ()))


def _dot3(a, b, dims=_NN):
    ah, al = _split_bf16(a)
    bh, bl = _split_bf16(b)
    return _dot(ah, bh, dims) + (_dot(ah, bl, dims) + _dot(al, bh, dims))


def _dot_exact_rhs(a, b_bf16, dims=_NN):
    hi = a.astype(BF16)
    r1 = a - hi.astype(F32)
    mid = r1.astype(BF16)
    lo = (r1 - mid.astype(F32)).astype(BF16)
    return _dot(hi, b_bf16, dims) + (_dot(mid, b_bf16, dims) + _dot(lo, b_bf16, dims))


def _layer_norm_rows(v, g, b, eps):
    mu = jnp.mean(v, axis=-1, keepdims=True)
    d = v - mu
    var = jnp.mean(d * d, axis=-1, keepdims=True)
    return d * lax.rsqrt(var + eps) * g + b


def _softplus(x):
    return jnp.maximum(x, 0.0) + jnp.log1p(jnp.exp(-jnp.abs(x)))


def _mm_kernel(a_ref, b_ref, o_ref):
    o_ref[...] = _dot(a_ref[...], b_ref[...]).astype(o_ref.dtype)


def matmul(a, b, *, tm, tn, out_dtype=F32):
    m, k = a.shape
    _, n = b.shape
    assert m % tm == 0 and n % tn == 0
    return pl.pallas_call(
        _mm_kernel,
        grid=(n // tn, m // tm),
        in_specs=[pl.BlockSpec((tm, k), lambda j, i: (i, 0)),
                  pl.BlockSpec((k, tn), lambda j, i: (0, j))],
        out_specs=pl.BlockSpec((tm, tn), lambda j, i: (i, j)),
        out_shape=jax.ShapeDtypeStruct((m, n), out_dtype),
        compiler_params=_cparams("parallel", "parallel"),
    )(a, b)


def _mm1_ln_kernel(a_ref, b_ref, x_ref, g_ref, beta_ref, o_ref, ob_ref):
    y = _layer_norm_rows(DN_ALPHA * x_ref[...] + _dot(a_ref[...], b_ref[...]), g_ref[...], beta_ref[...], LN_EPS)
    o_ref[...] = y
    ob_ref[...] = y.astype(BF16)


def _mm_ln_kernel(a_ref, b_ref, x_ref, g_ref, beta_ref, o_ref, ob_ref, acc_ref, *, nk):
    kk = pl.program_id(1)

    @pl.when(kk == 0)
    def _():
        acc_ref[...] = jnp.zeros_like(acc_ref)

    acc_ref[...] += _dot(a_ref[...], b_ref[...])

    @pl.when(kk == nk - 1)
    def _():
        v = DN_ALPHA * x_ref[...] + acc_ref[...]
        y = _layer_norm_rows(v, g_ref[...], beta_ref[...], LN_EPS)
        o_ref[...] = y
        ob_ref[...] = y.astype(BF16)


def matmul_residual_ln(a, b, layer, x, g, beta, *, tm, tk):
    m, k = a.shape
    n = b.shape[2]
    assert m % tm == 0 and k % tk == 0
    nk = k // tk
    single = nk == 1
    return pl.pallas_call(
        _mm1_ln_kernel if single else functools.partial(_mm_ln_kernel, nk=nk),
        grid=(m // tm, nk),
        in_specs=[pl.BlockSpec((tm, tk), lambda i, kk: (i, kk)),
                  pl.BlockSpec((None, tk, n), lambda i, kk: (layer, kk, 0)),
                  pl.BlockSpec((tm, n), lambda i, kk: (i, 0)),
                  pl.BlockSpec((1, n), lambda i, kk: (0, 0)),
                  pl.BlockSpec((1, n), lambda i, kk: (0, 0))],
        out_specs=[pl.BlockSpec((tm, n), lambda i, kk: (i, 0)),
                   pl.BlockSpec((tm, n), lambda i, kk: (i, 0))],
        out_shape=[jax.ShapeDtypeStruct((m, n), F32), jax.ShapeDtypeStruct((m, n), BF16)],
        scratch_shapes=[] if single else [pltpu.VMEM((tm, n), F32)],
        compiler_params=_cparams("parallel", "arbitrary"),
    )(a, b, x, g.reshape(1, n), beta.reshape(1, n))


def _mm2_ln_kernel(a1_ref, a2_ref, b1_ref, b2_ref, x_ref, g_ref, beta_ref, o_ref, ob_ref):
    acc = _dot(a1_ref[...], b1_ref[...]) + _dot(a2_ref[...], b2_ref[...])
    y = _layer_norm_rows(DN_ALPHA * x_ref[...] + acc, g_ref[...], beta_ref[...], LN_EPS)
    o_ref[...] = y
    ob_ref[...] = y.astype(BF16)


def matmul2_residual_ln(a1, a2, b, layer, x, g, beta, *, tm):
    m, k1 = a1.shape
    k2 = a2.shape[1]
    n = b.shape[2]
    assert k1 == k2
    row = lambda i: (i, 0)
    const = lambda i: (0, 0)
    return pl.pallas_call(
        _mm2_ln_kernel,
        grid=(m // tm,),
        in_specs=[pl.BlockSpec((tm, k1), row), pl.BlockSpec((tm, k2), row),
                  pl.BlockSpec((None, k1, n), lambda i: (layer, 0, 0)),
                  pl.BlockSpec((None, k2, n), lambda i: (layer, 1, 0)),
                  pl.BlockSpec((tm, n), row), pl.BlockSpec((1, n), const), pl.BlockSpec((1, n), const)],
        out_specs=[pl.BlockSpec((tm, n), row), pl.BlockSpec((tm, n), row)],
        out_shape=[jax.ShapeDtypeStruct((m, n), F32), jax.ShapeDtypeStruct((m, n), BF16)],
        compiler_params=_cparams("parallel"),
    )(a1, a2, b, b, x, g.reshape(1, n), beta.reshape(1, n))


def _ple_kernel(xb_ref, wg_ref, p_ref, wp_ref, x_ref, o_ref, ob_ref):
    gate = jax.nn.sigmoid(_dot(xb_ref[...], wg_ref[...]))
    proj = _dot(p_ref[...], wp_ref[...])
    y = x_ref[...] + gate * proj
    o_ref[...] = y
    ob_ref[...] = y.astype(BF16)


def ple_update(xb, wg, pb, wp, layer, x, *, tm, tn):
    m, d = xb.shape
    pd = pb.shape[-1]
    return pl.pallas_call(
        _ple_kernel,
        grid=(d // tn, m // tm),
        in_specs=[pl.BlockSpec((tm, d), lambda j, i: (i, 0)),
                  pl.BlockSpec((None, d, tn), lambda j, i: (layer, 0, j)),
                  pl.BlockSpec((None, None, tm, pd), lambda j, i: (layer, 0, i, 0)),
                  pl.BlockSpec((None, pd, tn), lambda j, i: (layer, 0, j)),
                  pl.BlockSpec((tm, tn), lambda j, i: (i, j))],
        out_specs=[pl.BlockSpec((tm, tn), lambda j, i: (i, j)),
                   pl.BlockSpec((tm, tn), lambda j, i: (i, j))],
        out_shape=[jax.ShapeDtypeStruct((m, d), F32), jax.ShapeDtypeStruct((m, d), BF16)],
        compiler_params=_cparams("parallel", "parallel"),
    )(xb, wg, pb, wp, x)


def _ffn_up_kernel(x_ref, wg_ref, wu_ref, cwg_ref, cwu_ref, cbg_ref, cbu_ref, o_ref, eg_ref, eu_ref, *, tm, sub):
    i = pl.program_id(1)

    @pl.when(i == 0)
    def _():
        eg_ref[0:HALO, :] = jnp.zeros((HALO, eg_ref.shape[1]), F32)
        eu_ref[0:HALO, :] = jnp.zeros((HALO, eu_ref.shape[1]), F32)

    @pl.when(i > 0)
    def _():
        eg_ref[0:HALO, :] = eg_ref[tm:tm + HALO, :]
        eu_ref[0:HALO, :] = eu_ref[tm:tm + HALO, :]

    tc = o_ref.shape[1]
    halves = ((0, tc // 2), (tc // 2, tc))

    def project(k, e_ref, w_ref):
        e_ref[HALO + k * sub:HALO + (k + 1) * sub, :] = _dot(x_ref[k * sub:(k + 1) * sub, :], w_ref[...])

    def conv(e_ref, w_ref, b_ref, base, c0, c1):
        acc = b_ref[:, c0:c1] + w_ref[FFN_CONV - 1:FFN_CONV, c0:c1] * e_ref[base:base + sub, c0:c1]
        for d in range(1, FFN_CONV):
            acc = acc + w_ref[FFN_CONV - 1 - d:FFN_CONV - d, c0:c1] * e_ref[base - d:base - d + sub, c0:c1]
        return acc

    def gate_rows(k, c0, c1):
        base = HALO + k * sub
        gate = conv(eg_ref, cwg_ref, cbg_ref, base, c0, c1)
        up = conv(eu_ref, cwu_ref, cbu_ref, base, c0, c1)
        o_ref[k * sub:(k + 1) * sub, c0:c1] = (jax.nn.gelu(gate) * up).astype(o_ref.dtype)

    project(0, eg_ref, wg_ref)
    project(0, eu_ref, wu_ref)
    for k in range(tm // sub):
        more = (k + 1) * sub < tm
        if more:
            project(k + 1, eg_ref, wg_ref)
        gate_rows(k, *halves[0])
        if more:
            project(k + 1, eu_ref, wu_ref)
        gate_rows(k, *halves[1])


def ffn_up(xb, w_up, layer, conv_w, conv_b, *, tm, sub, tc):
    s, d = xb.shape
    two_ff = w_up.shape[2]
    ff = two_ff // 2
    nc = ff // tc
    cb = conv_b.reshape(1, two_ff)
    return pl.pallas_call(
        functools.partial(_ffn_up_kernel, tm=tm, sub=sub),
        grid=(nc, s // tm),
        in_specs=[pl.BlockSpec((tm, d), lambda j, i: (i, 0)),
                  pl.BlockSpec((None, d, tc), lambda j, i: (layer, 0, j)),
                  pl.BlockSpec((None, d, tc), lambda j, i: (layer, 0, j + nc)),
                  pl.BlockSpec((FFN_CONV, tc), lambda j, i: (0, j)),
                  pl.BlockSpec((FFN_CONV, tc), lambda j, i: (0, j + nc)),
                  pl.BlockSpec((1, tc), lambda j, i: (0, j)),
                  pl.BlockSpec((1, tc), lambda j, i: (0, j + nc))],
        out_specs=pl.BlockSpec((tm, tc), lambda j, i: (i, j)),
        out_shape=jax.ShapeDtypeStruct((s, ff), BF16),
        scratch_shapes=[pltpu.VMEM((tm + HALO, tc), F32), pltpu.VMEM((tm + HALO, tc), F32)],
        compiler_params=_cparams("parallel", "arbitrary"),
    )(xb, w_up, w_up, conv_w, conv_w, cb, cb)


def _shift_mix(z, halo, mu, live):
    prev = pltpu.roll(z, 1, 0)
    row0 = lax.broadcasted_iota(jnp.int32, z.shape, 0) == 0
    prev = jnp.where(row0, halo[HALO - 1:HALO, :] * live, prev)
    return z + (prev - z) * mu


def _rwkv_prep_kernel(z_ref, zh_ref, lo_ref, loh_ref, mu_ref, mulo_ref, w0_ref, w2_ref, a0_ref, a2_ref,
                      g2_ref, kk_ref, ka_ref, rk_ref, hsum_ref, hspread_ref,
                      r_o, lw_o, k_o, v_o, kap_o, b_o, g_o, bonus_o):
    live = (pl.program_id(0) > 0).astype(F32)
    z = _shift_mix(z_ref[...], zh_ref[...], mu_ref[...], live)
    lo = _shift_mix(lo_ref[...], loh_ref[...], mulo_ref[...], live)
    r = z[:, 0:A_WIDTH]
    k = z[:, A_WIDTH:2 * A_WIDTH]
    v = z[:, 2 * A_WIDTH:3 * A_WIDTH]
    w = -_softplus(-(w0_ref[...] + _dot(jnp.tanh(lo).astype(BF16), w2_ref[...]))) - 0.5
    lw = -jnp.exp(w)
    a = jax.nn.sigmoid(a0_ref[...] + _dot(lo.astype(BF16), a2_ref[...]))
    g = _dot(jax.nn.sigmoid(lo).astype(BF16), g2_ref[...])
    kk = k * kk_ref[...]
    head_sum = lambda t: _dot_exact_rhs(_dot_exact_rhs(t, hsum_ref[...]), hspread_ref[...])
    kap = kk / jnp.maximum(jnp.sqrt(head_sum(kk * kk)), 1e-12)
    k2 = k * (1.0 + (a - 1.0) * ka_ref[...])
    bonus = head_sum(r * k2 * rk_ref[...]) * v
    b = kap * a
    g_o[...] = g
    bonus_o[...] = bonus
    for h in range(A_HEADS):
        sl = slice(h * A_HEAD, (h + 1) * A_HEAD)
        r_o[h] = r[:, sl]
        lw_o[h] = lw[:, sl]
        k_o[h] = k2[:, sl]
        v_o[h] = v[:, sl]
        kap_o[h] = kap[:, sl]
        b_o[h] = b[:, sl]


def rwkv_prep(z_rkv, z_lo, mu_rkv, mu_lo, w0, w2p, a0, a2p, g2p, k_k, k_a, r_k, *, tm):
    s = z_rkv.shape[0]
    lane_head = jnp.arange(A_WIDTH)[:, None] // A_HEAD
    hsum = (lane_head == jnp.arange(LANES)[None, :]).astype(BF16)
    hspread = hsum.T
    hb = tm // HALO
    row = lambda i: (i, 0)
    halo = lambda i: (jnp.maximum(i * hb - 1, 0), 0)
    const = lambda i: (0, 0)
    hm = jax.ShapeDtypeStruct((A_HEADS, s, A_HEAD), F32)
    hm_spec = pl.BlockSpec((A_HEADS, tm, A_HEAD), lambda i: (0, i, 0))
    full = jax.ShapeDtypeStruct((s, A_WIDTH), F32)
    vec = lambda a: a.reshape(1, -1)
    return pl.pallas_call(
        _rwkv_prep_kernel,
        grid=(s // tm,),
        in_specs=[pl.BlockSpec((tm, 3 * A_WIDTH), row), pl.BlockSpec((HALO, 3 * A_WIDTH), halo),
                  pl.BlockSpec((tm, A_LORA_PAD), row), pl.BlockSpec((HALO, A_LORA_PAD), halo),
                  pl.BlockSpec((1, 3 * A_WIDTH), const), pl.BlockSpec((1, A_LORA_PAD), const),
                  pl.BlockSpec((1, A_WIDTH), const), pl.BlockSpec((A_LORA_PAD, A_WIDTH), const),
                  pl.BlockSpec((1, A_WIDTH), const), pl.BlockSpec((A_LORA_PAD, A_WIDTH), const),
                  pl.BlockSpec((A_LORA_PAD, A_WIDTH), const),
                  pl.BlockSpec((1, A_WIDTH), const), pl.BlockSpec((1, A_WIDTH), const),
                  pl.BlockSpec((1, A_WIDTH), const), pl.BlockSpec((A_WIDTH, LANES), const),
                  pl.BlockSpec((LANES, A_WIDTH), const)],
        out_specs=[hm_spec] * 6 + [pl.BlockSpec((tm, A_WIDTH), row)] * 2,
        out_shape=[hm] * 6 + [full] * 2,
        compiler_params=_cparams("parallel"),
    )(z_rkv, z_rkv, z_lo, z_lo, vec(mu_rkv), vec(mu_lo), vec(w0), w2p, vec(a0), a2p, g2p,
      vec(k_k), vec(k_a), vec(r_k), hsum, hspread)


_BNN = (((2,), (1,)), ((0,), (0,)))
_BNT = (((2,), (2,)), ((0,), (0,)))
_BTN = (((1,), (1,)), ((0,), (0,)))


def _tri_inverse(a, row, col):
    mm1 = lambda p, q: _dot(p.astype(BF16), q.astype(BF16), _BNN)
    mm3 = lambda p, q: _dot3(p, q, _BNN)
    eye = (row == col).astype(F32)
    ad = jnp.where((row >> 3) == (col >> 3), a, 0.0)
    t = eye - ad
    a2 = mm1(ad, ad)
    t = t + mm1(t, a2)
    a4 = mm1(a2, a2)
    t = t + mm1(t, a4)
    for sh, mm in ((3, mm1), (4, mm3), (5, mm3)):
        inner = (row >> sh) == (col >> sh)
        outer = (row >> (sh + 1)) == (col >> (sh + 1))
        aoff = jnp.where(jnp.logical_and(outer, jnp.logical_not(inner)), a, 0.0)
        t = t - mm(mm(t, aoff), t)
    return t


def _rwkv_chunk_kernel(r_ref, lw_ref, k_ref, v_ref, kap_ref, b_ref, y_ref, state_ref, *, nch):
    c = CHUNK
    nb = A_HEADS * nch

    @pl.when(pl.program_id(0) == 0)
    def _():
        state_ref[...] = jnp.zeros_like(state_ref)

    row = lax.broadcasted_iota(jnp.int32, (1, c, c), 1)
    col = lax.broadcasted_iota(jnp.int32, (1, c, c), 2)
    tril = row >= col
    stril = row > col
    eye = row == col
    lower_ones = jnp.broadcast_to(tril.astype(BF16), (nb, c, c))

    load = lambda ref: ref[...].reshape(nb, c, A_HEAD)
    r, lw, k, v, kap, b = load(r_ref), load(lw_ref), load(k_ref), load(v_ref), load(kap_ref), load(b_ref)
    hi = lw.astype(BF16)
    r1 = lw - hi.astype(F32)
    mid = r1.astype(BF16)
    lo = (r1 - mid.astype(F32)).astype(BF16)
    ci = _dot(lower_ones, hi, _BNN) + (_dot(lower_ones, mid, _BNN) + _dot(lower_ones, lo, _BNN))
    ce = ci - lw
    cend = ci[:, c - 1:c, :]
    gn = jnp.exp(-ci)
    gend = jnp.exp(cend - ci)
    kap_h = kap * jnp.exp(ce)
    r_h = r * jnp.exp(ci)
    b_h = b * gn
    k_h = k * gn
    b_t = b * gend
    k_t = k * gend
    p = _dot3(jnp.concatenate([kap_h, r_h], axis=1), jnp.concatenate([b_h, k_h], axis=1), _BNT)
    a_ab = jnp.where(stril, p[:, :c, :c], 0.0)
    a_ak = jnp.where(stril, p[:, :c, c:], 0.0)
    r_b = jnp.where(tril, p[:, c:, :c], 0.0)
    r_k = jnp.where(tril, p[:, c:, c:], 0.0)
    t = _tri_inverse(a_ab, row, col)
    x = _dot3(t, jnp.concatenate([kap_h, _dot3(a_ak, v, _BNN)], axis=2), _BNN)
    wr = jnp.concatenate([x[:, :, :A_HEAD], r_h], axis=1)
    rbk = jnp.concatenate([r_b, r_k], axis=2)
    btk = jnp.concatenate([b_t, k_t, jnp.where(eye, jnp.exp(cend), 0.0)], axis=1)

    pick = lambda a, ch: a.reshape(A_HEADS, nch, *a.shape[1:])[:, ch]
    m = state_ref[...]
    for ch in range(nch):
        wm = _dot3(pick(wr, ch), m, _BNN)
        u = -(wm[:, :c] + pick(x, ch)[:, :, A_HEAD:])
        uv = jnp.concatenate([u, pick(v, ch)], axis=1)
        y = wm[:, c:] + _dot3(pick(rbk, ch), uv, _BNN)
        m = _dot3(pick(btk, ch), jnp.concatenate([uv, m], axis=1), _BTN)
        ym = jnp.mean(y, axis=-1, keepdims=True)
        yc = y - ym
        yv = jnp.mean(yc * yc, axis=-1, keepdims=True)
        y_ref[:, ch * c:(ch + 1) * c, :] = yc * lax.rsqrt(yv + A_GN_EPS)
    state_ref[...] = m


RWKV_CHUNKS_PER_STEP = 2


def rwkv_chunks(r, lw, k, v, kap, b):
    _, s, _ = r.shape
    nch = RWKV_CHUNKS_PER_STEP
    spec = pl.BlockSpec((A_HEADS, nch * CHUNK, A_HEAD), lambda n: (0, n, 0))
    return pl.pallas_call(
        functools.partial(_rwkv_chunk_kernel, nch=nch),
        grid=(s // (nch * CHUNK),),
        in_specs=[spec] * 6,
        out_specs=spec,
        out_shape=jax.ShapeDtypeStruct((A_HEADS, s, A_HEAD), F32),
        scratch_shapes=[pltpu.VMEM((A_HEADS, A_HEAD, A_HEAD), F32)],
        compiler_params=_cparams("arbitrary"),
    )(r, lw, k, v, kap, b)


def _rwkv_post_kernel(y_ref, bonus_ref, g_ref, gg_ref, gb_ref, o_ref):
    y = jnp.concatenate([y_ref[h] for h in range(A_HEADS)], axis=1)
    o_ref[...] = ((y * gg_ref[...] + gb_ref[...] + bonus_ref[...]) * g_ref[...]).astype(o_ref.dtype)


def rwkv_post(y, bonus, g, gn_g, gn_b, *, tm):
    _, s, _ = y.shape
    row = lambda i: (i, 0)
    const = lambda i: (0, 0)
    return pl.pallas_call(
        _rwkv_post_kernel,
        grid=(s // tm,),
        in_specs=[pl.BlockSpec((A_HEADS, tm, A_HEAD), lambda i: (0, i, 0)),
                  pl.BlockSpec((tm, A_WIDTH), row), pl.BlockSpec((tm, A_WIDTH), row),
                  pl.BlockSpec((1, A_WIDTH), const), pl.BlockSpec((1, A_WIDTH), const)],
        out_specs=pl.BlockSpec((tm, A_WIDTH), row),
        out_shape=jax.ShapeDtypeStruct((s, A_WIDTH), BF16),
        compiler_params=_cparams("parallel"),
    )(y, bonus, g, gn_g.reshape(1, -1), gn_b.reshape(1, -1))


def _rglru_kernel(xb_ref, gate_ref, halo_ref, cw_ref, cb_ref, wr_ref, br_ref, wi_ref, bi_ref, lam_ref,
                  o_ref, xe_ref, a_ref, u_ref, h_ref, carry_ref, *, tm):
    i = pl.program_id(0)

    @pl.when(i == 0)
    def _():
        carry_ref[...] = jnp.zeros_like(carry_ref)

    xe_ref[0:HALO, :] = halo_ref[...] * (i > 0).astype(F32)
    xe_ref[HALO:, :] = xb_ref[...]
    xc = cb_ref[...] + cw_ref[B_CONV - 1:B_CONV, :] * xe_ref[HALO:, :]
    for d in range(1, B_CONV):
        xc = xc + cw_ref[B_CONV - 1 - d:B_CONV - d, :] * xe_ref[HALO - d:HALO - d + tm, :]
    xcb = xc.astype(BF16)
    r = jax.nn.sigmoid(_dot(xcb, wr_ref[...]) + br_ref[...])
    gi = jax.nn.sigmoid(_dot(xcb, wi_ref[...]) + bi_ref[...])
    log_a = -B_C * r * _softplus(-lam_ref[...])
    a = jnp.exp(log_a)
    a_ref[...] = a
    u_ref[...] = jnp.sqrt(-jnp.tanh(log_a) * (a * a + 1.0)) * (gi * xc)

    def group(gidx, h):
        base = pl.multiple_of(gidx * HALO, HALO)
        a8 = a_ref[pl.ds(base, HALO), :]
        u8 = u_ref[pl.ds(base, HALO), :]
        rows = []
        for rr in range(HALO):
            h = a8[rr:rr + 1, :] * h + u8[rr:rr + 1, :]
            rows.append(h)
        h_ref[pl.ds(base, HALO), :] = jnp.concatenate(rows, axis=0)
        return h

    carry_ref[...] = lax.fori_loop(0, tm // HALO, group, carry_ref[...])
    o_ref[...] = (jax.nn.gelu(gate_ref[...]) * h_ref[...]).astype(o_ref.dtype)


def rglru(z_b, conv_w, conv_b, wr, b_r, wi, b_i, lam, *, tm):
    s = z_b.shape[0]
    hb = tm // HALO
    row = lambda i: (i, 0)
    const = lambda i: (0, 0)
    vec = lambda a: a.reshape(1, -1)
    return pl.pallas_call(
        functools.partial(_rglru_kernel, tm=tm),
        grid=(s // tm,),
        in_specs=[pl.BlockSpec((tm, B_WIDTH), row), pl.BlockSpec((tm, B_WIDTH), lambda i: (i, 1)),
                  pl.BlockSpec((HALO, B_WIDTH), lambda i: (jnp.maximum(i * hb - 1, 0), 0)),
                  pl.BlockSpec((B_CONV, B_WIDTH), const), pl.BlockSpec((1, B_WIDTH), const),
                  pl.BlockSpec((B_WIDTH, B_WIDTH), const), pl.BlockSpec((1, B_WIDTH), const),
                  pl.BlockSpec((B_WIDTH, B_WIDTH), const), pl.BlockSpec((1, B_WIDTH), const),
                  pl.BlockSpec((1, B_WIDTH), const)],
        out_specs=pl.BlockSpec((tm, B_WIDTH), row),
        out_shape=jax.ShapeDtypeStruct((s, B_WIDTH), BF16),
        scratch_shapes=[pltpu.VMEM((tm + HALO, B_WIDTH), F32), pltpu.VMEM((tm, B_WIDTH), F32),
                        pltpu.VMEM((tm, B_WIDTH), F32), pltpu.VMEM((tm, B_WIDTH), F32),
                        pltpu.VMEM((1, B_WIDTH), F32)],
        compiler_params=_cparams("arbitrary"),
    )(z_b, z_b, z_b, conv_w, vec(conv_b), wr, vec(b_r), wi, vec(b_i), vec(lam))


ODD_PAD = 1024
ODD_KIDX_AT = 768
ODD_WIDX_AT = 896
QB = 256
SCORE_SCALE = (IDX_HEADS ** -0.5) * (IDX_DIM ** -0.5)
INT_MIN = -(2 ** 31)
CHUNK_SHIFT = 6
CUT_BITS = 14
COUNT_ROWS = 32
HEAD_GROUP = 2
FAR_TILES = 2
SEL_TILES = 4


def _dsa_in_kernel(x_ref, w_ref, qn_ref, kvn_ref, kg_ref, kb_ref, cq_o, ckv_o, kidx_o, widx_o):
    acc = _dot(x_ref[...], w_ref[...])
    cq = acc[:, 0:C_Q_RANK]
    ckv = acc[:, C_Q_RANK:C_Q_RANK + C_KV_RANK]
    kidx = acc[:, ODD_KIDX_AT:ODD_KIDX_AT + IDX_DIM]
    widx = acc[:, ODD_WIDX_AT:ODD_WIDX_AT + IDX_HEADS]
    rms = lambda t, g: t * lax.rsqrt(jnp.mean(t * t, axis=-1, keepdims=True) + 1e-6) * g
    cq_o[...] = rms(cq, qn_ref[...]).astype(BF16)
    ckv_o[...] = rms(ckv, kvn_ref[...]).astype(BF16)
    kidx_o[...] = _layer_norm_rows(kidx, kg_ref[...], kb_ref[...], LN_EPS).astype(BF16)
    widx_o[...] = widx * SCORE_SCALE


def dsa_in(xb, w_pad, q_norm, kv_norm, kidx_g, kidx_b, *, tm):
    s, d = xb.shape
    row = lambda i: (i, 0)
    const = lambda i: (0, 0)
    vec = lambda a: a.reshape(1, -1)
    return pl.pallas_call(
        _dsa_in_kernel,
        grid=(s // tm,),
        in_specs=[pl.BlockSpec((tm, d), row), pl.BlockSpec((d, ODD_PAD), const),
                  pl.BlockSpec((1, C_Q_RANK), const), pl.BlockSpec((1, C_KV_RANK), const),
                  pl.BlockSpec((1, IDX_DIM), const), pl.BlockSpec((1, IDX_DIM), const)],
        out_specs=[pl.BlockSpec((tm, C_Q_RANK), row), pl.BlockSpec((tm, C_KV_RANK), row),
                   pl.BlockSpec((tm, IDX_DIM), row), pl.BlockSpec((tm, IDX_HEADS), row)],
        out_shape=[jax.ShapeDtypeStruct((s, C_Q_RANK), BF16), jax.ShapeDtypeStruct((s, C_KV_RANK), BF16),
                   jax.ShapeDtypeStruct((s, IDX_DIM), BF16), jax.ShapeDtypeStruct((s, IDX_HEADS), F32)],
        compiler_params=_cparams("parallel"),
    )(xb, w_pad, vec(q_norm), vec(kv_norm), vec(kidx_g), vec(kidx_b))


def _qabs_kernel(cq_ref, wuq_ref, wuk_ref, o_ref):
    q = _dot(cq_ref[...], wuq_ref[...]).astype(BF16)
    for h in range(C_HEADS):
        qa = _dot(q[:, h * C_HEAD_DIM:(h + 1) * C_HEAD_DIM], wuk_ref[h], _NT)
        o_ref[h] = (qa * (C_HEAD_DIM ** -0.5 * LOG2E)).astype(BF16)


def dsa_qabs(cq, w_uq, w_uk, *, tm):
    s = cq.shape[0]
    return pl.pallas_call(
        _qabs_kernel,
        grid=(s // tm,),
        in_specs=[pl.BlockSpec((tm, C_Q_RANK), lambda i: (i, 0)),
                  pl.BlockSpec((C_Q_RANK, C_HEADS * C_HEAD_DIM), lambda i: (0, 0)),
                  pl.BlockSpec((C_HEADS, C_KV_RANK, C_HEAD_DIM), lambda i: (0, 0, 0))],
        out_specs=pl.BlockSpec((C_HEADS, tm, C_KV_RANK), lambda i: (0, i, 0)),
        out_shape=jax.ShapeDtypeStruct((C_HEADS, s, C_KV_RANK), BF16),
        compiler_params=_cparams("parallel"),
    )(cq, w_uq, w_uk)


def _sortable_key(score):
    bits = lax.bitcast_convert_type(score + 0.0, jnp.int32)
    return jnp.where(bits < 0, bits ^ jnp.int32(0x7FFFFFFF), bits)


def _dsa_attn_kernel(qabs_ref, qidx_ref, widx_t_ref, kidx_ref, ckv_ref, wuv_ref, bias_ref,
                     o_ref, keys_ref, acc_ref, m_ref, l_ref, *, k_sel):
    i = pl.program_id(0)
    n_tiles = i + 1
    k_local = lax.broadcasted_iota(jnp.int32, (QB, QB), 0)
    q_local = lax.broadcasted_iota(jnp.int32, (QB, QB), 1)
    allowed_diag_t = (k_local >> CHUNK_SHIFT) <= (q_local >> CHUNK_SHIFT)

    widx_t = widx_t_ref[...]

    def score_keys(off, width):
        kt = kidx_ref[pl.ds(off, width), :]
        sc = jnp.zeros((width, QB), F32)
        for j in range(IDX_HEADS):
            d = _dot(kt, qidx_ref[:, j * IDX_DIM:(j + 1) * IDX_DIM], _NT)
            sc = sc + widx_t[j:j + 1, :] * jnp.maximum(d, 0.0)
        return _sortable_key(sc)

    wide = SEL_TILES * QB

    def score_pair(tp, carry):
        off = pl.multiple_of(tp * wide, wide)
        keys_ref[pl.ds(off, wide), :] = score_keys(off, wide)
        return carry

    def score_tile(t, carry):
        off = pl.multiple_of(t * QB, QB)
        key = score_keys(off, QB)
        keys_ref[pl.ds(off, QB), :] = jnp.where(jnp.logical_or(t < i, allowed_diag_t), key, jnp.int32(INT_MIN))
        return carry

    lax.fori_loop(0, i // SEL_TILES, score_pair, 0)
    lax.fori_loop(SEL_TILES * (i // SEL_TILES), n_tiles, score_tile, 0)

    def selected(kt, off, thr, cut):
        pos = lax.broadcasted_iota(jnp.int32, kt.shape, 0) + off
        return jnp.logical_or(kt > thr, jnp.logical_and(kt == thr, pos < cut))

    def count(pred):
        def hits(off, width, acc):
            hit = pred(keys_ref[pl.ds(off, width), :], off).astype(jnp.int32)
            return acc + jnp.sum(hit.reshape(width // COUNT_ROWS, COUNT_ROWS, QB), axis=0)

        def pair(tp, acc):
            return hits(pl.multiple_of(tp * wide, wide), wide, acc)

        def single(t, acc):
            return hits(pl.multiple_of(t * QB, QB), QB, acc)

        acc = lax.fori_loop(0, n_tiles // SEL_TILES, pair, jnp.zeros((COUNT_ROWS, QB), jnp.int32))
        acc = lax.fori_loop(SEL_TILES * (n_tiles // SEL_TILES), n_tiles, single, acc)
        return jnp.sum(acc.astype(F32), axis=0, keepdims=True).astype(jnp.int32)

    def thr_bit(it, thr):
        cand = thr + jnp.left_shift(jnp.int32(1), 31 - it)
        cnt = count(lambda kt, off: kt >= cand)
        return jnp.where(cnt >= k_sel, cand, thr)

    thr = lax.fori_loop(0, 32, thr_bit, jnp.full((1, QB), INT_MIN, jnp.int32))

    def cut_bit(it, cut):
        cand = cut + jnp.left_shift(jnp.int32(1), CUT_BITS - 1 - it)
        cnt = count(lambda kt, off: selected(kt, off, thr, cand))
        return jnp.where(cnt <= k_sel, cand, cut)

    n_ge = count(lambda kt, off: kt >= thr)
    tied = jnp.logical_and(n_ge > k_sel, thr > INT_MIN)
    any_tied = jnp.max(jnp.where(tied, 1.0, 0.0)) > 0.0
    cut = lax.cond(any_tied,
                   lambda: lax.fori_loop(0, CUT_BITS, cut_bit, jnp.zeros((1, QB), jnp.int32)),
                   lambda: jnp.full((1, QB), 2 ** CUT_BITS - 1, jnp.int32))
    rep_lanes = lambda a, n: jnp.concatenate([a] * (n // LANES), axis=-1)

    m_ref[...] = jnp.full(m_ref.shape, NEG_BIG, F32)
    l_ref[...] = jnp.zeros(l_ref.shape, F32)
    acc_ref[...] = jnp.zeros(acc_ref.shape, F32)

    def attend(off, width, near):
        sel = selected(keys_ref[pl.ds(off, width), :], off, thr, cut)
        if near == 1:
            sel = jnp.logical_and(sel, allowed_diag_t)
        mask_add = jnp.where(sel, 0.0, NEG_BIG).T[None]
        kv = ckv_ref[pl.ds(off, width), :]
        rep = lambda a, n: jnp.concatenate([a] * (n // LANES), axis=-1)

        for g in range(C_HEADS // HEAD_GROUP):
            hs = slice(g * HEAD_GROUP, (g + 1) * HEAD_GROUP)
            q = qabs_ref[hs].reshape(HEAD_GROUP * QB, C_KV_RANK)
            s = _dot(q, kv, _NT).reshape(HEAD_GROUP, QB, width) + mask_add
            if near is not None:
                s = s + bias_ref[near, hs]
            m_old = m_ref[hs]
            row_max = jnp.broadcast_to(jnp.max(s, axis=-1, keepdims=True), m_old.shape)
            m_new = jnp.maximum(m_old, row_max)
            alpha = jnp.exp2(m_old - m_new)
            p = jnp.exp2(s - rep(m_new, width))
            row_sum = jnp.broadcast_to(jnp.sum(p, axis=-1, keepdims=True), m_old.shape)
            l_ref[hs] = alpha * l_ref[hs] + row_sum
            pv = _dot(p.astype(BF16).reshape(HEAD_GROUP * QB, width), kv).reshape(HEAD_GROUP, QB, C_KV_RANK)
            acc_ref[hs] = rep(alpha, C_KV_RANK) * acc_ref[hs] + pv
            m_ref[hs] = m_new

    n_far = jnp.maximum(i - 1, 0)

    n_wide = n_far // FAR_TILES

    def far_wide(tw, carry):
        attend(pl.multiple_of(tw * (FAR_TILES * QB), FAR_TILES * QB), FAR_TILES * QB, None)
        return carry

    def far_single(t, carry):
        attend(pl.multiple_of(t * QB, QB), QB, None)
        return carry

    lax.fori_loop(0, n_wide, far_wide, 0)
    lax.fori_loop(n_wide * FAR_TILES, n_far, far_single, 0)

    @pl.when(i >= 1)
    def _():
        attend(pl.multiple_of((i - 1) * QB, QB), QB, 0)

    attend(pl.multiple_of(i * QB, QB), QB, 1)

    for h in range(C_HEADS):
        o_lat = (acc_ref[h] / rep_lanes(l_ref[h], C_KV_RANK)).astype(BF16)
        o_ref[:, h * C_HEAD_DIM:(h + 1) * C_HEAD_DIM] = _dot(o_lat, wuv_ref[h]).astype(o_ref.dtype)


def dsa_attention(qabs, qidx, widx_t, kidx, ckv, w_uv, bias_near, *, k_sel):
    _, s, _ = qabs.shape
    full2 = lambda i: (0, 0)
    once = pl.Buffered(1)
    return pl.pallas_call(
        functools.partial(_dsa_attn_kernel, k_sel=k_sel),
        grid=(s // QB,),
        in_specs=[pl.BlockSpec((C_HEADS, QB, C_KV_RANK), lambda i: (0, i, 0)),
                  pl.BlockSpec((QB, IDX_HEADS * IDX_DIM), lambda i: (i, 0)),
                  pl.BlockSpec((IDX_HEADS, QB), lambda i: (0, i)),
                  pl.BlockSpec((s, IDX_DIM), full2, pipeline_mode=once),
                  pl.BlockSpec((s, C_KV_RANK), full2, pipeline_mode=once),
                  pl.BlockSpec((C_HEADS, C_KV_RANK, C_HEAD_DIM), lambda i: (0, 0, 0), pipeline_mode=once),
                  pl.BlockSpec((2, C_HEADS, QB, QB), lambda i: (0, 0, 0, 0), pipeline_mode=once)],
        out_specs=pl.BlockSpec((QB, C_HEADS * C_HEAD_DIM), lambda i: (i, 0)),
        out_shape=jax.ShapeDtypeStruct((s, C_HEADS * C_HEAD_DIM), BF16),
        scratch_shapes=[pltpu.VMEM((s, QB), jnp.int32),
                        pltpu.VMEM((C_HEADS, QB, C_KV_RANK), F32),
                        pltpu.VMEM((C_HEADS, QB, LANES), F32),
                        pltpu.VMEM((C_HEADS, QB, LANES), F32)],
        compiler_params=_cparams("parallel"),
    )(qabs, qidx, widx_t, kidx, ckv, w_uv, bias_near)


def _t5_bucket(rel):
    nb = REL_BUCKETS // 2
    max_exact = nb // 2
    ret = jnp.where(rel > 0, nb, 0)
    n = jnp.abs(rel)
    nf = jnp.maximum(n, 1).astype(jnp.float32)
    large = max_exact + (jnp.log(nf / max_exact) / math.log(REL_MAX_DIST / max_exact) * (nb - max_exact)).astype(jnp.int32)
    large = jnp.minimum(large, nb - 1)
    return ret + jnp.where(n < max_exact, n, large)


def _near_bias(rel_bias):
    ql = jnp.arange(QB)[:, None]
    sl = jnp.arange(QB)[None, :]
    rel = jnp.stack([sl - ql - QB, sl - ql])
    far = rel_bias[_t5_bucket(jnp.array(-2 * QB))]
    table = ((rel_bias - far) * LOG2E).astype(F32)
    onehot = jax.nn.one_hot(_t5_bucket(rel), REL_BUCKETS, dtype=F32)
    return jnp.einsum('tqsb,bh->thqs', onehot, table, precision=lax.Precision.HIGHEST)


def _pad_rows(w, at, total):
    return jnp.zeros((total, w.shape[1]), w.dtype).at[at:at + w.shape[0]].set(w)


def _block_diag(w):
    n, d, e = w.shape
    eye = jnp.eye(n, dtype=w.dtype)
    return (eye[:, None, :, None] * w[:, :, None, :]).reshape(n * d, n * e)


def _even_mixer(x, xb, w_in, w_out, j, mu, w0, w2, a0, a2, g2, k_k, k_a, r_k, gn_g, gn_b,
                conv_w, conv_b, w_r, b_r, w_i, b_i, lam, ln_g, ln_b, tm):
    w_in = w_in.astype(BF16)
    n_rkv = 3 * A_WIDTH
    z_rkv = matmul(xb, w_in[:, :n_rkv], tm=tm, tn=1024)
    w_lo = jnp.pad(w_in[:, n_rkv:A_COLS], ((0, 0), (0, A_LORA_PAD - A_LORA)))
    z_lo = matmul(xb, w_lo, tm=tm, tn=A_LORA_PAD)
    z_b = matmul(xb, w_in[:, A_COLS:], tm=tm, tn=1024)

    mu_lo = jnp.pad(mu[n_rkv:], (0, A_LORA_PAD - A_LORA))
    w2p = _pad_rows(w2, 0, A_LORA_PAD).astype(BF16)
    a2p = _pad_rows(a2, A_DECAY_LORA, A_LORA_PAD).astype(BF16)
    g2p = _pad_rows(g2, A_DECAY_LORA + A_ICL_LORA, A_LORA_PAD).astype(BF16)
    r, lw, k2, v, kap, b, g, bonus = rwkv_prep(
        z_rkv, z_lo, mu[:n_rkv], mu_lo, w0, w2p, a0, a2p, g2p, k_k, k_a, r_k.reshape(-1), tm=min(tm, 256))
    y = rwkv_chunks(r, lw, k2, v, kap, b)
    y_a = rwkv_post(y, bonus, g, gn_g, gn_b, tm=tm)

    y_b = rglru(z_b, conv_w, conv_b, _block_diag(w_r).astype(BF16), b_r,
                _block_diag(w_i).astype(BF16), b_i, lam, tm=min(tm, 256))
    return matmul2_residual_ln(y_a, y_b, w_out, j, x, ln_g, ln_b, tm=tm)


def _odd_mixer(x, xb, w_in, w_out, j, q_norm, kv_norm, w_uq, w_uk, w_uv, w_qidx, kidx_g, kidx_b,
               bias_near, ln_g, ln_b, tm):
    s = x.shape[0]
    d = w_in.shape[0]
    n_qkv = C_Q_RANK + C_KV_RANK
    w_pad = jnp.zeros((d, ODD_PAD), BF16)
    w_pad = w_pad.at[:, :n_qkv].set(w_in[:, :n_qkv].astype(BF16))
    w_pad = w_pad.at[:, ODD_KIDX_AT:ODD_KIDX_AT + IDX_DIM].set(w_in[:, n_qkv:n_qkv + IDX_DIM].astype(BF16))
    w_pad = w_pad.at[:, ODD_WIDX_AT:ODD_WIDX_AT + IDX_HEADS].set(w_in[:, n_qkv + IDX_DIM:].astype(BF16))
    cq, ckv, kidx, widx = dsa_in(xb, w_pad, q_norm, kv_norm, kidx_g, kidx_b, tm=tm)
    qabs = dsa_qabs(cq, w_uq.astype(BF16), w_uk.astype(BF16), tm=tm)
    qidx = matmul(cq, w_qidx.astype(BF16), tm=tm, tn=IDX_HEADS * IDX_DIM, out_dtype=BF16)
    o = dsa_attention(qabs, qidx, widx.T, kidx, ckv, w_uv.astype(BF16), bias_near,
                      k_sel=min(TOPK_MAX, s // 4))
    return matmul_residual_ln(o, w_out, j, x, ln_g, ln_b, tm=tm, tk=o.shape[1])


def kernel(x, p, rel_bias, ln1_g, ln1_b, ln2_g, ln2_b, ffn_w_up, ffn_conv_w, ffn_conv_b, ffn_w_down, ple_w_proj, ple_w_gate, ev_w_in, ev_w_out, a_mu, a_w0, a_w2, a_a0, a_a2, a_g2, a_k_k, a_k_a, a_r_k, a_gn_g, a_gn_b, b_conv_w, b_conv_b, b_w_r, b_b_r, b_w_i, b_b_i, b_lambda, od_w_in, od_w_out, c_q_norm, c_kv_norm, c_w_uq, c_w_uk, c_w_uv, c_w_qidx, c_kidx_g, c_kidx_b):
    bsz, s, d = x.shape
    assert bsz == 1 and s % QB == 0 and s <= 2 ** CUT_BITS
    tm = min(512, s)
    x = x[0]
    xb = x.astype(BF16)
    bias_near = _near_bias(rel_bias)
    ev_w_out_b, od_w_out_b = ev_w_out.astype(BF16), od_w_out.astype(BF16)
    w_up_b, w_down_b = ffn_w_up.astype(BF16), ffn_w_down.astype(BF16)
    w_gate_b, w_proj_b, p_b = ple_w_gate.astype(BF16), ple_w_proj.astype(BF16), p.astype(BF16)
    for layer in range(DEPTH):
        j = layer // 2
        if layer % 2 == 0:
            x, xb = _even_mixer(x, xb, ev_w_in[j], ev_w_out_b, j, a_mu[j], a_w0[j], a_w2[j], a_a0[j], a_a2[j],
                                a_g2[j], a_k_k[j], a_k_a[j], a_r_k[j], a_gn_g[j], a_gn_b[j],
                                b_conv_w[j], b_conv_b[j], b_w_r[j], b_b_r[j], b_w_i[j], b_b_i[j], b_lambda[j],
                                ln1_g[layer], ln1_b[layer], tm)
        else:
            x, xb = _odd_mixer(x, xb, od_w_in[j], od_w_out_b, j, c_q_norm[j], c_kv_norm[j], c_w_uq[j], c_w_uk[j],
                               c_w_uv[j], c_w_qidx[j], c_kidx_g[j], c_kidx_b[j], bias_near,
                               ln1_g[layer], ln1_b[layer], tm)
        hm = ffn_up(xb, w_up_b, layer, ffn_conv_w[layer], ffn_conv_b[layer], tm=min(4 * tm, s), sub=min(2 * tm, s), tc=512)
        x, xb = matmul_residual_ln(hm, w_down_b, layer, x, ln2_g[layer], ln2_b[layer], tm=tm, tk=D_FF // 2)
        x, xb = ple_update(xb, w_gate_b, p_b, w_proj_b, layer, x, tm=min(2 * tm, s), tn=1024)
    return x[None]
```
